```python
import functools
import numpy as np
import jax
import jax.numpy as jnp
from jax import lax

D_MODEL = 4096
BATCH = 4
SEQ = 2048
DEPTH = 2

GRID_W = 64
CTX_LEN = 256
CHUNK = 128
EPS = 1e-6
ROPE_BASE = 10000.0
N_EVEN = (DEPTH + 1) // 2
N_ODD = DEPTH // 2
RET_HEADS = 8
RET_DV = D_MODEL // (2 * RET_HEADS)
RET_DK = RET_DV // 2
MLSTM_HEADS = 8
MLSTM_DV = D_MODEL // (2 * MLSTM_HEADS)
MLSTM_DK = MLSTM_DV // 2
CONV_K = 3
A_QK = RET_HEADS * RET_DK
A_V = RET_HEADS * RET_DV
B_QK = MLSTM_HEADS * MLSTM_DK
B_V = MLSTM_HEADS * MLSTM_DV
N_GATE = 4 * MLSTM_HEADS
IN_SIZES = (A_QK, A_QK, A_V, A_V, 2 * B_QK, B_V, B_V, N_GATE)
IN_COLS = sum(IN_SIZES)
MIX_W = A_V + B_V
RWKV_N = 64
RWKV_HEADS = D_MODEL // RWKV_N
DECAY_LORA = max(32, int(round(1.8 * D_MODEL ** 0.5 / 32)) * 32)
AAA_LORA = DECAY_LORA
GATE_LORA = max(32, int(round(0.6 * D_MODEL ** 0.8 / 32)) * 32)
LNX_EPS = 64e-5
D_FF = 4 * D_MODEL

kernel_name = 'hybrid_retention_mlstm_rwkv7_prefix_dit'


def rms_norm(x, g):
    xf = x.astype(jnp.float32)
    y = xf * lax.rsqrt(jnp.mean(jnp.square(xf), axis=-1, keepdims=True) + EPS)
    return (y * g.astype(jnp.float32)).astype(x.dtype)


def modulate(h, g, shift, scale):
    return rms_norm(h, g) * (1 + scale) + shift


def to_heads(t, n_heads):
    b, s, _ = t.shape
    return t.reshape(b, s, n_heads, -1).transpose(0, 2, 1, 3)


def from_heads(t):
    b, h, s, d = t.shape
    return t.transpose(0, 2, 1, 3).reshape(b, s, h * d)


def head_rms_norm(t, g):
    tf = t.astype(jnp.float32)
    y = tf * lax.rsqrt(jnp.mean(jnp.square(tf), axis=-1, keepdims=True) + EPS)
    return from_heads(y) * g.astype(jnp.float32)


def rope_1d(t, pos):
    half = t.shape[-1] // 2
    inv = jnp.power(ROPE_BASE, -jnp.arange(half, dtype=jnp.float32) / half)
    ang = pos[:, None] * inv[None, :]
    cos, sin = jnp.cos(ang), jnp.sin(ang)
    tf = t.astype(jnp.float32)
    t1, t2 = tf[..., :half], tf[..., half:]
    return jnp.concatenate([t1 * cos - t2 * sin, t1 * sin + t2 * cos], axis=-1).astype(t.dtype)


def axial_rope(t, rows, cols):
    half = t.shape[-1] // 2
    return jnp.concatenate([rope_1d(t[..., :half], rows), rope_1d(t[..., half:], cols)], axis=-1)


def to_chunks(t):
    b, h, s = t.shape[:3]
    t = t.reshape((b, h, s // CHUNK, CHUNK) + t.shape[3:])
    return jnp.moveaxis(t, 2, 0)


def from_chunks(t):
    t = jnp.moveaxis(t, 0, 2)
    b, h, n, l = t.shape[:4]
    return t.reshape((b, h, n * l) + t.shape[4:])


def retention_log_decay():
    h = jnp.arange(RET_HEADS, dtype=jnp.float32) / max(RET_HEADS - 1, 1)
    return jnp.log1p(-jnp.exp2(-(5.0 + 7.0 * h)))


def retention_chunked(q, k, v, state, log_gamma):
    dtype = v.dtype
    q, k, v = q.astype(jnp.float32), k.astype(jnp.float32) * RET_DK ** -0.5, v.astype(jnp.float32)
    pos = jnp.arange(CHUNK, dtype=jnp.float32)
    diff = pos[:, None] - pos[None, :]
    intra = jnp.where(diff >= 0, jnp.exp(jnp.maximum(diff, 0.0)[None] * log_gamma[:, None, None]), 0.0)
    q_dec = jnp.exp((pos[None, :] + 1.0) * log_gamma[:, None])[..., None]
    k_dec = jnp.exp((CHUNK - 1.0 - pos[None, :]) * log_gamma[:, None])[..., None]
    c_dec = jnp.exp(CHUNK * log_gamma)[:, None, None]

    def step(s, blk):
        qc, kc, vc = blk
        sc = jnp.einsum('bhid,bhjd->bhij', qc, kc) * intra
        o = jnp.einsum('bhij,bhjv->bhiv', sc, vc) + jnp.einsum('bhid,bhdv->bhiv', qc * q_dec, s)
        s = s * c_dec + jnp.einsum('bhjd,bhjv->bhdv', kc * k_dec, vc)
        return s, o

    s_fin, o = lax.scan(step, state, (to_chunks(q), to_chunks(k), to_chunks(v)))
    return from_chunks(o).astype(dtype), s_fin


def mlstm_chunked(q, k, v, i_pre, log_f, state):
    dtype = v.dtype
    q, k, v = q.astype(jnp.float32), k.astype(jnp.float32) * MLSTM_DK ** -0.5, v.astype(jnp.float32)
    causal = jnp.tril(jnp.ones((CHUNK, CHUNK), dtype=bool))

    def step(carry, blk):
        c_mem, n_mem, m = carry
        qc, kc, vc, ic, fc = blk
        b = jnp.cumsum(fc, axis=-1)
        d_log = jnp.where(causal, b[..., :, None] - b[..., None, :] + ic[..., None, :], -jnp.inf)
        m_inter = b + m[..., None]
        m_row = jnp.maximum(jnp.max(d_log, axis=-1), m_inter)
        w_intra = jnp.exp(d_log - m_row[..., None])
        w_inter = jnp.exp(m_inter - m_row)
        sc = jnp.einsum('bhid,bhjd->bhij', qc, kc) * w_intra
        num = jnp.einsum('bhij,bhjv->bhiv', sc, vc) + w_inter[..., None] * jnp.einsum('bhid,bhdv->bhiv', qc, c_mem)
        den = jnp.sum(sc, axis=-1) + w_inter * jnp.einsum('bhid,bhd->bhi', qc, n_mem)
        h = num / jnp.maximum(jnp.abs(den), jnp.exp(-m_row))[..., None]
        b_last = b[..., -1]
        g = b_last[..., None] - b + ic
        m_new = jnp.maximum(b_last + m, jnp.max(g, axis=-1))
        w_k = jnp.exp(g - m_new[..., None])
        w_c = jnp.exp(b_last + m - m_new)
        c_mem = w_c[..., None, None] * c_mem + jnp.einsum('bhjd,bhjv->bhdv', kc * w_k[..., None], vc)
        n_mem = w_c[..., None] * n_mem + jnp.einsum('bhjd,bhj->bhd', kc, w_k)
        return (c_mem, n_mem, m_new), h

    blks = (to_chunks(q), to_chunks(k), to_chunks(v),
            to_chunks(i_pre.astype(jnp.float32)), to_chunks(log_f.astype(jnp.float32)))
    s_fin, h = lax.scan(step, state, blks)
    return from_chunks(h).astype(dtype), s_fin


def rwkv7_scan(r, w, k, v, a, b, state):
    dtype = v.dtype
    tm = lambda t: jnp.moveaxis(t.astype(jnp.float32), 2, 0)

    def step(s, inp):
        rt, wt, kt, vt, at, bt = inp
        sa = jnp.einsum('bhvk,bhk->bhv', s, at)
        s = s * wt[:, :, None, :] + sa[..., None] * bt[:, :, None, :] + vt[..., None] * kt[:, :, None, :]
        return s, jnp.einsum('bhvk,bhk->bhv', s, rt)

    s_fin, y = lax.scan(step, state, (tm(r), tm(w), tm(k), tm(v), tm(a), tm(b)))
    return jnp.moveaxis(y, 0, 2).astype(dtype), s_fin


def run_bidir(scan_f, scan_b, ctx_f, lat_f, ctx_b, lat_b, init):
    rev = lambda t: jnp.flip(t, axis=2)
    yc_f, st_f = scan_f(*ctx_f, init)
    yl_f, _ = scan_f(*lat_f, st_f)
    yc_b, st_b = scan_b(*[rev(t) for t in ctx_b], init)
    yl_b, _ = scan_b(*[rev(t) for t in lat_b], st_b)
    return yc_f + rev(yc_b), yl_f + rev(yl_b)


def dwconv_grid(t, w):
    b, s, ch = t.shape
    y = lax.conv_general_dilated(t.reshape(b, s // GRID_W, GRID_W, ch), w[:, :, None, :], (1, 1), 'SAME',
                                 dimension_numbers=('NHWC', 'HWIO', 'NHWC'), feature_group_count=ch)
    return y.reshape(b, s, ch)


def dwconv_seq(t, w):
    ch = t.shape[-1]
    return lax.conv_general_dilated(t, w[CONV_K // 2][:, None, :], (1,), 'SAME',
                                    dimension_numbers=('NWC', 'WIO', 'NWC'), feature_group_count=ch)


def qshift_grid(t):
    b, s, d = t.shape
    g = t.reshape(b, s // GRID_W, GRID_W, d)
    q = d // 4
    left = jnp.pad(g[:, :, :-1, :q], ((0, 0), (0, 0), (1, 0), (0, 0)))
    right = jnp.pad(g[:, :, 1:, q:2 * q], ((0, 0), (0, 0), (0, 1), (0, 0)))
    up = jnp.pad(g[:, :-1, :, 2 * q:3 * q], ((0, 0), (1, 0), (0, 0), (0, 0)))
    down = jnp.pad(g[:, 1:, :, 3 * q:], ((0, 0), (0, 1), (0, 0), (0, 0)))
    return jnp.concatenate([left, right, up, down], axis=-1).reshape(b, s, d)


def shift_seq(t):
    h = t.shape[-1] // 2
    prev = jnp.pad(t[:, :-1, :h], ((0, 0), (1, 0), (0, 0)))
    nxt = jnp.pad(t[:, 1:, h:], ((0, 0), (0, 1), (0, 0)))
    return jnp.concatenate([prev, nxt], axis=-1)


def even_mixer(h_ctx, h_lat, w_in, b_gate, conv_qk, gn_ret, gn_ml, w_out, with_ctx):
    bsz, s_lat, _ = h_lat.shape
    pos = jnp.arange(s_lat)
    rows = (pos // GRID_W).astype(jnp.float32)
    cols = (pos % GRID_W).astype(jnp.float32)
    offsets = [int(o) for o in np.cumsum(IN_SIZES)[:-1]]

    def project(h, conv, rotate):
        aq, ak, av, ag, bqk, bv, bo, gates = jnp.split(h @ w_in, offsets, axis=-1)
        bq, bk = jnp.split(jax.nn.silu(conv(bqk, conv_qk)), 2, axis=-1)
        aq, ak = to_heads(aq, RET_HEADS), to_heads(ak, RET_HEADS)
        if rotate:
            aq, ak = axial_rope(aq, rows, cols), axial_rope(ak, rows, cols)
        g4 = (gates + b_gate).astype(jnp.float32)
        g4 = g4.reshape(h.shape[0], h.shape[1], 4, MLSTM_HEADS).transpose(2, 0, 3, 1)
        qkv_b = (to_heads(bq, MLSTM_HEADS), to_heads(bk, MLSTM_HEADS), to_heads(bv, MLSTM_HEADS))
        ret_args = (aq, ak, to_heads(av, RET_HEADS))
        ml_f = qkv_b + (g4[0], jax.nn.log_sigmoid(g4[1]))
        ml_b = qkv_b + (g4[2], jax.nn.log_sigmoid(g4[3]))
        return ret_args, ml_f, ml_b, ag, bo

    ret_c, mlf_c, mlb_c, ag_c, bo_c = project(h_ctx, dwconv_seq, False)
    ret_l, mlf_l, mlb_l, ag_l, bo_l = project(h_lat, dwconv_grid, True)

    lg = retention_log_decay()
    ret_init = jnp.zeros((bsz, RET_HEADS, RET_DK, RET_DV), jnp.float32)
    yr_c, yr_l = run_bidir(functools.partial(retention_chunked, log_gamma=lg),
                           functools.partial(retention_chunked, log_gamma=lg[::-1]),
                           ret_c, ret_l, ret_c, ret_l, ret_init)
    ml_init = (jnp.zeros((bsz, MLSTM_HEADS, MLSTM_DK, MLSTM_DV), jnp.float32),
               jnp.zeros((bsz, MLSTM_HEADS, MLSTM_DK), jnp.float32),
               jnp.zeros((bsz, MLSTM_HEADS), jnp.float32))
    ym_c, ym_l = run_bidir(mlstm_chunked, mlstm_chunked, mlf_c, mlf_l, mlb_c, mlb_l, ml_init)

    def merge(yr, ym, ag, bo):
        ret = head_rms_norm(yr, gn_ret) * jax.nn.silu(ag.astype(jnp.float32))
        ml = head_rms_norm(ym, gn_ml) * jax.nn.sigmoid(bo.astype(jnp.float32))
        return jnp.concatenate([ret, ml], axis=-1).astype(ag.dtype) @ w_out

    out_ctx = merge(yr_c, ym_c, ag_c, bo_c) if with_ctx else None
    return out_ctx, merge(yr_l, ym_l, ag_l, bo_l)


def rwkv_mixer(h_ctx, h_lat, mu, w_rkv, w0, w1, w2, a0, a1, a2, g1, g2, k_k, k_a, r_k,
               lnx_w, lnx_b, w_out, with_ctx):
    bsz = h_lat.shape[0]

    def prepare(h, shifted):
        xx = shifted - h
        xr, xw, xk, xv, xa, xg = [h + xx * mu[i] for i in range(6)]
        r = xr @ w_rkv[0]
        k = xk @ w_rkv[1]
        v = xv @ w_rkv[2]
        gate = jax.nn.sigmoid(xg @ g1) @ g2
        kk = to_heads(k * k_k, RWKV_HEADS).astype(jnp.float32)
        kk = kk * lax.rsqrt(jnp.maximum(jnp.sum(jnp.square(kk), axis=-1, keepdims=True), 1e-24))
        r_h, v_h = to_heads(r, RWKV_HEADS), to_heads(v, RWKV_HEADS)
        dirs = []
        for d in range(2):
            w_log = -jax.nn.softplus(-(w0[d] + jnp.tanh(xw @ w1[d]) @ w2[d]).astype(jnp.float32)) - 0.5
            decay = jnp.exp(-jnp.exp(w_log))
            a = jax.nn.sigmoid((a0[d] + (xa @ a1[d]) @ a2[d]).astype(jnp.float32))
            k_d = k.astype(jnp.float32) * (1 + (a - 1) * k_a.astype(jnp.float32))
            a_h = to_heads(a, RWKV_HEADS)
            dirs.append((r_h, to_heads(decay, RWKV_HEADS), to_heads(k_d, RWKV_HEADS), v_h, -kk, kk * a_h))
        k_bonus = 0.5 * (dirs[0][2] + dirs[1][2])
        return dirs, gate, k_bonus, r_h, v_h

    dirs_c, gate_c, kb_c, r_c, v_c = prepare(h_ctx, shift_seq(h_ctx))
    dirs_l, gate_l, kb_l, r_l, v_l = prepare(h_lat, qshift_grid(h_lat))
    init = jnp.zeros((bsz, RWKV_HEADS, RWKV_N, RWKV_N), jnp.float32)
    y_c, y_l = run_bidir(rwkv7_scan, rwkv7_scan, dirs_c[0], dirs_l[0], dirs_c[1], dirs_l[1], init)
    r_k_h = r_k.reshape(RWKV_HEADS, 1, RWKV_N).astype(jnp.float32)

    def finish(y, r, v, kb, gate):
        yf = y.astype(jnp.float32)
        mean = jnp.mean(yf, axis=-1, keepdims=True)
        var = jnp.mean(jnp.square(yf - mean), axis=-1, keepdims=True)
        yn = from_heads((yf - mean) * lax.rsqrt(var + LNX_EPS)) * lnx_w + lnx_b
        bonus = jnp.sum(r.astype(jnp.float32) * kb * r_k_h, axis=-1, keepdims=True) * v.astype(jnp.float32)
        out = (yn + from_heads(bonus)) * gate.astype(jnp.float32)
        return out.astype(gate.dtype) @ w_out

    out_ctx = finish(y_c, r_c, v_c, kb_c, gate_c) if with_ctx else None
    return out_ctx, finish(y_l, r_l, v_l, kb_l, gate_l)


def squared_relu_mlp(u, w1, w2):
    return jnp.square(jax.nn.relu(u @ w1)) @ w2


def setup_inputs(seed: int = 0) -> dict:
    key = jax.random.key(seed)
    keys = iter(jax.random.split(key, 48))
    D = D_MODEL

    def nrm(shape, scale):
        return jax.random.normal(next(keys), shape, jnp.float32) * scale

    def unif(shape, lo, hi):
        return jax.random.uniform(next(keys), shape, jnp.float32, minval=lo, maxval=hi)

    fbias = jnp.linspace(3.0, 6.0, MLSTM_HEADS, dtype=jnp.float32)
    ev_b_gate = jnp.concatenate([nrm((N_EVEN, MLSTM_HEADS), 0.1), fbias + nrm((N_EVEN, MLSTM_HEADS), 0.1),
                                 nrm((N_EVEN, MLSTM_HEADS), 0.1), fbias + nrm((N_EVEN, MLSTM_HEADS), 0.1)], axis=-1)
    return {
        'x': nrm((BATCH, SEQ, D), 1.0),
        'c': nrm((BATCH, D), 1.0),
        'ctx': nrm((BATCH, CTX_LEN, D), 1.0),
        'c_ctx': nrm((D,), 1.0),
        'ada_w': nrm((DEPTH, D, 6 * D), D ** -0.5),
        'ada_b': nrm((DEPTH, 6 * D), 0.01),
        'norm_g': 1.0 + nrm((DEPTH, 4, D), 0.1),
        'mlp_w_in': nrm((DEPTH, D, D_FF), D ** -0.5),
        'mlp_w_out': nrm((DEPTH, D_FF, D), D_FF ** -0.5),
        'ev_w_in': nrm((N_EVEN, D, IN_COLS), D ** -0.5),
        'ev_b_gate': ev_b_gate,
        'ev_conv_qk': nrm((N_EVEN, CONV_K, CONV_K, 2 * B_QK), 1.0 / CONV_K),
        'ev_gn_ret': 1.0 + nrm((N_EVEN, A_V), 0.1),
        'ev_gn_mlstm': 1.0 + nrm((N_EVEN, B_V), 0.1),
        'ev_w_out': nrm((N_EVEN, MIX_W, D), MIX_W ** -0.5),
        'od_mu': unif((N_ODD, 6, D), 0.0, 1.0),
        'od_w_rkv': nrm((N_ODD, 3, D, D), D ** -0.5),
        'od_w0': unif((N_ODD, 2, D), -6.0, -1.0),
        'od_w1': nrm((N_ODD, 2, D, DECAY_LORA), D ** -0.5),
        'od_w2': nrm((N_ODD, 2, DECAY_LORA, D), 0.1 * DECAY_LORA ** -0.5),
        'od_a0': nrm((N_ODD, 2, D), 0.1),
        'od_a1': nrm((N_ODD, 2, D, AAA_LORA), D ** -0.5),
        'od_a2': nrm((N_ODD, 2, AAA_LORA, D), AAA_LORA ** -0.5),
        'od_g1': nrm((N_ODD, D, GATE_LORA), D ** -0.5),
        'od_g2': nrm((N_ODD, GATE_LORA, D), GATE_LORA ** -0.5),
        'od_k_k': 0.85 + nrm((N_ODD, D), 0.05),
        'od_k_a': 1.0 + nrm((N_ODD, D), 0.05),
        'od_r_k': nrm((N_ODD, D), 0.1),
        'od_lnx_w': 1.0 + nrm((N_ODD, D), 0.1),
        'od_lnx_b': nrm((N_ODD, D), 0.01),
        'od_w_out': nrm((N_ODD, D, D), D ** -0.5),
    }


def reference(x, c, ctx, c_ctx, ada_w, ada_b, norm_g, mlp_w_in, mlp_w_out, ev_w_in, ev_b_gate, ev_conv_qk,
              ev_gn_ret, ev_gn_mlstm, ev_w_out, od_mu, od_w_rkv, od_w0, od_w1, od_w2, od_a0, od_a1, od_a2,
              od_g1, od_g2, od_k_k, od_k_a, od_r_k, od_lnx_w, od_lnx_b, od_w_out):
    cond_lat = jax.nn.silu(c)[:, None, :]
    cond_ctx = jax.nn.silu(c_ctx)[None, None, :]
    h_lat, h_ctx = x, ctx
    for layer in range(DEPTH):
        with_ctx = layer < DEPTH - 1
        ml = jnp.split(cond_lat @ ada_w[layer] + ada_b[layer], 6, axis=-1)
        mc = jnp.split(cond_ctx @ ada_w[layer] + ada_b[layer], 6, axis=-1)
        g_pre_mix, g_post_mix, g_pre_ff, g_post_ff = norm_g[layer, 0], norm_g[layer, 1], norm_g[layer, 2], norm_g[layer, 3]
        u_lat = modulate(h_lat, g_pre_mix, ml[0], ml[1])
        u_ctx = modulate(h_ctx, g_pre_mix, mc[0], mc[1])
        if layer % 2 == 0:
            e = layer // 2
            o_ctx, o_lat = even_mixer(u_ctx, u_lat, ev_w_in[e], ev_b_gate[e], ev_conv_qk[e], ev_gn_ret[e],
                                      ev_gn_mlstm[e], ev_w_out[e], with_ctx)
        else:
            o = layer // 2
            o_ctx, o_lat = rwkv_mixer(u_ctx, u_lat, od_mu[o], od_w_rkv[o], od_w0[o], od_w1[o], od_w2[o],
                                      od_a0[o], od_a1[o], od_a2[o], od_g1[o], od_g2[o], od_k_k[o], od_k_a[o],
                                      od_r_k[o], od_lnx_w[o], od_lnx_b[o], od_w_out[o], with_ctx)
        h_lat = h_lat + ml[2] * rms_norm(o_lat, g_post_mix)
        f_lat = squared_relu_mlp(modulate(h_lat, g_pre_ff, ml[3], ml[4]), mlp_w_in[layer], mlp_w_out[layer])
        h_lat = h_lat + ml[5] * rms_norm(f_lat, g_post_ff)
        if with_ctx:
            h_ctx = h_ctx + mc[2] * rms_norm(o_ctx, g_post_mix)
            f_ctx = squared_relu_mlp(modulate(h_ctx, g_pre_ff, mc[3], mc[4]), mlp_w_in[layer], mlp_w_out[layer])
            h_ctx = h_ctx + mc[5] * rms_norm(f_ctx, g_post_ff)
    return h_lat
```

```python
import functools
import math

import jax
import jax.numpy as jnp
from jax import lax
from jax.experimental import pallas as pl
from jax.experimental.pallas import tpu as pltpu

F32 = jnp.float32
BF16 = jnp.bfloat16

EPS = 1e-6
LNX_EPS = 64e-5
ROPE_BASE = 10000.0
GRID_W = 64
CHUNK = 128
RWKV_CHUNK = 64
HEAD_DK = 128
HEAD_DV = 256
RWKV_N = 64
INV_LEAF = 4
CONV_K = 3
LANE = 128
MOD_ROWS = 8
VMEM_BYTES = 64 * 1024 * 1024

NT_DIMS = (((1,), (1,)), ((), ()))
TN_DIMS = (((0,), (0,)), ((), ()))


def _dot(a, b):
    return jnp.dot(a, b, preferred_element_type=F32)


def _dot_nt(a, b):
    return lax.dot_general(a, b, NT_DIMS, preferred_element_type=F32)


def _dot_tn(a, b):
    return lax.dot_general(a, b, TN_DIMS, preferred_element_type=F32)


def _split_dot(m, x):
    hi = x.astype(BF16)
    lo = (x - hi.astype(F32)).astype(BF16)
    return _dot(m, hi) + _dot(m, lo)


def _split_dot_r(x, m):
    hi = x.astype(BF16)
    lo = (x - hi.astype(F32)).astype(BF16)
    return _dot(hi, m) + _dot(lo, m)


def _cparams(semantics, vmem_bytes):
    limit = int(min(max(vmem_bytes * 5 // 4 + (4 << 20), 16 << 20), VMEM_BYTES - (6 << 20)))
    return pltpu.CompilerParams(dimension_semantics=semantics, vmem_limit_bytes=limit)


def _pow2_tile(pref, *dims):
    t = pref
    while any(d % t for d in dims):
        t //= 2
    return t


def _round_up(x, m):
    return (x + m - 1) // m * m


def _ada_kernel(c_ref, w_ref, b_ref, o_ref):
    c = c_ref[...]
    s = c * jax.nn.sigmoid(c)
    o_ref[0] = _dot(s.astype(BF16), w_ref[0].astype(BF16)) + b_ref[0]


def _ada_tables(cond, ada_w, ada_b):
    depth, d, n = ada_w.shape
    tn = _pow2_tile(512, n)
    return pl.pallas_call(
        _ada_kernel,
        grid=(depth, n // tn),
        in_specs=[pl.BlockSpec((MOD_ROWS, d), lambda l, j: (0, 0)),
                  pl.BlockSpec((1, d, tn), lambda l, j: (l, 0, j)),
                  pl.BlockSpec((1, 1, tn), lambda l, j: (l, 0, j))],
        out_specs=pl.BlockSpec((1, MOD_ROWS, tn), lambda l, j: (l, 0, j)),
        out_shape=jax.ShapeDtypeStruct((depth, MOD_ROWS, n), F32),
        compiler_params=_cparams(("parallel", "parallel"), 2 * d * tn * 4 + d * tn * 2),
        name="ada_tables",
    )(cond, ada_w, ada_b.reshape(depth, 1, n))


ROW_BLOCK = 32


def _modulate_rows(h_ref, g_ref, sh_ref, sc_ref, u_scr, tm):
    g = g_ref[...]
    sh = sh_ref[0]
    sc = sc_ref[0]

    def body(r, carry):
        sl = pl.ds(pl.multiple_of(r * ROW_BLOCK, ROW_BLOCK), ROW_BLOCK)
        x = h_ref[sl, :]
        ms = jnp.mean(x * x, axis=-1, keepdims=True)
        u = (x * lax.rsqrt(ms + EPS) * g) * (1.0 + sc) + sh
        u_scr[sl, :] = u.astype(u_scr.dtype)
        return carry

    lax.fori_loop(0, tm // ROW_BLOCK, body, 0)


def _norm_residual_rows(acc_ref, h_ref, gate_ref, g_ref, o_ref, tm):
    g = g_ref[...]
    gate = gate_ref[0]

    def body(r, carry):
        sl = pl.ds(pl.multiple_of(r * ROW_BLOCK, ROW_BLOCK), ROW_BLOCK)
        a = acc_ref[sl, :]
        ms = jnp.mean(a * a, axis=-1, keepdims=True)
        o_ref[sl, :] = h_ref[sl, :] + gate * (a * lax.rsqrt(ms + EPS) * g)
        return carry

    lax.fori_loop(0, tm // ROW_BLOCK, body, 0)


def _mod_row_map(tm, n_lat_tiles, s, b):
    def f(i):
        return jnp.where(i < n_lat_tiles, (i * tm) // s, b)
    return f


def _inproj_kernel(h_ref, g_ref, sh_ref, sc_ref, w_ref, wg_ref, bg_ref, p_ref, gt_ref, u_scr, *, tm):
    @pl.when(pl.program_id(1) == 0)
    def _():
        _modulate_rows(h_ref, g_ref, sh_ref, sc_ref, u_scr, tm)
        gt_ref[...] = _dot(u_scr[...], wg_ref[...]) + bg_ref[...]

    p_ref[...] = _dot(u_scr[...], w_ref[...]).astype(p_ref.dtype)


def _inproj(h, g, shift, scale, w, wg, bg, dims):
    r, d = h.shape
    n = w.shape[1]
    tm = _pow2_tile(512, dims["S"], dims["RC"])
    tn = _pow2_tile(1024, n)
    modrow = _mod_row_map(tm, dims["RL"] // tm, dims["S"], dims["B"])
    vmem = 2 * tm * d * 4 + tm * d * 2 + 2 * d * tn * 2 + 2 * tm * tn * 2 + 2 * d * LANE * 2 + 2 * tm * LANE * 4
    return pl.pallas_call(
        functools.partial(_inproj_kernel, tm=tm),
        grid=(r // tm, n // tn),
        in_specs=[pl.BlockSpec((tm, d), lambda i, j: (i, 0)),
                  pl.BlockSpec((1, d), lambda i, j: (0, 0)),
                  pl.BlockSpec((1, 1, d), lambda i, j: (modrow(i), 0, 0)),
                  pl.BlockSpec((1, 1, d), lambda i, j: (modrow(i), 0, 0)),
                  pl.BlockSpec((d, tn), lambda i, j: (0, j)),
                  pl.BlockSpec((d, LANE), lambda i, j: (0, 0)),
                  pl.BlockSpec((1, LANE), lambda i, j: (0, 0))],
        out_specs=[pl.BlockSpec((tm, tn), lambda i, j: (i, j)),
                   pl.BlockSpec((tm, LANE), lambda i, j: (i, 0))],
        out_shape=[jax.ShapeDtypeStruct((r, n), BF16), jax.ShapeDtypeStruct((r, LANE), F32)],
        scratch_shapes=[pltpu.VMEM((tm, d), BF16)],
        compiler_params=_cparams(("parallel", "arbitrary"), vmem),
        name="even_inproj",
    )(h, g, shift, scale, w, wg, bg)


def _conv_kernel(x_ref, w_ref, post_ref, *rest, rows, grid_mode):
    o_ref = rest[-1]
    x = x_ref[...].astype(F32)
    t = lax.broadcasted_iota(jnp.int32, x.shape, 0)
    w = w_ref[...]
    if grid_mode:
        col = t % GRID_W
        has_prev, has_next = col > 0, col < GRID_W - 1
    else:
        has_prev, has_next = t > 0, t < rows - 1
    xl = jnp.where(has_prev, pltpu.roll(x, 1, 0), 0.0)
    xr = jnp.where(has_next, pltpu.roll(x, rows - 1, 0), 0.0)

    def tap_row(dr):
        return xl * w[3 * dr:3 * dr + 1] + x * w[3 * dr + 1:3 * dr + 2] + xr * w[3 * dr + 2:3 * dr + 3]

    y = tap_row(1)
    if grid_mode:
        z = jnp.zeros((GRID_W, x.shape[1]), F32)
        y = y + jnp.concatenate([z, tap_row(0)[:rows - GRID_W]], axis=0)
        y = y + jnp.concatenate([tap_row(2)[GRID_W:], z], axis=0)
    y = y * jax.nn.sigmoid(y)
    o_ref[...] = (y * post_ref[...]).astype(o_ref.dtype)


def _conv_qk(p, conv_w, post, dims):
    b, s, c = dims["B"], dims["S"], dims["C"]
    r = p.shape[0]
    nqk = dims["NQK"]
    ch = 2 * nqk
    tc = LANE
    col0 = 6 * nqk // tc
    w9 = conv_w.reshape(CONV_K * CONV_K, ch)
    out_shape = jax.ShapeDtypeStruct((r, ch), BF16)

    def call(rows, nblk, blk0, grid_mode, prev):
        in_specs = [pl.BlockSpec((rows, tc), lambda i, j: (blk0 + i, col0 + j)),
                    pl.BlockSpec((CONV_K * CONV_K, tc), lambda i, j: (0, j)),
                    pl.BlockSpec((1, tc), lambda i, j: (0, j))]
        args = [p, w9, post]
        aliases = {}
        if prev is not None:
            in_specs.append(pl.BlockSpec(memory_space=pl.ANY))
            args.append(prev)
            aliases = {3: 0}
        return pl.pallas_call(
            functools.partial(_conv_kernel, rows=rows, grid_mode=grid_mode),
            grid=(nblk, ch // tc),
            in_specs=in_specs,
            out_specs=pl.BlockSpec((rows, tc), lambda i, j: (blk0 + i, j)),
            out_shape=out_shape,
            input_output_aliases=aliases,
            compiler_params=_cparams(("parallel", "parallel"), 16 * rows * tc * 4),
            name="conv_lat" if grid_mode else "conv_ctx",
        )(*args)

    lat = call(s, b, 0, True, None)
    return call(c, b, (b * s) // c, False, lat)


def _chunk_maps(dims, chunk, rev):
    b_, s, c = dims["B"], dims["S"], dims["C"]
    ncc, ncl = c // chunk, s // chunk
    nc = ncc + ncl

    def pos(n):
        if not rev:
            return n
        return jnp.where(n < ncc, ncc - 1 - n, ncc + nc - 1 - n)

    def rowblk(b, n):
        q = pos(n)
        return jnp.where(q < ncc, (b_ * s) // chunk + b * ncc + q, b * ncl + q - ncc)

    return nc, pos, rowblk


def _retention_kernel(q_ref, k_ref, v_ref, cos_ref, sin_ref, intra_ref, qd_ref, kd_ref, cd_ref, y_ref, s_scr,
                      *, heads):
    @pl.when(pl.program_id(1) == 0)
    def _():
        s_scr[...] = jnp.zeros_like(s_scr)

    cos = cos_ref[...]
    sin = sin_ref[...]
    lane = lax.broadcasted_iota(jnp.int32, cos.shape, 1)
    first = (lane % (HEAD_DK // 2)) < (HEAD_DK // 4)

    def rope(x):
        swapped = jnp.where(first, pltpu.roll(x, HEAD_DK - HEAD_DK // 4, 1), pltpu.roll(x, HEAD_DK // 4, 1))
        return x * cos + swapped * sin

    for h in range(heads):
        ks = slice(h * HEAD_DK, (h + 1) * HEAD_DK)
        vs = slice(h * HEAD_DV, (h + 1) * HEAD_DV)
        q = rope(q_ref[:, ks].astype(F32))
        k = rope(k_ref[:, ks].astype(F32)) * HEAD_DK ** -0.5
        v = v_ref[:, vs]
        s = s_scr[h]
        sc = _dot_nt(q.astype(BF16), k.astype(BF16)) * intra_ref[h]
        o = _dot(sc.astype(BF16), v) + _dot((q * qd_ref[h]).astype(BF16), s.astype(BF16))
        s_scr[h] = s * cd_ref[h] + _dot_tn((k * kd_ref[h]).astype(BF16), v)
        y_ref[:, vs] = o.astype(y_ref.dtype)


def _retention_tables(heads, rev):
    h = jnp.arange(heads, dtype=F32) / max(heads - 1, 1)
    lg = jnp.log1p(-jnp.exp2(-(5.0 + 7.0 * h)))
    if rev:
        lg = lg[::-1]
    pos = jnp.arange(CHUNK, dtype=F32)
    diff = pos[:, None] - pos[None, :]
    if rev:
        diff = -diff
    intra = jnp.where(diff >= 0, jnp.exp(jnp.maximum(diff, 0.0)[None] * lg[:, None, None]), 0.0)
    q_pow = (CHUNK - pos) if rev else (pos + 1.0)
    k_pow = pos if rev else (CHUNK - 1.0 - pos)
    qd = jnp.exp(q_pow[None, :] * lg[:, None])[..., None]
    kd = jnp.exp(k_pow[None, :] * lg[:, None])[..., None]
    cd = jnp.exp(CHUNK * lg)[:, None, None]
    return intra, qd, kd, cd


def _retention(p, cos, sin, dims, rev):
    heads, nqk, nv = dims["H"], dims["NQK"], dims["NV"]
    r = p.shape[0]
    nc, pos, rowblk = _chunk_maps(dims, CHUNK, rev)
    intra, qd, kd, cd = _retention_tables(heads, rev)
    full = lambda shape: pl.BlockSpec(shape, lambda b, n: (0,) * len(shape))
    vmem = 2 * CHUNK * (2 * nqk + 2 * nv) * 2 + 2 * heads * CHUNK * (CHUNK + 2 * LANE) * 4 + heads * HEAD_DK * HEAD_DV * 4
    return pl.pallas_call(
        functools.partial(_retention_kernel, heads=heads),
        grid=(dims["B"], nc),
        in_specs=[pl.BlockSpec((CHUNK, nqk), lambda b, n: (rowblk(b, n), 0)),
                  pl.BlockSpec((CHUNK, nqk), lambda b, n: (rowblk(b, n), 1)),
                  pl.BlockSpec((CHUNK, nv), lambda b, n: (rowblk(b, n), 1)),
                  pl.BlockSpec((CHUNK, HEAD_DK), lambda b, n: (pos(n), 0)),
                  pl.BlockSpec((CHUNK, HEAD_DK), lambda b, n: (pos(n), 0)),
                  full((heads, CHUNK, CHUNK)), full((heads, CHUNK, 1)), full((heads, CHUNK, 1)),
                  full((heads, 1, 1))],
        out_specs=pl.BlockSpec((CHUNK, nv), lambda b, n: (rowblk(b, n), 0)),
        out_shape=jax.ShapeDtypeStruct((r, nv), BF16),
        scratch_shapes=[pltpu.VMEM((heads, HEAD_DK, HEAD_DV), F32)],
        compiler_params=_cparams(("parallel", "arbitrary"), vmem),
        name="retention_bwd" if rev else "retention_fwd",
    )(p, p, p, cos, sin, intra, qd, kd, cd)


def _rope_tables(dims):
    s, c = dims["S"], dims["C"]
    quarter = HEAD_DK // 4
    inv = jnp.power(ROPE_BASE, -jnp.arange(quarter, dtype=F32) / quarter)
    t = jnp.arange(s)
    rows = (t // GRID_W).astype(F32)
    cols = (t % GRID_W).astype(F32)
    ar = rows[:, None] * inv[None, :]
    ac = cols[:, None] * inv[None, :]
    cos = jnp.concatenate([jnp.cos(ar), jnp.cos(ar), jnp.cos(ac), jnp.cos(ac)], axis=-1)
    sin = jnp.concatenate([-jnp.sin(ar), jnp.sin(ar), -jnp.sin(ac), jnp.sin(ac)], axis=-1)
    cos = jnp.concatenate([jnp.ones((c, HEAD_DK), F32), cos], axis=0)
    sin = jnp.concatenate([jnp.zeros((c, HEAD_DK), F32), sin], axis=0)
    return cos, sin


def _mlstm_kernel(q_ref, k_ref, v_ref, g_ref, y_ref, c_scr, n_scr, m_scr, *, heads, rev):
    @pl.when(pl.program_id(1) == 0)
    def _():
        c_scr[...] = jnp.zeros_like(c_scr)
        n_scr[...] = jnp.zeros_like(n_scr)
        m_scr[...] = jnp.zeros_like(m_scr)

    ti = lax.broadcasted_iota(jnp.int32, (CHUNK, CHUNK), 0)
    si = lax.broadcasted_iota(jnp.int32, (CHUNK, CHUNK), 1)
    causal = (si >= ti) if rev else (si <= ti)
    tri = causal.astype(BF16)
    g = g_ref[...]
    logf = jnp.minimum(g, 0.0) - jnp.log1p(jnp.exp(-jnp.abs(g)))
    cs = _split_dot(tri, logf)
    g_t = g.T
    cs_t = cs.T
    icol0 = 2 * heads if rev else 0
    fcol0 = icol0 + heads
    last = 0 if rev else CHUNK - 1

    for h in range(heads):
        ks = slice(h * HEAD_DK, (h + 1) * HEAD_DK)
        vs = slice(h * HEAD_DV, (h + 1) * HEAD_DV)
        ic, fc = icol0 + h, fcol0 + h
        b_col, b_row = cs[:, fc:fc + 1], cs_t[fc:fc + 1, :]
        i_col, i_row = g[:, ic:ic + 1], g_t[ic:ic + 1, :]
        b_last = cs[last:last + 1, fc:fc + 1]
        m = m_scr[h]
        q = q_ref[:, ks]
        k = k_ref[:, ks]
        v = v_ref[:, vs]
        c_mem = c_scr[h]
        n_mem = n_scr[h]

        d_log = jnp.where(causal, b_col - b_row + i_row, -jnp.inf)
        m_inter = b_col + m
        m_row = jnp.maximum(jnp.max(d_log, axis=-1, keepdims=True), m_inter)
        w_intra = jnp.exp(d_log - m_row)
        w_inter = jnp.exp(m_inter - m_row)
        sc = _dot_nt(q, k) * w_intra
        num = _dot(sc.astype(BF16), v) + w_inter * _dot(q, c_mem.astype(BF16))
        den = jnp.sum(sc, axis=-1, keepdims=True) + w_inter * jnp.sum(q.astype(F32) * n_mem, axis=-1, keepdims=True)
        y_ref[:, vs] = (num / jnp.maximum(jnp.abs(den), jnp.exp(-m_row))).astype(y_ref.dtype)

        gk = b_last - b_col + i_col
        m_new = jnp.maximum(b_last + m, jnp.max(gk, axis=0, keepdims=True))
        w_k = jnp.exp(gk - m_new)
        w_c = jnp.exp(b_last + m - m_new)
        kw = k.astype(F32) * w_k
        c_scr[h] = w_c * c_mem + _dot_tn(kw.astype(BF16), v)
        n_scr[h] = w_c * n_mem + jnp.sum(kw, axis=0, keepdims=True)
        m_scr[h] = m_new


def _mlstm(qk, p, gates, dims, rev):
    heads, nqk, nv = dims["H"], dims["NQK"], dims["NV"]
    r = p.shape[0]
    nc, _, rowblk = _chunk_maps(dims, CHUNK, rev)
    vmem = 2 * CHUNK * (2 * nqk + 2 * nv) * 2 + 2 * CHUNK * LANE * 4 + heads * HEAD_DK * (HEAD_DV + 8) * 4
    return pl.pallas_call(
        functools.partial(_mlstm_kernel, heads=heads, rev=rev),
        grid=(dims["B"], nc),
        in_specs=[pl.BlockSpec((CHUNK, nqk), lambda b, n: (rowblk(b, n), 0)),
                  pl.BlockSpec((CHUNK, nqk), lambda b, n: (rowblk(b, n), 1)),
                  pl.BlockSpec((CHUNK, nv), lambda b, n: (rowblk(b, n), 4)),
                  pl.BlockSpec((CHUNK, LANE), lambda b, n: (rowblk(b, n), 0))],
        out_specs=pl.BlockSpec((CHUNK, nv), lambda b, n: (rowblk(b, n), 0)),
        out_shape=jax.ShapeDtypeStruct((r, nv), BF16),
        scratch_shapes=[pltpu.VMEM((heads, HEAD_DK, HEAD_DV), F32),
                        pltpu.VMEM((heads, 1, HEAD_DK), F32),
                        pltpu.VMEM((heads, 1, 1), F32)],
        compiler_params=_cparams(("parallel", "arbitrary"), vmem),
        name="mlstm_bwd" if rev else "mlstm_fwd",
    )(qk, qk, p, gates)


def _merge_kernel(yrf_ref, yrb_ref, ymf_ref, ymb_ref, ag_ref, bo_ref, gr_ref, gm_ref, z_ref, *, heads):
    nv = heads * HEAD_DV
    for h in range(heads):
        vs = slice(h * HEAD_DV, (h + 1) * HEAD_DV)
        yr = yrf_ref[:, vs].astype(F32) + yrb_ref[:, vs].astype(F32)
        yr = yr * lax.rsqrt(jnp.mean(yr * yr, axis=-1, keepdims=True) + EPS) * gr_ref[:, vs]
        ag = ag_ref[:, vs].astype(F32)
        z_ref[:, vs] = (yr * (ag * jax.nn.sigmoid(ag))).astype(z_ref.dtype)
        ym = ymf_ref[:, vs].astype(F32) + ymb_ref[:, vs].astype(F32)
        ym = ym * lax.rsqrt(jnp.mean(ym * ym, axis=-1, keepdims=True) + EPS) * gm_ref[:, vs]
        bo = bo_ref[:, vs].astype(F32)
        z_ref[:, nv + h * HEAD_DV:nv + (h + 1) * HEAD_DV] = (ym * jax.nn.sigmoid(bo)).astype(z_ref.dtype)


def _merge(yrf, yrb, ymf, ymb, p, gn_ret, gn_ml, dims, rows):
    heads, nv = dims["H"], dims["NV"]
    tm = _pow2_tile(128, dims["S"], dims["RC"])
    ys = pl.BlockSpec((tm, nv), lambda i: (i, 0))
    return pl.pallas_call(
        functools.partial(_merge_kernel, heads=heads),
        grid=(rows // tm,),
        in_specs=[ys, ys, ys, ys,
                  pl.BlockSpec((tm, nv), lambda i: (i, 2)),
                  pl.BlockSpec((tm, nv), lambda i: (i, 5)),
                  pl.BlockSpec((1, nv), lambda i: (0, 0)),
                  pl.BlockSpec((1, nv), lambda i: (0, 0))],
        out_specs=pl.BlockSpec((tm, 2 * nv), lambda i: (i, 0)),
        out_shape=jax.ShapeDtypeStruct((rows, 2 * nv), BF16),
        compiler_params=_cparams(("parallel",), 2 * tm * nv * 2 * 8 + 8 * tm * nv * 4),
        name="even_merge",
    )(yrf, yrb, ymf, ymb, p, p, gn_ret, gn_ml)


def _outproj_kernel(z_ref, w_ref, h_ref, gate_ref, g_ref, o_ref, acc_scr, *, tm, tn):
    j = pl.program_id(1)
    acc_scr[:, pl.ds(pl.multiple_of(j * tn, tn), tn)] = _dot(z_ref[...], w_ref[...])

    @pl.when(j == pl.num_programs(1) - 1)
    def _():
        _norm_residual_rows(acc_scr, h_ref, gate_ref, g_ref, o_ref, tm)


def _outproj(z, w, h, gate, g, dims, rows):
    k, n = w.shape
    tm = _pow2_tile(256, dims["S"], dims["RC"])
    tn = _pow2_tile(512, n)
    modrow = _mod_row_map(tm, dims["RL"] // tm, dims["S"], dims["B"])
    vmem = 2 * tm * k * 2 + 2 * k * tn * 2 + tm * n * 4 * 5
    return pl.pallas_call(
        functools.partial(_outproj_kernel, tm=tm, tn=tn),
        grid=(rows // tm, n // tn),
        in_specs=[pl.BlockSpec((tm, k), lambda i, j: (i, 0)),
                  pl.BlockSpec((k, tn), lambda i, j: (0, j)),
                  pl.BlockSpec((tm, n), lambda i, j: (i, 0)),
                  pl.BlockSpec((1, 1, n), lambda i, j: (modrow(i), 0, 0)),
                  pl.BlockSpec((1, n), lambda i, j: (0, 0))],
        out_specs=pl.BlockSpec((tm, n), lambda i, j: (i, 0)),
        out_shape=jax.ShapeDtypeStruct((rows, n), F32),
        scratch_shapes=[pltpu.VMEM((tm, n), F32)],
        compiler_params=_cparams(("parallel", "arbitrary"), vmem),
        name="outproj_norm_residual",
    )(z, w, h, gate, g)


def _mlp_kernel(h_ref, gpre_ref, sh_ref, sc_ref, w1_ref, w2_ref, gate_ref, gpost_ref, o_ref, u_scr, *, tm):
    f = pl.program_id(1)

    @pl.when(f == 0)
    def _():
        _modulate_rows(h_ref, gpre_ref, sh_ref, sc_ref, u_scr, tm)

    a = jnp.maximum(_dot(u_scr[...], w1_ref[...]), 0.0)
    part = _dot((a * a).astype(BF16), w2_ref[...])

    @pl.when(f == 0)
    def _():
        o_ref[...] = part

    @pl.when(f > 0)
    def _():
        o_ref[...] += part

    @pl.when(f == pl.num_programs(1) - 1)
    def _():
        _norm_residual_rows(o_ref, h_ref, gate_ref, gpost_ref, o_ref, tm)


def _mlp(h, g_pre, shift, scale, w1, w2, gate, g_post, dims, rows):
    d, ff = w1.shape
    tm = _pow2_tile(512, dims["S"], dims["RC"])
    tf = _pow2_tile(256, ff)
    modrow = _mod_row_map(tm, dims["RL"] // tm, dims["S"], dims["B"])
    mod_spec = pl.BlockSpec((1, 1, d), lambda i, f: (modrow(i), 0, 0))
    row_spec = pl.BlockSpec((1, d), lambda i, f: (0, 0))
    vmem = 4 * tm * d * 4 + tm * d * 2 + 4 * d * tf * 2 + 3 * tm * tf * 4
    return pl.pallas_call(
        functools.partial(_mlp_kernel, tm=tm),
        grid=(rows // tm, ff // tf),
        in_specs=[pl.BlockSpec((tm, d), lambda i, f: (i, 0)),
                  row_spec, mod_spec, mod_spec,
                  pl.BlockSpec((d, tf), lambda i, f: (0, f)),
                  pl.BlockSpec((tf, d), lambda i, f: (f, 0)),
                  mod_spec, row_spec],
        out_specs=pl.BlockSpec((tm, d), lambda i, f: (i, 0)),
        out_shape=jax.ShapeDtypeStruct((rows, d), F32),
        scratch_shapes=[pltpu.VMEM((tm, d), BF16)],
        compiler_params=_cparams(("parallel", "arbitrary"), vmem),
        name="mlp_block",
    )(h, g_pre, shift, scale, w1, w2, gate, g_post)


HALO = GRID_W


def _rwkv_prep_kernel(h_ref, ha_ref, hb_ref, g_ref, sh_ref, sc_ref, mu_ref, o_ref, rs_scr, ra_scr, rb_scr,
                      *, tm, tc, n_lat_tiles, s, c):
    i = pl.program_id(0)
    jc = pl.program_id(1)
    is_lat = i < n_lat_tiles
    cs = pl.ds(pl.multiple_of(jc * tc, tc), tc)

    @pl.when(jc == 0)
    def _():
        def rstd(x):
            return lax.rsqrt(jnp.mean(x * x, axis=-1, keepdims=True) + EPS)

        def body(r, carry):
            sl = pl.ds(pl.multiple_of(r * ROW_BLOCK, ROW_BLOCK), ROW_BLOCK)
            rs_scr[sl, :] = rstd(h_ref[sl, :])
            return carry

        lax.fori_loop(0, tm // ROW_BLOCK, body, 0)
        ra_scr[...] = rstd(ha_ref[...])
        rb_scr[...] = rstd(hb_ref[...])

    g = g_ref[...]
    sh = sh_ref[0]
    sc = sc_ref[0]

    def mod(x, rs):
        return (x * rs * g) * (1.0 + sc) + sh

    u = mod(h_ref[:, cs], rs_scr[...])
    t = lax.broadcasted_iota(jnp.int32, u.shape, 0)
    period = jnp.where(is_lat, GRID_W, c)
    tpos = t & (period - 1)

    def emit(shifted):
        xx = shifted - u
        for m in range(6):
            o_ref[m] = (u + xx * mu_ref[m]).astype(o_ref.dtype)

    use_prev = (jc == 0) | (jnp.logical_not(is_lat) & (jc == 1))
    use_next = (is_lat & (jc == 1)) | (jnp.logical_not(is_lat) & (jc >= 2))

    @pl.when(use_prev)
    def _():
        emit(jnp.where(tpos > 0, pltpu.roll(u, 1, 0), 0.0))

    @pl.when(use_next)
    def _():
        emit(jnp.where(tpos < period - 1, pltpu.roll(u, tm - 1, 0), 0.0))

    @pl.when(is_lat & (jc == 2))
    def _():
        first = (i * tm) % s == 0
        above = jnp.where(first, 0.0, mod(ha_ref[:, cs], ra_scr[...]))
        emit(jnp.concatenate([above, u[:tm - HALO]], axis=0))

    @pl.when(is_lat & (jc == 3))
    def _():
        lastt = ((i + 1) * tm) % s == 0
        below = jnp.where(lastt, 0.0, mod(hb_ref[:, cs], rb_scr[...]))
        emit(jnp.concatenate([u[HALO:], below], axis=0))


def _rwkv_prep(h, g, shift, scale, mu6, dims):
    r, d = h.shape
    s, c = dims["S"], dims["C"]
    tm = c
    assert tm % HALO == 0 and s % tm == 0 and tm & (tm - 1) == 0
    tc = d // 4
    hb = tm // HALO
    nblk = r // HALO
    n_lat_tiles = dims["RL"] // tm
    modrow = _mod_row_map(tm, n_lat_tiles, s, dims["B"])
    mod_spec = pl.BlockSpec((1, 1, tc), lambda i, j: (modrow(i), 0, j))
    vmem = 2 * tm * d * 4 + 4 * HALO * d * 4 + 2 * 6 * tm * tc * 2 + 12 * tm * tc * 4
    return pl.pallas_call(
        functools.partial(_rwkv_prep_kernel, tm=tm, tc=tc, n_lat_tiles=n_lat_tiles, s=s, c=c),
        grid=(r // tm, 4),
        in_specs=[pl.BlockSpec((tm, d), lambda i, j: (i, 0)),
                  pl.BlockSpec((HALO, d), lambda i, j: (jnp.maximum(i * hb - 1, 0), 0)),
                  pl.BlockSpec((HALO, d), lambda i, j: (jnp.minimum((i + 1) * hb, nblk - 1), 0)),
                  pl.BlockSpec((1, tc), lambda i, j: (0, j)),
                  mod_spec, mod_spec,
                  pl.BlockSpec((6, 1, tc), lambda i, j: (0, 0, j))],
        out_specs=pl.BlockSpec((6, tm, tc), lambda i, j: (0, i, j)),
        out_shape=jax.ShapeDtypeStruct((6, r, d), BF16),
        scratch_shapes=[pltpu.VMEM((tm, 1), F32), pltpu.VMEM((HALO, 1), F32), pltpu.VMEM((HALO, 1), F32)],
        compiler_params=_cparams(("parallel", "arbitrary"), vmem),
        name="rwkv_prep",
    )(h, h, h, g, shift, scale, mu6)


def _bmm_kernel(x_ref, w_ref, o_ref):
    o_ref[0] = _dot(x_ref[0], w_ref[0]).astype(o_ref.dtype)


def _bmm(x, w, lhs0, out_dtype, dims):
    ng, k, n = w.shape
    r = x.shape[1]
    tm = _pow2_tile(512, dims["S"], dims["RC"])
    tn = _pow2_tile(1024, n)
    osz = jnp.dtype(out_dtype).itemsize
    return pl.pallas_call(
        _bmm_kernel,
        grid=(ng, r // tm, n // tn),
        in_specs=[pl.BlockSpec((1, tm, k), lambda g, i, j: (lhs0 + g, i, 0)),
                  pl.BlockSpec((1, k, tn), lambda g, i, j: (g, 0, j))],
        out_specs=pl.BlockSpec((1, tm, tn), lambda g, i, j: (g, i, j)),
        out_shape=jax.ShapeDtypeStruct((ng, r, n), out_dtype),
        compiler_params=_cparams(("parallel", "parallel", "arbitrary"),
                                 2 * tm * k * 2 + 2 * k * tn * 2 + 2 * tm * tn * osz + tm * tn * 4),
        name="rwkv_bmm",
    )(x, w)


def _pair_ones():
    a = lax.broadcasted_iota(jnp.int32, (LANE, LANE), 0) // RWKV_N
    b = lax.broadcasted_iota(jnp.int32, (LANE, LANE), 1) // RWKV_N
    return (a == b).astype(BF16)


def _head_sums(x, ones):
    parts = [_split_dot_r(x[:, l:l + LANE], ones) for l in range(0, x.shape[1], LANE)]
    return parts[0] if len(parts) == 1 else jnp.concatenate(parts, axis=1)


def _rwkv_mid_kernel(k_ref, hid_ref, w2_ref, a2_ref, g2_ref, w0_ref, a0_ref, kk_w_ref,
                     lw_ref, a_ref, kk_ref, gate_ref, *, lp):
    ones = _pair_ones()
    hw = hid_ref[0]
    ha = hid_ref[1]
    hg = hid_ref[2]
    for d in range(2):
        tw = jnp.tanh(hw[:, d * lp:(d + 1) * lp]).astype(BF16)
        z = w0_ref[d] + _dot(tw, w2_ref[d])
        w_log = jnp.minimum(z, 0.0) - jnp.log1p(jnp.exp(-jnp.abs(z))) - 0.5
        lw_ref[d] = -jnp.exp(w_log)
        za = a0_ref[d] + _dot(ha[:, d * lp:(d + 1) * lp].astype(BF16), a2_ref[d])
        a_ref[d] = jax.nn.sigmoid(za).astype(a_ref.dtype)
    gate_ref[...] = _dot(jax.nn.sigmoid(hg).astype(BF16), g2_ref[...]).astype(gate_ref.dtype)
    kk = k_ref[0].astype(F32) * kk_w_ref[...]
    ss = _head_sums(kk * kk, ones)
    kk_ref[...] = (kk * lax.rsqrt(jnp.maximum(ss, 1e-24))).astype(kk_ref.dtype)


def _rwkv_mid(rkv, hid, w2, a2, g2, w0, a0, k_k, dims):
    _, r, d = rkv.shape
    lp = w2.shape[1]
    lw_tot = hid.shape[2]
    tm = _pow2_tile(256, dims["S"], dims["RC"])
    tc = _pow2_tile(1024, d)
    vmem = 2 * (tm * tc * 2 + 3 * tm * lw_tot * 4 + (4 * lp + g2.shape[0]) * tc * 2
                + tm * tc * (8 + 4 + 2 + 2)) + 10 * tm * tc * 4
    return pl.pallas_call(
        functools.partial(_rwkv_mid_kernel, lp=lp),
        grid=(r // tm, d // tc),
        in_specs=[pl.BlockSpec((1, tm, tc), lambda i, j: (1, i, j)),
                  pl.BlockSpec((3, tm, lw_tot), lambda i, j: (0, i, 0)),
                  pl.BlockSpec((2, lp, tc), lambda i, j: (0, 0, j)),
                  pl.BlockSpec((2, lp, tc), lambda i, j: (0, 0, j)),
                  pl.BlockSpec((g2.shape[0], tc), lambda i, j: (0, j)),
                  pl.BlockSpec((2, 1, tc), lambda i, j: (0, 0, j)),
                  pl.BlockSpec((2, 1, tc), lambda i, j: (0, 0, j)),
                  pl.BlockSpec((1, tc), lambda i, j: (0, j))],
        out_specs=[pl.BlockSpec((2, tm, tc), lambda i, j: (0, i, j)),
                   pl.BlockSpec((2, tm, tc), lambda i, j: (0, i, j)),
                   pl.BlockSpec((tm, tc), lambda i, j: (i, j)),
                   pl.BlockSpec((tm, tc), lambda i, j: (i, j))],
        out_shape=[jax.ShapeDtypeStruct((2, r, d), F32), jax.ShapeDtypeStruct((2, r, d), BF16),
                   jax.ShapeDtypeStruct((r, d), BF16), jax.ShapeDtypeStruct((r, d), BF16)],
        compiler_params=_cparams(("parallel", "arbitrary"), vmem),
        name="rwkv_mid",
    )(rkv, hid, w2, a2, g2, w0, a0, k_k)


def _rwkv_scan_kernel(r_ref, k_ref, v_ref, kk_ref, a_ref, lw_ref, ka_ref, y_ref, s_scr, *, rev, npairs):
    @pl.when(pl.program_id(2) == 0)
    def _():
        s_scr[...] = jnp.zeros_like(s_scr)

    ln = RWKV_CHUNK
    row = lax.broadcasted_iota(jnp.int32, (ln, LANE), 0)
    lane = lax.broadcasted_iota(jnp.int32, (ln, LANE), 1)
    col = lane % RWKV_N
    head0 = lane < RWKV_N
    strict = (col > row) if rev else (col < row)
    incl = (col >= row) if rev else (col <= row)
    eye = (col == row).astype(F32)

    def same_block(size):
        return (row // size) == (col // size)

    leaf = same_block(INV_LEAF)
    offs = [same_block(2 * sz) & jnp.logical_not(same_block(sz))
            for sz in (INV_LEAF << i for i in range(int(math.log2(ln // INV_LEAF))))]
    ti = lax.broadcasted_iota(jnp.int32, (ln, ln), 0)
    si = lax.broadcasted_iota(jnp.int32, (ln, ln), 1)
    tri = ((si >= ti) if rev else (si <= ti)).astype(BF16)
    ra = lax.broadcasted_iota(jnp.int32, (LANE, LANE), 0) // RWKV_N
    rb = lax.broadcasted_iota(jnp.int32, (LANE, LANE), 1) // RWKV_N
    same_head = ra == rb
    last = 0 if rev else ln - 1

    def bd(x):
        zero = jnp.zeros_like(x)
        return jnp.concatenate([jnp.where(head0, x, zero), jnp.where(head0, zero, x)], axis=0)

    def pm(a, b):
        return _dot(a.astype(BF16), bd(b.astype(BF16)))

    for p in range(npairs):
        sl = slice(p * LANE, (p + 1) * LANE)
        lw = lw_ref[0, :, sl]
        c = _split_dot(tri, lw)
        wi = jnp.exp(c)
        winv = jnp.exp(-c)
        wex = jnp.exp(c - lw)
        wtot = jnp.exp(c[last:last + 1, :])
        r = r_ref[0, :, sl].astype(F32)
        k = k_ref[0, :, sl].astype(F32)
        v = v_ref[0, :, sl]
        kk = kk_ref[:, sl].astype(F32)
        a = a_ref[0, :, sl].astype(F32)
        kd = k * (1.0 + (a - 1.0) * ka_ref[:, sl])
        at = (-(kk * wex)).astype(BF16)
        bt = (kk * a * winv).astype(BF16)
        kt = (kd * winv).astype(BF16)
        rt = (r * wi).astype(BF16)

        res = _dot_nt(jnp.concatenate([at, rt], axis=0), jnp.concatenate([bd(bt), bd(kt)], axis=0))
        mab = jnp.where(strict, res[:ln, :LANE], 0.0)
        mak = jnp.where(strict, res[:ln, LANE:], 0.0)
        rbm = jnp.where(incl, res[ln:, :LANE], 0.0)
        rkm = jnp.where(incl, res[ln:, LANE:], 0.0)

        md = jnp.where(leaf, mab, 0.0)
        t1 = eye + md
        tinv = t1 + pm(pm(md, md), t1)
        for off in offs:
            tinv = tinv + pm(pm(tinv, jnp.where(off, mab, 0.0)), tinv)

        s0 = s_scr[p]
        s0b = s0.astype(BF16)
        x = _dot_nt(at, s0b) + _dot(mak.astype(BF16), bd(v))
        u = pm(tinv, x)
        ub = u.astype(BF16)
        y = _dot_nt(rt, s0b) + _dot(jnp.concatenate([rbm, rkm], axis=1).astype(BF16),
                                    jnp.concatenate([bd(ub), bd(v)], axis=0))
        upd = _dot_tn(jnp.concatenate([ub, v], axis=0), jnp.concatenate([bt, kt], axis=0))
        s_scr[p] = (s0 + jnp.where(same_head, upd, 0.0)) * wtot
        y_ref[:, sl] = y.astype(y_ref.dtype)


def _rwkv_scan(rkv, kk, a, lw, k_a, dims, rev):
    _, r, d = rkv.shape
    npairs = min(8, d // LANE)
    pw = npairs * LANE
    ln = RWKV_CHUNK
    nc, _, rowblk = _chunk_maps(dims, ln, rev)
    dsel = 1 if rev else 0

    def spec3(lead):
        return pl.BlockSpec((1, ln, pw), lambda b, g, n: (lead, rowblk(b, n), g))

    return pl.pallas_call(
        functools.partial(_rwkv_scan_kernel, rev=rev, npairs=npairs),
        grid=(dims["B"], d // pw, nc),
        in_specs=[spec3(0), spec3(1), spec3(2),
                  pl.BlockSpec((ln, pw), lambda b, g, n: (rowblk(b, n), g)),
                  spec3(dsel), spec3(dsel),
                  pl.BlockSpec((1, pw), lambda b, g, n: (0, g))],
        out_specs=pl.BlockSpec((ln, pw), lambda b, g, n: (rowblk(b, n), g)),
        out_shape=jax.ShapeDtypeStruct((r, d), BF16),
        scratch_shapes=[pltpu.VMEM((npairs, LANE, LANE), F32)],
        compiler_params=_cparams(("parallel", "parallel", "arbitrary"),
                                 2 * ln * pw * (5 * 2 + 4 + 2) + npairs * LANE * LANE * 4 + (8 << 20)),
        name="rwkv_scan_bwd" if rev else "rwkv_scan_fwd",
    )(rkv, rkv, rkv, kk, a, lw, k_a)


def _rwkv_finish_kernel(yf_ref, yb_ref, r_ref, k_ref, v_ref, a_ref, gate_ref, ka_ref, rk_ref, lw_ref, lb_ref, z_ref):
    ones = _pair_ones()
    inv_n = 1.0 / RWKV_N
    y = yf_ref[...].astype(F32) + yb_ref[...].astype(F32)
    mean = _head_sums(y, ones) * inv_n
    yc = y - mean
    var = _head_sums(yc * yc, ones) * inv_n
    yn = yc * lax.rsqrt(var + LNX_EPS) * lw_ref[...] + lb_ref[...]
    k = k_ref[0].astype(F32)
    ka = ka_ref[...]
    kd0 = k * (1.0 + (a_ref[0].astype(F32) - 1.0) * ka)
    kd1 = k * (1.0 + (a_ref[1].astype(F32) - 1.0) * ka)
    kb = 0.5 * (kd0 + kd1)
    bonus = _head_sums(r_ref[0].astype(F32) * kb * rk_ref[...], ones) * v_ref[0].astype(F32)
    z_ref[...] = ((yn + bonus) * gate_ref[...].astype(F32)).astype(z_ref.dtype)


def _rwkv_finish(yf, yb, rkv, a, gate, k_a, r_k, lnx_w, lnx_b, dims, rows):
    d = yf.shape[1]
    tm = _pow2_tile(256, dims["S"], dims["RC"])
    tc = _pow2_tile(512, d)
    blk = pl.BlockSpec((tm, tc), lambda i, j: (i, j))
    vec = pl.BlockSpec((1, tc), lambda i, j: (0, j))

    def lead(g):
        return pl.BlockSpec((1, tm, tc), lambda i, j: (g, i, j))

    return pl.pallas_call(
        _rwkv_finish_kernel,
        grid=(rows // tm, d // tc),
        in_specs=[blk, blk, lead(0), lead(1), lead(2),
                  pl.BlockSpec((2, tm, tc), lambda i, j: (0, i, j)), blk, vec, vec, vec, vec],
        out_specs=blk,
        out_shape=jax.ShapeDtypeStruct((rows, d), BF16),
        compiler_params=_cparams(("parallel", "parallel"), 2 * tm * tc * 2 * 9 + 16 * tm * tc * 4),
        name="rwkv_finish",
    )(yf, yb, rkv, rkv, rkv, a, gate, k_a, r_k, lnx_w, lnx_b)


def _even_layer(h, mod, norm_g, w_in, b_gate, conv_qk, gn_ret, gn_ml, w_out, dims, rows_out):
    d = h.shape[1]
    heads, nqk, nv = dims["H"], dims["NQK"], dims["NV"]
    n_main = 4 * nqk + 4 * nv
    n_gate = 4 * heads
    w_main = w_in[:, :n_main].astype(BF16)
    w_gate = jnp.pad(w_in[:, n_main:], ((0, 0), (0, LANE - n_gate))).astype(BF16)
    bg = jnp.pad(b_gate, (0, LANE - n_gate)).reshape(1, LANE)
    shift, scale, gate = mod[0], mod[1], mod[2]
    p, gates = _inproj(h, norm_g[0].reshape(1, d), shift, scale, w_main, w_gate, bg, dims)
    post = jnp.concatenate([jnp.ones((nqk,), F32), jnp.full((nqk,), HEAD_DK ** -0.5, F32)]).reshape(1, 2 * nqk)
    qk = _conv_qk(p, conv_qk, post, dims)
    cos, sin = _rope_tables(dims)
    yrf = _retention(p, cos, sin, dims, False)
    yrb = _retention(p, cos, sin, dims, True)
    ymf = _mlstm(qk, p, gates, dims, False)
    ymb = _mlstm(qk, p, gates, dims, True)
    z = _merge(yrf, yrb, ymf, ymb, p, gn_ret.reshape(1, nv), gn_ml.reshape(1, nv), dims, rows_out)
    return _outproj(z, w_out.astype(BF16), h, gate, norm_g[1].reshape(1, d), dims, rows_out)


def _rwkv_layer(h, mod, norm_g, mu, w_rkv, w0, w1, w2, a0, a1, a2, g1, g2, k_k, k_a, r_k, lnx_w, lnx_b, w_out,
                dims, rows_out):
    d = h.shape[1]
    shift, scale, gate = mod[0], mod[1], mod[2]
    order = jnp.array([0, 2, 3, 1, 4, 5])
    mu6 = mu[order].reshape(6, 1, d)
    x6 = _rwkv_prep(h, norm_g[0].reshape(1, d), shift, scale, mu6, dims)
    rkv = _bmm(x6, w_rkv.astype(BF16), 0, BF16, dims)
    lora = w1.shape[2]
    glora = g1.shape[1]
    lp = _round_up(lora, LANE)
    gp = _round_up(glora, LANE)
    lw_tot = max(2 * lp, gp)

    def two(w):
        w = jnp.pad(w, ((0, 0), (0, 0), (0, lp - lora)))
        w = jnp.concatenate([w[0], w[1]], axis=1)
        return jnp.pad(w, ((0, 0), (0, lw_tot - 2 * lp)))

    w_l1 = jnp.stack([two(w1), two(a1), jnp.pad(g1, ((0, 0), (0, lw_tot - glora)))]).astype(BF16)
    hid = _bmm(x6, w_l1, 3, F32, dims)
    pad_rows = lambda w, n: jnp.pad(w, ((0, 0),) * (w.ndim - 2) + ((0, n - w.shape[-2]), (0, 0))).astype(BF16)
    lw, a, kk, gmul = _rwkv_mid(rkv, hid, pad_rows(w2, lp), pad_rows(a2, lp), pad_rows(g2, lw_tot),
                                w0.reshape(2, 1, d), a0.reshape(2, 1, d), k_k.reshape(1, d), dims)
    ka = k_a.reshape(1, d)
    yf = _rwkv_scan(rkv, kk, a, lw, ka, dims, False)
    yb = _rwkv_scan(rkv, kk, a, lw, ka, dims, True)
    z = _rwkv_finish(yf, yb, rkv, a, gmul, ka, r_k.reshape(1, d), lnx_w.reshape(1, d), lnx_b.reshape(1, d),
                     dims, rows_out)
    return _outproj(z, w_out.astype(BF16), h, gate, norm_g[1].reshape(1, d), dims, rows_out)


def kernel(x, c, ctx, c_ctx, ada_w, ada_b, norm_g, mlp_w_in, mlp_w_out, ev_w_in, ev_b_gate, ev_conv_qk,
           ev_gn_ret, ev_gn_mlstm, ev_w_out, od_mu, od_w_rkv, od_w0, od_w1, od_w2, od_a0, od_a1, od_a2,
           od_g1, od_g2, od_k_k, od_k_a, od_r_k, od_lnx_w, od_lnx_b, od_w_out):
    b, s, d = x.shape
    cl = ctx.shape[1]
    depth = ada_w.shape[0]
    heads = d // (2 * HEAD_DV)
    assert b + 1 <= MOD_ROWS and s % CHUNK == 0 and cl % CHUNK == 0 and s % GRID_W == 0
    dims = dict(B=b, S=s, C=cl, RL=b * s, RC=b * cl, H=heads, NQK=heads * HEAD_DK, NV=heads * HEAD_DV)
    rl, r = dims["RL"], dims["RL"] + dims["RC"]

    cond = jnp.concatenate([c, c_ctx[None, :], jnp.zeros((MOD_ROWS - b - 1, d), F32)], axis=0)
    tables = _ada_tables(cond, ada_w, ada_b)
    h = jnp.concatenate([x.reshape(rl, d), ctx.reshape(b * cl, d)], axis=0)

    for layer in range(depth):
        with_ctx = layer < depth - 1
        rows_out = r if with_ctx else rl
        mod = tables[layer].reshape(MOD_ROWS, 6, 1, d).transpose(1, 0, 2, 3)
        g = norm_g[layer]
        if layer % 2 == 0:
            e = layer // 2
            h_mix = _even_layer(h, mod[0:3], g[0:2], ev_w_in[e], ev_b_gate[e], ev_conv_qk[e], ev_gn_ret[e],
                                ev_gn_mlstm[e], ev_w_out[e], dims, rows_out)
        else:
            o = layer // 2
            h_mix = _rwkv_layer(h, mod[0:3], g[0:2], od_mu[o], od_w_rkv[o], od_w0[o], od_w1[o], od_w2[o],
                                od_a0[o], od_a1[o], od_a2[o], od_g1[o], od_g2[o], od_k_k[o], od_k_a[o], od_r_k[o],
                                od_lnx_w[o], od_lnx_b[o], od_w_out[o], dims, rows_out)
        h = _mlp(h_mix, g[2].reshape(1, d), mod[3], mod[4], mlp_w_in[layer].astype(BF16),
                 mlp_w_out[layer].astype(BF16), mod[5], g[3].reshape(1, d), dims, rows_out)
    return h[:rl].reshape(b, s, d)
```

```python
import functools
import math

import jax
import jax.numpy as jnp
from jax import lax
from jax.experimental import pallas as pl
from jax.experimental.pallas import tpu as pltpu

F32 = jnp.float32
BF16 = jnp.bfloat16

EPS = 1e-6
LNX_EPS = 64e-5
ROPE_BASE = 10000.0
GRID_W = 64
CHUNK = 128
RWKV_CHUNK = 64
HEAD_DK = 128
HEAD_DV = 256
RWKV_N = 64
INV_LEAF = 4
CONV_K = 3
LANE = 128
MOD_ROWS = 8
VMEM_BYTES = 64 * 1024 * 1024

NT_DIMS = (((1,), (1,)), ((), ()))
TN_DIMS = (((0,), (0,)), ((), ()))


def _dot(a, b):
    return jnp.dot(a, b, preferred_element_type=F32)


def _dot_nt(a, b):
    return lax.dot_general(a, b, NT_DIMS, preferred_element_type=F32)


def _dot_tn(a, b):
    return lax.dot_general(a, b, TN_DIMS, preferred_element_type=F32)


def _split_dot(m, x):
    hi = x.astype(BF16)
    lo = (x - hi.astype(F32)).astype(BF16)
    return _dot(m, hi) + _dot(m, lo)


def _split_dot_r(x, m):
    hi = x.astype(BF16)
    lo = (x - hi.astype(F32)).astype(BF16)
    return _dot(hi, m) + _dot(lo, m)


def _cparams(semantics, vmem_bytes):
    limit = int(min(max(vmem_bytes * 5 // 4 + (4 << 20), 16 << 20), VMEM_BYTES - (6 << 20)))
    return pltpu.CompilerParams(dimension_semantics=semantics, vmem_limit_bytes=limit)


def _pow2_tile(pref, *dims):
    t = pref
    while any(d % t for d in dims):
        t //= 2
    return t


def _round_up(x, m):
    return (x + m - 1) // m * m


def _ada_kernel(c_ref, w_ref, b_ref, o_ref):
    c = c_ref[...]
    s = c * jax.nn.sigmoid(c)
    o_ref[0] = _dot(s.astype(BF16), w_ref[0].astype(BF16)) + b_ref[0]


def _ada_tables(cond, ada_w, ada_b):
    depth, d, n = ada_w.shape
    tn = _pow2_tile(512, n)
    return pl.pallas_call(
        _ada_kernel,
        grid=(depth, n // tn),
        in_specs=[pl.BlockSpec((MOD_ROWS, d), lambda l, j: (0, 0)),
                  pl.BlockSpec((1, d, tn), lambda l, j: (l, 0, j)),
                  pl.BlockSpec((1, 1, tn), lambda l, j: (l, 0, j))],
        out_specs=pl.BlockSpec((1, MOD_ROWS, tn), lambda l, j: (l, 0, j)),
        out_shape=jax.ShapeDtypeStruct((depth, MOD_ROWS, n), F32),
        compiler_params=_cparams(("parallel", "parallel"), 2 * d * tn * 4 + d * tn * 2),
        name="ada_tables",
    )(cond, ada_w, ada_b.reshape(depth, 1, n))


ROW_BLOCK = 32


def _modulate_rows(h_ref, g_ref, sh_ref, sc_ref, u_scr, tm):
    g = g_ref[...]
    sh = sh_ref[0]
    sc = sc_ref[0]

    def body(r, carry):
        sl = pl.ds(pl.multiple_of(r * ROW_BLOCK, ROW_BLOCK), ROW_BLOCK)
        x = h_ref[sl, :]
        ms = jnp.mean(x * x, axis=-1, keepdims=True)
        u = (x * lax.rsqrt(ms + EPS) * g) * (1.0 + sc) + sh
        u_scr[sl, :] = u.astype(u_scr.dtype)
        return carry

    lax.fori_loop(0, tm // ROW_BLOCK, body, 0)


def _norm_residual_rows(acc_ref, h_ref, gate_ref, g_ref, o_ref, tm):
    g = g_ref[...]
    gate = gate_ref[0]

    def body(r, carry):
        sl = pl.ds(pl.multiple_of(r * ROW_BLOCK, ROW_BLOCK), ROW_BLOCK)
        a = acc_ref[sl, :]
        ms = jnp.mean(a * a, axis=-1, keepdims=True)
        o_ref[sl, :] = h_ref[sl, :] + gate * (a * lax.rsqrt(ms + EPS) * g)
        return carry

    lax.fori_loop(0, tm // ROW_BLOCK, body, 0)


def _mod_row_map(tm, n_lat_tiles, s, b):
    def f(i):
        return jnp.where(i < n_lat_tiles, (i * tm) // s, b)
    return f


def _inproj_kernel(h_ref, g_ref, sh_ref, sc_ref, w_ref, wg_ref, bg_ref, p_ref, gt_ref, u_scr, *, tm):
    @pl.when(pl.program_id(1) == 0)
    def _():
        _modulate_rows(h_ref, g_ref, sh_ref, sc_ref, u_scr, tm)
        gt_ref[...] = _dot(u_scr[...], wg_ref[...]) + bg_ref[...]

    p_ref[...] = _dot(u_scr[...], w_ref[...]).astype(p_ref.dtype)


def _inproj(h, g, shift, scale, w, wg, bg, dims):
    r, d = h.shape
    n = w.shape[1]
    tm = _pow2_tile(512, dims["S"], dims["RC"])
    tn = _pow2_tile(1024, n)
    modrow = _mod_row_map(tm, dims["RL"] // tm, dims["S"], dims["B"])
    vmem = 2 * tm * d * 4 + tm * d * 2 + 2 * d * tn * 2 + 2 * tm * tn * 2 + 2 * d * LANE * 2 + 2 * tm * LANE * 4
    return pl.pallas_call(
        functools.partial(_inproj_kernel, tm=tm),
        grid=(r // tm, n // tn),
        in_specs=[pl.BlockSpec((tm, d), lambda i, j: (i, 0)),
                  pl.BlockSpec((1, d), lambda i, j: (0, 0)),
                  pl.BlockSpec((1, 1, d), lambda i, j: (modrow(i), 0, 0)),
                  pl.BlockSpec((1, 1, d), lambda i, j: (modrow(i), 0, 0)),
                  pl.BlockSpec((d, tn), lambda i, j: (0, j)),
                  pl.BlockSpec((d, LANE), lambda i, j: (0, 0)),
                  pl.BlockSpec((1, LANE), lambda i, j: (0, 0))],
        out_specs=[pl.BlockSpec((tm, tn), lambda i, j: (i, j)),
                   pl.BlockSpec((tm, LANE), lambda i, j: (i, 0))],
        out_shape=[jax.ShapeDtypeStruct((r, n), BF16), jax.ShapeDtypeStruct((r, LANE), F32)],
        scratch_shapes=[pltpu.VMEM((tm, d), BF16)],
        compiler_params=_cparams(("parallel", "arbitrary"), vmem),
        name="even_inproj",
    )(h, g, shift, scale, w, wg, bg)


def _conv_kernel(x_ref, w_ref, post_ref, *rest, rows, grid_mode):
    o_ref = rest[-1]
    x = x_ref[...].astype(F32)
    t = lax.broadcasted_iota(jnp.int32, x.shape, 0)
    w = w_ref[...]
    if grid_mode:
        col = t % GRID_W
        has_prev, has_next = col > 0, col < GRID_W - 1
    else:
        has_prev, has_next = t > 0, t < rows - 1
    xl = jnp.where(has_prev, pltpu.roll(x, 1, 0), 0.0)
    xr = jnp.where(has_next, pltpu.roll(x, rows - 1, 0), 0.0)

    def tap_row(dr):
        return xl * w[3 * dr:3 * dr + 1] + x * w[3 * dr + 1:3 * dr + 2] + xr * w[3 * dr + 2:3 * dr + 3]

    y = tap_row(1)
    if grid_mode:
        z = jnp.zeros((GRID_W, x.shape[1]), F32)
        y = y + jnp.concatenate([z, tap_row(0)[:rows - GRID_W]], axis=0)
        y = y + jnp.concatenate([tap_row(2)[GRID_W:], z], axis=0)
    y = y * jax.nn.sigmoid(y)
    o_ref[...] = (y * post_ref[...]).astype(o_ref.dtype)


def _conv_qk(p, conv_w, post, dims):
    b, s, c = dims["B"], dims["S"], dims["C"]
    r = p.shape[0]
    nqk = dims["NQK"]
    ch = 2 * nqk
    tc = LANE
    col0 = 6 * nqk // tc
    w9 = conv_w.reshape(CONV_K * CONV_K, ch)
    out_shape = jax.ShapeDtypeStruct((r, ch), BF16)

    def call(rows, nblk, blk0, grid_mode, prev):
        in_specs = [pl.BlockSpec((rows, tc), lambda i, j: (blk0 + i, col0 + j)),
                    pl.BlockSpec((CONV_K * CONV_K, tc), lambda i, j: (0, j)),
                    pl.BlockSpec((1, tc), lambda i, j: (0, j))]
        args = [p, w9, post]
        aliases = {}
        if prev is not None:
            in_specs.append(pl.BlockSpec(memory_space=pl.ANY))
            args.append(prev)
            aliases = {3: 0}
        return pl.pallas_call(
            functools.partial(_conv_kernel, rows=rows, grid_mode=grid_mode),
            grid=(nblk, ch // tc),
            in_specs=in_specs,
            out_specs=pl.BlockSpec((rows, tc), lambda i, j: (blk0 + i, j)),
            out_shape=out_shape,
            input_output_aliases=aliases,
            compiler_params=_cparams(("parallel", "parallel"), 16 * rows * tc * 4),
            name="conv_lat" if grid_mode else "conv_ctx",
        )(*args)

    lat = call(s, b, 0, True, None)
    return call(c, b, (b * s) // c, False, lat)


def _chunk_maps(dims, chunk, rev):
    b_, s, c = dims["B"], dims["S"], dims["C"]
    ncc, ncl = c // chunk, s // chunk
    nc = ncc + ncl

    def pos(n):
        if not rev:
            return n
        return jnp.where(n < ncc, ncc - 1 - n, ncc + nc - 1 - n)

    def rowblk(b, n):
        q = pos(n)
        return jnp.where(q < ncc, (b_ * s) // chunk + b * ncc + q, b * ncl + q - ncc)

    return nc, pos, rowblk


def _retention_kernel(q_ref, k_ref, v_ref, cos_ref, sin_ref, intra_ref, qd_ref, kd_ref, cd_ref, y_ref, s_scr,
                      *, heads):
    @pl.when(pl.program_id(1) == 0)
    def _():
        s_scr[...] = jnp.zeros_like(s_scr)

    cos = cos_ref[...]
    sin = sin_ref[...]
    lane = lax.broadcasted_iota(jnp.int32, cos.shape, 1)
    first = (lane % (HEAD_DK // 2)) < (HEAD_DK // 4)

    def rope(x):
        swapped = jnp.where(first, pltpu.roll(x, HEAD_DK - HEAD_DK // 4, 1), pltpu.roll(x, HEAD_DK // 4, 1))
        return x * cos + swapped * sin

    for h in range(heads):
        ks = slice(h * HEAD_DK, (h + 1) * HEAD_DK)
        vs = slice(h * HEAD_DV, (h + 1) * HEAD_DV)
        q = rope(q_ref[:, ks].astype(F32))
        k = rope(k_ref[:, ks].astype(F32)) * HEAD_DK ** -0.5
        v = v_ref[:, vs]
        s = s_scr[h]
        sc = _dot_nt(q.astype(BF16), k.astype(BF16)) * intra_ref[h]
        o = _dot(sc.astype(BF16), v) + _dot((q * qd_ref[h]).astype(BF16), s.astype(BF16))
        s_scr[h] = s * cd_ref[h] + _dot_tn((k * kd_ref[h]).astype(BF16), v)
        y_ref[:, vs] = o.astype(y_ref.dtype)


def _retention_tables(heads, rev):
    h = jnp.arange(heads, dtype=F32) / max(heads - 1, 1)
    lg = jnp.log1p(-jnp.exp2(-(5.0 + 7.0 * h)))
    if rev:
        lg = lg[::-1]
    pos = jnp.arange(CHUNK, dtype=F32)
    diff = pos[:, None] - pos[None, :]
    if rev:
        diff = -diff
    intra = jnp.where(diff >= 0, jnp.exp(jnp.maximum(diff, 0.0)[None] * lg[:, None, None]), 0.0)
    q_pow = (CHUNK - pos) if rev else (pos + 1.0)
    k_pow = pos if rev else (CHUNK - 1.0 - pos)
    qd = jnp.exp(q_pow[None, :] * lg[:, None])[..., None]
    kd = jnp.exp(k_pow[None, :] * lg[:, None])[..., None]
    cd = jnp.exp(CHUNK * lg)[:, None, None]
    return intra, qd, kd, cd


def _retention(p, cos, sin, dims, rev):
    heads, nqk, nv = dims["H"], dims["NQK"], dims["NV"]
    r = p.shape[0]
    nc, pos, rowblk = _chunk_maps(dims, CHUNK, rev)
    intra, qd, kd, cd = _retention_tables(heads, rev)
    full = lambda shape: pl.BlockSpec(shape, lambda b, n: (0,) * len(shape))
    vmem = 2 * CHUNK * (2 * nqk + 2 * nv) * 2 + 2 * heads * CHUNK * (CHUNK + 2 * LANE) * 4 + heads * HEAD_DK * HEAD_DV * 4
    return pl.pallas_call(
        functools.partial(_retention_kernel, heads=heads),
        grid=(dims["B"], nc),
        in_specs=[pl.BlockSpec((CHUNK, nqk), lambda b, n: (rowblk(b, n), 0)),
                  pl.BlockSpec((CHUNK, nqk), lambda b, n: (rowblk(b, n), 1)),
                  pl.BlockSpec((CHUNK, nv), lambda b, n: (rowblk(b, n), 1)),
                  pl.BlockSpec((CHUNK, HEAD_DK), lambda b, n: (pos(n), 0)),
                  pl.BlockSpec((CHUNK, HEAD_DK), lambda b, n: (pos(n), 0)),
                  full((heads, CHUNK, CHUNK)), full((heads, CHUNK, 1)), full((heads, CHUNK, 1)),
                  full((heads, 1, 1))],
        out_specs=pl.BlockSpec((CHUNK, nv), lambda b, n: (rowblk(b, n), 0)),
        out_shape=jax.ShapeDtypeStruct((r, nv), BF16),
        scratch_shapes=[pltpu.VMEM((heads, HEAD_DK, HEAD_DV), F32)],
        compiler_params=_cparams(("parallel", "arbitrary"), vmem),
        name="retention_bwd" if rev else "retention_fwd",
    )(p, p, p, cos, sin, intra, qd, kd, cd)


def _rope_tables(dims):
    s, c = dims["S"], dims["C"]
    quarter = HEAD_DK // 4
    inv = jnp.power(ROPE_BASE, -jnp.arange(quarter, dtype=F32) / quarter)
    t = jnp.arange(s)
    rows = (t // GRID_W).astype(F32)
    cols = (t % GRID_W).astype(F32)
    ar = rows[:, None] * inv[None, :]
    ac = cols[:, None] * inv[None, :]
    cos = jnp.concatenate([jnp.cos(ar), jnp.cos(ar), jnp.cos(ac), jnp.cos(ac)], axis=-1)
    sin = jnp.concatenate([-jnp.sin(ar), jnp.sin(ar), -jnp.sin(ac), jnp.sin(ac)], axis=-1)
    cos = jnp.concatenate([jnp.ones((c, HEAD_DK), F32), cos], axis=0)
    sin = jnp.concatenate([jnp.zeros((c, HEAD_DK), F32), sin], axis=0)
    return cos, sin


def _mlstm_kernel(q_ref, k_ref, v_ref, g_ref, y_ref, c_scr, n_scr, m_scr, *, heads, rev):
    @pl.when(pl.program_id(1) == 0)
    def _():
        c_scr[...] = jnp.zeros_like(c_scr)
        n_scr[...] = jnp.zeros_like(n_scr)
        m_scr[...] = jnp.zeros_like(m_scr)

    ti = lax.broadcasted_iota(jnp.int32, (CHUNK, CHUNK), 0)
    si = lax.broadcasted_iota(jnp.int32, (CHUNK, CHUNK), 1)
    causal = (si >= ti) if rev else (si <= ti)
    tri = causal.astype(BF16)
    g = g_ref[...]
    logf = jnp.minimum(g, 0.0) - jnp.log1p(jnp.exp(-jnp.abs(g)))
    cs = _split_dot(tri, logf)
    g_t = g.T
    cs_t = cs.T
    icol0 = 2 * heads if rev else 0
    fcol0 = icol0 + heads
    last = 0 if rev else CHUNK - 1

    for h in range(heads):
        ks = slice(h * HEAD_DK, (h + 1) * HEAD_DK)
        vs = slice(h * HEAD_DV, (h + 1) * HEAD_DV)
        ic, fc = icol0 + h, fcol0 + h
        b_col, b_row = cs[:, fc:fc + 1], cs_t[fc:fc + 1, :]
        i_col, i_row = g[:, ic:ic + 1], g_t[ic:ic + 1, :]
        b_last = cs[last:last + 1, fc:fc + 1]
        m = m_scr[h]
        q = q_ref[:, ks]
        k = k_ref[:, ks]
        v = v_ref[:, vs]
        c_mem = c_scr[h]
        n_mem = n_scr[h]

        d_log = jnp.where(causal, b_col - b_row + i_row, -jnp.inf)
        m_inter = b_col + m
        m_row = jnp.maximum(jnp.max(d_log, axis=-1, keepdims=True), m_inter)
        w_intra = jnp.exp(d_log - m_row)
        w_inter = jnp.exp(m_inter - m_row)
        sc = _dot_nt(q, k) * w_intra
        num = _dot(sc.astype(BF16), v) + w_inter * _dot(q, c_mem.astype(BF16))
        den = jnp.sum(sc, axis=-1, keepdims=True) + w_inter * jnp.sum(q.astype(F32) * n_mem, axis=-1, keepdims=True)
        y_ref[:, vs] = (num / jnp.maximum(jnp.abs(den), jnp.exp(-m_row))).astype(y_ref.dtype)

        gk = b_last - b_col + i_col
        m_new = jnp.maximum(b_last + m, jnp.max(gk, axis=0, keepdims=True))
        w_k = jnp.exp(gk - m_new)
        w_c = jnp.exp(b_last + m - m_new)
        kw = k.astype(F32) * w_k
        c_scr[h] = w_c * c_mem + _dot_tn(kw.astype(BF16), v)
        n_scr[h] = w_c * n_mem + jnp.sum(kw, axis=0, keepdims=True)
        m_scr[h] = m_new


def _mlstm(qk, p, gates, dims, rev):
    heads, nqk, nv = dims["H"], dims["NQK"], dims["NV"]
    r = p.shape[0]
    nc, _, rowblk = _chunk_maps(dims, CHUNK, rev)
    vmem = 2 * CHUNK * (2 * nqk + 2 * nv) * 2 + 2 * CHUNK * LANE * 4 + heads * HEAD_DK * (HEAD_DV + 8) * 4
    return pl.pallas_call(
        functools.partial(_mlstm_kernel, heads=heads, rev=rev),
        grid=(dims["B"], nc),
        in_specs=[pl.BlockSpec((CHUNK, nqk), lambda b, n: (rowblk(b, n), 0)),
                  pl.BlockSpec((CHUNK, nqk), lambda b, n: (rowblk(b, n), 1)),
                  pl.BlockSpec((CHUNK, nv), lambda b, n: (rowblk(b, n), 4)),
                  pl.BlockSpec((CHUNK, LANE), lambda b, n: (rowblk(b, n), 0))],
        out_specs=pl.BlockSpec((CHUNK, nv), lambda b, n: (rowblk(b, n), 0)),
        out_shape=jax.ShapeDtypeStruct((r, nv), BF16),
        scratch_shapes=[pltpu.VMEM((heads, HEAD_DK, HEAD_DV), F32),
                        pltpu.VMEM((heads, 1, HEAD_DK), F32),
                        pltpu.VMEM((heads, 1, 1), F32)],
        compiler_params=_cparams(("parallel", "arbitrary"), vmem),
        name="mlstm_bwd" if rev else "mlstm_fwd",
    )(qk, qk, p, gates)


def _merge_kernel(yrf_ref, yrb_ref, ymf_ref, ymb_ref, ag_ref, bo_ref, gr_ref, gm_ref, z_ref, *, heads):
    nv = heads * HEAD_DV
    for h in range(heads):
        vs = slice(h * HEAD_DV, (h + 1) * HEAD_DV)
        yr = yrf_ref[:, vs].astype(F32) + yrb_ref[:, vs].astype(F32)
        yr = yr * lax.rsqrt(jnp.mean(yr * yr, axis=-1, keepdims=True) + EPS) * gr_ref[:, vs]
        ag = ag_ref[:, vs].astype(F32)
        z_ref[:, vs] = (yr * (ag * jax.nn.sigmoid(ag))).astype(z_ref.dtype)
        ym = ymf_ref[:, vs].astype(F32) + ymb_ref[:, vs].astype(F32)
        ym = ym * lax.rsqrt(jnp.mean(ym * ym, axis=-1, keepdims=True) + EPS) * gm_ref[:, vs]
        bo = bo_ref[:, vs].astype(F32)
        z_ref[:, nv + h * HEAD_DV:nv + (h + 1) * HEAD_DV] = (ym * jax.nn.sigmoid(bo)).astype(z_ref.dtype)


def _merge(yrf, yrb, ymf, ymb, p, gn_ret, gn_ml, dims, rows):
    heads, nv = dims["H"], dims["NV"]
    tm = _pow2_tile(128, dims["S"], dims["RC"])
    ys = pl.BlockSpec((tm, nv), lambda i: (i, 0))
    return pl.pallas_call(
        functools.partial(_merge_kernel, heads=heads),
        grid=(rows // tm,),
        in_specs=[ys, ys, ys, ys,
                  pl.BlockSpec((tm, nv), lambda i: (i, 2)),
                  pl.BlockSpec((tm, nv), lambda i: (i, 5)),
                  pl.BlockSpec((1, nv), lambda i: (0, 0)),
                  pl.BlockSpec((1, nv), lambda i: (0, 0))],
        out_specs=pl.BlockSpec((tm, 2 * nv), lambda i: (i, 0)),
        out_shape=jax.ShapeDtypeStruct((rows, 2 * nv), BF16),
        compiler_params=_cparams(("parallel",), 2 * tm * nv * 2 * 8 + 8 * tm * nv * 4),
        name="even_merge",
    )(yrf, yrb, ymf, ymb, p, p, gn_ret, gn_ml)


def _outproj_kernel(z_ref, w_ref, h_ref, gate_ref, g_ref, o_ref, acc_scr, *, tm, tn):
    j = pl.program_id(1)
    acc_scr[:, pl.ds(pl.multiple_of(j * tn, tn), tn)] = _dot(z_ref[...], w_ref[...])

    @pl.when(j == pl.num_programs(1) - 1)
    def _():
        _norm_residual_rows(acc_scr, h_ref, gate_ref, g_ref, o_ref, tm)


def _outproj(z, w, h, gate, g, dims, rows):
    k, n = w.shape
    tm = _pow2_tile(256, dims["S"], dims["RC"])
    tn = _pow2_tile(512, n)
    modrow = _mod_row_map(tm, dims["RL"] // tm, dims["S"], dims["B"])
    vmem = 2 * tm * k * 2 + 2 * k * tn * 2 + tm * n * 4 * 5
    return pl.pallas_call(
        functools.partial(_outproj_kernel, tm=tm, tn=tn),
        grid=(rows // tm, n // tn),
        in_specs=[pl.BlockSpec((tm, k), lambda i, j: (i, 0)),
                  pl.BlockSpec((k, tn), lambda i, j: (0, j)),
                  pl.BlockSpec((tm, n), lambda i, j: (i, 0)),
                  pl.BlockSpec((1, 1, n), lambda i, j: (modrow(i), 0, 0)),
                  pl.BlockSpec((1, n), lambda i, j: (0, 0))],
        out_specs=pl.BlockSpec((tm, n), lambda i, j: (i, 0)),
        out_shape=jax.ShapeDtypeStruct((rows, n), F32),
        scratch_shapes=[pltpu.VMEM((tm, n), F32)],
        compiler_params=_cparams(("parallel", "arbitrary"), vmem),
        name="outproj_norm_residual",
    )(z, w, h, gate, g)


def _mlp_kernel(h_ref, gpre_ref, sh_ref, sc_ref, w1_ref, w2_ref, gate_ref, gpost_ref, o_ref, u_scr, *, tm):
    f = pl.program_id(1)

    @pl.when(f == 0)
    def _():
        _modulate_rows(h_ref, gpre_ref, sh_ref, sc_ref, u_scr, tm)
        o_ref[...] = jnp.zeros_like(o_ref)

    a = jnp.maximum(_dot(u_scr[...], w1_ref[...]), 0.0)
    o_ref[...] += _dot((a * a).astype(BF16), w2_ref[...])

    @pl.when(f == pl.num_programs(1) - 1)
    def _():
        _norm_residual_rows(o_ref, h_ref, gate_ref, gpost_ref, o_ref, tm)


def _mlp(h, g_pre, shift, scale, w1, w2, gate, g_post, dims, rows):
    d, ff = w1.shape
    tm = _pow2_tile(512, dims["S"], dims["RC"])
    tf = _pow2_tile(512, ff)
    modrow = _mod_row_map(tm, dims["RL"] // tm, dims["S"], dims["B"])
    mod_spec = pl.BlockSpec((1, 1, d), lambda i, f: (modrow(i), 0, 0))
    row_spec = pl.BlockSpec((1, d), lambda i, f: (0, 0))
    vmem = 3 * tm * d * 4 + tm * d * 2 + 4 * d * tf * 2 + 3 * tm * tf * 4
    return pl.pallas_call(
        functools.partial(_mlp_kernel, tm=tm),
        grid=(rows // tm, ff // tf),
        in_specs=[pl.BlockSpec((tm, d), lambda i, f: (i, 0), pipeline_mode=pl.Buffered(1)),
                  row_spec, mod_spec, mod_spec,
                  pl.BlockSpec((d, tf), lambda i, f: (0, f)),
                  pl.BlockSpec((tf, d), lambda i, f: (f, 0)),
                  mod_spec, row_spec],
        out_specs=pl.BlockSpec((tm, d), lambda i, f: (i, 0)),
        out_shape=jax.ShapeDtypeStruct((rows, d), F32),
        scratch_shapes=[pltpu.VMEM((tm, d), BF16)],
        compiler_params=_cparams(("parallel", "arbitrary"), vmem),
        name="mlp_block",
    )(h, g_pre, shift, scale, w1, w2, gate, g_post)


HALO = GRID_W


def _rwkv_prep_kernel(h_ref, ha_ref, hb_ref, g_ref, sh_ref, sc_ref, mu_ref, o_ref, rs_scr, ra_scr, rb_scr,
                      *, tm, tc, n_lat_tiles, s, c):
    i = pl.program_id(0)
    jc = pl.program_id(1)
    is_lat = i < n_lat_tiles
    cs = pl.ds(pl.multiple_of(jc * tc, tc), tc)

    @pl.when(jc == 0)
    def _():
        def rstd(x):
            return lax.rsqrt(jnp.mean(x * x, axis=-1, keepdims=True) + EPS)

        def body(r, carry):
            sl = pl.ds(pl.multiple_of(r * ROW_BLOCK, ROW_BLOCK), ROW_BLOCK)
            rs_scr[sl, :] = rstd(h_ref[sl, :])
            return carry

        lax.fori_loop(0, tm // ROW_BLOCK, body, 0)
        ra_scr[...] = rstd(ha_ref[...])
        rb_scr[...] = rstd(hb_ref[...])

    g = g_ref[...]
    sh = sh_ref[0]
    sc = sc_ref[0]

    def mod(x, rs):
        return (x * rs * g) * (1.0 + sc) + sh

    u = mod(h_ref[:, cs], rs_scr[...])
    t = lax.broadcasted_iota(jnp.int32, u.shape, 0)
    period = jnp.where(is_lat, GRID_W, c)
    tpos = t & (period - 1)

    def emit(shifted):
        xx = shifted - u
        for m in range(6):
            o_ref[m] = (u + xx * mu_ref[m]).astype(o_ref.dtype)

    use_prev = (jc == 0) | (jnp.logical_not(is_lat) & (jc == 1))
    use_next = (is_lat & (jc == 1)) | (jnp.logical_not(is_lat) & (jc >= 2))

    @pl.when(use_prev)
    def _():
        emit(jnp.where(tpos > 0, pltpu.roll(u, 1, 0), 0.0))

    @pl.when(use_next)
    def _():
        emit(jnp.where(tpos < period - 1, pltpu.roll(u, tm - 1, 0), 0.0))

    @pl.when(is_lat & (jc == 2))
    def _():
        first = (i * tm) % s == 0
        above = jnp.where(first, 0.0, mod(ha_ref[:, cs], ra_scr[...]))
        emit(jnp.concatenate([above, u[:tm - HALO]], axis=0))

    @pl.when(is_lat & (jc == 3))
    def _():
        lastt = ((i + 1) * tm) % s == 0
        below = jnp.where(lastt, 0.0, mod(hb_ref[:, cs], rb_scr[...]))
        emit(jnp.concatenate([u[HALO:], below], axis=0))


def _rwkv_prep(h, g, shift, scale, mu6, dims):
    r, d = h.shape
    s, c = dims["S"], dims["C"]
    tm = c
    assert tm % HALO == 0 and s % tm == 0 and tm & (tm - 1) == 0
    tc = d // 4
    hb = tm // HALO
    nblk = r // HALO
    n_lat_tiles = dims["RL"] // tm
    modrow = _mod_row_map(tm, n_lat_tiles, s, dims["B"])
    mod_spec = pl.BlockSpec((1, 1, tc), lambda i, j: (modrow(i), 0, j))
    vmem = 2 * tm * d * 4 + 4 * HALO * d * 4 + 2 * 6 * tm * tc * 2 + 12 * tm * tc * 4
    return pl.pallas_call(
        functools.partial(_rwkv_prep_kernel, tm=tm, tc=tc, n_lat_tiles=n_lat_tiles, s=s, c=c),
        grid=(r // tm, 4),
        in_specs=[pl.BlockSpec((tm, d), lambda i, j: (i, 0)),
                  pl.BlockSpec((HALO, d), lambda i, j: (jnp.maximum(i * hb - 1, 0), 0)),
                  pl.BlockSpec((HALO, d), lambda i, j: (jnp.minimum((i + 1) * hb, nblk - 1), 0)),
                  pl.BlockSpec((1, tc), lambda i, j: (0, j)),
                  mod_spec, mod_spec,
                  pl.BlockSpec((6, 1, tc), lambda i, j: (0, 0, j))],
        out_specs=pl.BlockSpec((6, tm, tc), lambda i, j: (0, i, j)),
        out_shape=jax.ShapeDtypeStruct((6, r, d), BF16),
        scratch_shapes=[pltpu.VMEM((tm, 1), F32), pltpu.VMEM((HALO, 1), F32), pltpu.VMEM((HALO, 1), F32)],
        compiler_params=_cparams(("parallel", "arbitrary"), vmem),
        name="rwkv_prep",
    )(h, h, h, g, shift, scale, mu6)


def _bmm_kernel(x_ref, w_ref, o_ref):
    o_ref[0] = _dot(x_ref[0], w_ref[0]).astype(o_ref.dtype)


def _bmm(x, w, lhs0, out_dtype, dims):
    ng, k, n = w.shape
    r = x.shape[1]
    tm = _pow2_tile(512, dims["S"], dims["RC"])
    tn = _pow2_tile(1024, n)
    osz = jnp.dtype(out_dtype).itemsize
    return pl.pallas_call(
        _bmm_kernel,
        grid=(ng, r // tm, n // tn),
        in_specs=[pl.BlockSpec((1, tm, k), lambda g, i, j: (lhs0 + g, i, 0)),
                  pl.BlockSpec((1, k, tn), lambda g, i, j: (g, 0, j))],
        out_specs=pl.BlockSpec((1, tm, tn), lambda g, i, j: (g, i, j)),
        out_shape=jax.ShapeDtypeStruct((ng, r, n), out_dtype),
        compiler_params=_cparams(("parallel", "parallel", "arbitrary"),
                                 2 * tm * k * 2 + 2 * k * tn * 2 + 2 * tm * tn * osz + tm * tn * 4),
        name="rwkv_bmm",
    )(x, w)


def _pair_ones():
    a = lax.broadcasted_iota(jnp.int32, (LANE, LANE), 0) // RWKV_N
    b = lax.broadcasted_iota(jnp.int32, (LANE, LANE), 1) // RWKV_N
    return (a == b).astype(BF16)


def _head_sums(x, ones):
    parts = [_split_dot_r(x[:, l:l + LANE], ones) for l in range(0, x.shape[1], LANE)]
    return parts[0] if len(parts) == 1 else jnp.concatenate(parts, axis=1)


def _rwkv_mid_kernel(k_ref, hid_ref, w2_ref, a2_ref, g2_ref, w0_ref, a0_ref, kk_w_ref,
                     lw_ref, a_ref, kk_ref, gate_ref, *, lp):
    ones = _pair_ones()
    hw = hid_ref[0]
    ha = hid_ref[1]
    hg = hid_ref[2]
    for d in range(2):
        tw = jnp.tanh(hw[:, d * lp:(d + 1) * lp]).astype(BF16)
        z = w0_ref[d] + _dot(tw, w2_ref[d])
        w_log = jnp.minimum(z, 0.0) - jnp.log1p(jnp.exp(-jnp.abs(z))) - 0.5
        lw_ref[d] = -jnp.exp(w_log)
        za = a0_ref[d] + _dot(ha[:, d * lp:(d + 1) * lp].astype(BF16), a2_ref[d])
        a_ref[d] = jax.nn.sigmoid(za).astype(a_ref.dtype)
    gate_ref[...] = _dot(jax.nn.sigmoid(hg).astype(BF16), g2_ref[...]).astype(gate_ref.dtype)
    kk = k_ref[0].astype(F32) * kk_w_ref[...]
    ss = _head_sums(kk * kk, ones)
    kk_ref[...] = (kk * lax.rsqrt(jnp.maximum(ss, 1e-24))).astype(kk_ref.dtype)


def _rwkv_mid(rkv, hid, w2, a2, g2, w0, a0, k_k, dims):
    _, r, d = rkv.shape
    lp = w2.shape[1]
    lw_tot = hid.shape[2]
    tm = _pow2_tile(256, dims["S"], dims["RC"])
    tc = _pow2_tile(1024, d)
    vmem = 2 * (tm * tc * 2 + 3 * tm * lw_tot * 4 + (4 * lp + g2.shape[0]) * tc * 2
                + tm * tc * (8 + 4 + 2 + 2)) + 10 * tm * tc * 4
    return pl.pallas_call(
        functools.partial(_rwkv_mid_kernel, lp=lp),
        grid=(r // tm, d // tc),
        in_specs=[pl.BlockSpec((1, tm, tc), lambda i, j: (1, i, j)),
                  pl.BlockSpec((3, tm, lw_tot), lambda i, j: (0, i, 0)),
                  pl.BlockSpec((2, lp, tc), lambda i, j: (0, 0, j)),
                  pl.BlockSpec((2, lp, tc), lambda i, j: (0, 0, j)),
                  pl.BlockSpec((g2.shape[0], tc), lambda i, j: (0, j)),
                  pl.BlockSpec((2, 1, tc), lambda i, j: (0, 0, j)),
                  pl.BlockSpec((2, 1, tc), lambda i, j: (0, 0, j)),
                  pl.BlockSpec((1, tc), lambda i, j: (0, j))],
        out_specs=[pl.BlockSpec((2, tm, tc), lambda i, j: (0, i, j)),
                   pl.BlockSpec((2, tm, tc), lambda i, j: (0, i, j)),
                   pl.BlockSpec((tm, tc), lambda i, j: (i, j)),
                   pl.BlockSpec((tm, tc), lambda i, j: (i, j))],
        out_shape=[jax.ShapeDtypeStruct((2, r, d), F32), jax.ShapeDtypeStruct((2, r, d), BF16),
                   jax.ShapeDtypeStruct((r, d), BF16), jax.ShapeDtypeStruct((r, d), BF16)],
        compiler_params=_cparams(("parallel", "arbitrary"), vmem),
        name="rwkv_mid",
    )(rkv, hid, w2, a2, g2, w0, a0, k_k)


def _rwkv_scan_kernel(r_ref, k_ref, v_ref, kk_ref, a_ref, lw_ref, ka_ref, y_ref, s_scr, *, rev, npairs):
    @pl.when(pl.program_id(2) == 0)
    def _():
        s_scr[...] = jnp.zeros_like(s_scr)

    ln = RWKV_CHUNK
    row = lax.broadcasted_iota(jnp.int32, (ln, LANE), 0)
    lane = lax.broadcasted_iota(jnp.int32, (ln, LANE), 1)
    col = lane % RWKV_N
    head0 = lane < RWKV_N
    strict = (col > row) if rev else (col < row)
    incl = (col >= row) if rev else (col <= row)
    eye = (col == row).astype(F32)

    def same_block(size):
        return (row // size) == (col // size)

    leaf = same_block(INV_LEAF)
    offs = [same_block(2 * sz) & jnp.logical_not(same_block(sz))
            for sz in (INV_LEAF << i for i in range(int(math.log2(ln // INV_LEAF))))]
    ti = lax.broadcasted_iota(jnp.int32, (ln, ln), 0)
    si = lax.broadcasted_iota(jnp.int32, (ln, ln), 1)
    tri = ((si >= ti) if rev else (si <= ti)).astype(BF16)
    ra = lax.broadcasted_iota(jnp.int32, (LANE, LANE), 0) // RWKV_N
    rb = lax.broadcasted_iota(jnp.int32, (LANE, LANE), 1) // RWKV_N
    same_head = ra == rb
    last = 0 if rev else ln - 1

    def bd(x):
        zero = jnp.zeros_like(x)
        return jnp.concatenate([jnp.where(head0, x, zero), jnp.where(head0, zero, x)], axis=0)

    def pm(a, b):
        return _dot(a.astype(BF16), bd(b.astype(BF16)))

    pairs = range(npairs)
    sls = [slice(p * LANE, (p + 1) * LANE) for p in pairs]

    def prepare(sl):
        lw = lw_ref[0, :, sl]
        c = _split_dot(tri, lw)
        a = a_ref[0, :, sl].astype(F32)
        kk = kk_ref[:, sl].astype(F32)
        winv = jnp.exp(-c)
        kd = k_ref[0, :, sl].astype(F32) * (1.0 + (a - 1.0) * ka_ref[:, sl])
        at = (-(kk * jnp.exp(c - lw))).astype(BF16)
        bt = (kk * a * winv).astype(BF16)
        kt = (kd * winv).astype(BF16)
        rt = (r_ref[0, :, sl].astype(F32) * jnp.exp(c)).astype(BF16)
        return at, bt, kt, rt, jnp.exp(c[last:last + 1, :])

    prep = [prepare(sl) for sl in sls]
    res = [_dot_nt(jnp.concatenate([at, rt], axis=0), jnp.concatenate([bd(bt), bd(kt)], axis=0))
           for at, bt, kt, rt, _ in prep]
    mab = [jnp.where(strict, x[:ln, :LANE], 0.0) for x in res]
    mak = [jnp.where(strict, x[:ln, LANE:], 0.0).astype(BF16) for x in res]
    rbk = [jnp.concatenate([jnp.where(incl, x[ln:, :LANE], 0.0), jnp.where(incl, x[ln:, LANE:], 0.0)],
                           axis=1).astype(BF16) for x in res]

    md = [jnp.where(leaf, m, 0.0) for m in mab]
    sq = [pm(m, m) for m in md]
    tinv = [eye + m + pm(q, eye + m) for m, q in zip(md, sq)]
    for off in offs:
        half = [pm(t, jnp.where(off, m, 0.0)) for t, m in zip(tinv, mab)]
        tinv = [t + pm(hf, t) for t, hf in zip(tinv, half)]

    s0 = [s_scr[p] for p in pairs]
    s0b = [s.astype(BF16) for s in s0]
    vs = [v_ref[0, :, sl] for sl in sls]
    x = [_dot_nt(pr[0], sb) + _dot(mk, bd(v)) for pr, sb, mk, v in zip(prep, s0b, mak, vs)]
    ub = [pm(t, xx).astype(BF16) for t, xx in zip(tinv, x)]
    for p in pairs:
        at, bt, kt, rt, wtot = prep[p]
        y = _dot_nt(rt, s0b[p]) + _dot(rbk[p], jnp.concatenate([bd(ub[p]), bd(vs[p])], axis=0))
        y_ref[:, sls[p]] = y.astype(y_ref.dtype)
        upd = _dot_tn(jnp.concatenate([ub[p], vs[p]], axis=0), jnp.concatenate([bt, kt], axis=0))
        s_scr[p] = (s0[p] + jnp.where(same_head, upd, 0.0)) * wtot


def _rwkv_scan(rkv, kk, a, lw, k_a, dims, rev):
    _, r, d = rkv.shape
    npairs = min(32, d // LANE)
    pw = npairs * LANE
    ln = RWKV_CHUNK
    nc, _, rowblk = _chunk_maps(dims, ln, rev)
    dsel = 1 if rev else 0

    def spec3(lead):
        return pl.BlockSpec((1, ln, pw), lambda b, g, n: (lead, rowblk(b, n), g))

    return pl.pallas_call(
        functools.partial(_rwkv_scan_kernel, rev=rev, npairs=npairs),
        grid=(dims["B"], d // pw, nc),
        in_specs=[spec3(0), spec3(1), spec3(2),
                  pl.BlockSpec((ln, pw), lambda b, g, n: (rowblk(b, n), g)),
                  spec3(dsel), spec3(dsel),
                  pl.BlockSpec((1, pw), lambda b, g, n: (0, g))],
        out_specs=pl.BlockSpec((ln, pw), lambda b, g, n: (rowblk(b, n), g)),
        out_shape=jax.ShapeDtypeStruct((r, d), BF16),
        scratch_shapes=[pltpu.VMEM((npairs, LANE, LANE), F32)],
        compiler_params=_cparams(("parallel", "parallel", "arbitrary"),
                                 2 * ln * pw * (5 * 2 + 4 + 2) + npairs * LANE * LANE * 4 + (8 << 20)),
        name="rwkv_scan_bwd" if rev else "rwkv_scan_fwd",
    )(rkv, rkv, rkv, kk, a, lw, k_a)


def _rwkv_finish_kernel(yf_ref, yb_ref, r_ref, k_ref, v_ref, a_ref, gate_ref, ka_ref, rk_ref, lw_ref, lb_ref, z_ref):
    ones = _pair_ones()
    inv_n = 1.0 / RWKV_N
    y = yf_ref[...].astype(F32) + yb_ref[...].astype(F32)
    mean = _head_sums(y, ones) * inv_n
    yc = y - mean
    var = _head_sums(yc * yc, ones) * inv_n
    yn = yc * lax.rsqrt(var + LNX_EPS) * lw_ref[...] + lb_ref[...]
    k = k_ref[0].astype(F32)
    ka = ka_ref[...]
    kd0 = k * (1.0 + (a_ref[0].astype(F32) - 1.0) * ka)
    kd1 = k * (1.0 + (a_ref[1].astype(F32) - 1.0) * ka)
    kb = 0.5 * (kd0 + kd1)
    bonus = _head_sums(r_ref[0].astype(F32) * kb * rk_ref[...], ones) * v_ref[0].astype(F32)
    z_ref[...] = ((yn + bonus) * gate_ref[...].astype(F32)).astype(z_ref.dtype)


def _rwkv_finish(yf, yb, rkv, a, gate, k_a, r_k, lnx_w, lnx_b, dims, rows):
    d = yf.shape[1]
    tm = _pow2_tile(256, dims["S"], dims["RC"])
    tc = _pow2_tile(512, d)
    blk = pl.BlockSpec((tm, tc), lambda i, j: (i, j))
    vec = pl.BlockSpec((1, tc), lambda i, j: (0, j))

    def lead(g):
        return pl.BlockSpec((1, tm, tc), lambda i, j: (g, i, j))

    return pl.pallas_call(
        _rwkv_finish_kernel,
        grid=(rows // tm, d // tc),
        in_specs=[blk, blk, lead(0), lead(1), lead(2),
                  pl.BlockSpec((2, tm, tc), lambda i, j: (0, i, j)), blk, vec, vec, vec, vec],
        out_specs=blk,
        out_shape=jax.ShapeDtypeStruct((rows, d), BF16),
        compiler_params=_cparams(("parallel", "parallel"), 2 * tm * tc * 2 * 9 + 16 * tm * tc * 4),
        name="rwkv_finish",
    )(yf, yb, rkv, rkv, rkv, a, gate, k_a, r_k, lnx_w, lnx_b)


def _even_layer(h, mod, norm_g, w_in, b_gate, conv_qk, gn_ret, gn_ml, w_out, dims, rows_out):
    d = h.shape[1]
    heads, nqk, nv = dims["H"], dims["NQK"], dims["NV"]
    n_main = 4 * nqk + 4 * nv
    n_gate = 4 * heads
    w_main = w_in[:, :n_main].astype(BF16)
    w_gate = jnp.pad(w_in[:, n_main:], ((0, 0), (0, LANE - n_gate))).astype(BF16)
    bg = jnp.pad(b_gate, (0, LANE - n_gate)).reshape(1, LANE)
    shift, scale, gate = mod[0], mod[1], mod[2]
    p, gates = _inproj(h, norm_g[0].reshape(1, d), shift, scale, w_main, w_gate, bg, dims)
    post = jnp.concatenate([jnp.ones((nqk,), F32), jnp.full((nqk,), HEAD_DK ** -0.5, F32)]).reshape(1, 2 * nqk)
    qk = _conv_qk(p, conv_qk, post, dims)
    cos, sin = _rope_tables(dims)
    yrf = _retention(p, cos, sin, dims, False)
    yrb = _retention(p, cos, sin, dims, True)
    ymf = _mlstm(qk, p, gates, dims, False)
    ymb = _mlstm(qk, p, gates, dims, True)
    z = _merge(yrf, yrb, ymf, ymb, p, gn_ret.reshape(1, nv), gn_ml.reshape(1, nv), dims, rows_out)
    return _outproj(z, w_out.astype(BF16), h, gate, norm_g[1].reshape(1, d), dims, rows_out)


def _rwkv_layer(h, mod, norm_g, mu, w_rkv, w0, w1, w2, a0, a1, a2, g1, g2, k_k, k_a, r_k, lnx_w, lnx_b, w_out,
                dims, rows_out):
    d = h.shape[1]
    shift, scale, gate = mod[0], mod[1], mod[2]
    order = jnp.array([0, 2, 3, 1, 4, 5])
    mu6 = mu[order].reshape(6, 1, d)
    x6 = _rwkv_prep(h, norm_g[0].reshape(1, d), shift, scale, mu6, dims)
    rkv = _bmm(x6, w_rkv.astype(BF16), 0, BF16, dims)
    lora = w1.shape[2]
    glora = g1.shape[1]
    lp = _round_up(lora, LANE)
    gp = _round_up(glora, LANE)
    lw_tot = max(2 * lp, gp)

    def two(w):
        w = jnp.pad(w, ((0, 0), (0, 0), (0, lp - lora)))
        w = jnp.concatenate([w[0], w[1]], axis=1)
        return jnp.pad(w, ((0, 0), (0, lw_tot - 2 * lp)))

    w_l1 = jnp.stack([two(w1), two(a1), jnp.pad(g1, ((0, 0), (0, lw_tot - glora)))]).astype(BF16)
    hid = _bmm(x6, w_l1, 3, F32, dims)
    pad_rows = lambda w, n: jnp.pad(w, ((0, 0),) * (w.ndim - 2) + ((0, n - w.shape[-2]), (0, 0))).astype(BF16)
    lw, a, kk, gmul = _rwkv_mid(rkv, hid, pad_rows(w2, lp), pad_rows(a2, lp), pad_rows(g2, lw_tot),
                                w0.reshape(2, 1, d), a0.reshape(2, 1, d), k_k.reshape(1, d), dims)
    ka = k_a.reshape(1, d)
    yf = _rwkv_scan(rkv, kk, a, lw, ka, dims, False)
    yb = _rwkv_scan(rkv, kk, a, lw, ka, dims, True)
    z = _rwkv_finish(yf, yb, rkv, a, gmul, ka, r_k.reshape(1, d), lnx_w.reshape(1, d), lnx_b.reshape(1, d),
                     dims, rows_out)
    return _outproj(z, w_out.astype(BF16), h, gate, norm_g[1].reshape(1, d), dims, rows_out)


def kernel(x, c, ctx, c_ctx, ada_w, ada_b, norm_g, mlp_w_in, mlp_w_out, ev_w_in, ev_b_gate, ev_conv_qk,
           ev_gn_ret, ev_gn_mlstm, ev_w_out, od_mu, od_w_rkv, od_w0, od_w1, od_w2, od_a0, od_a1, od_a2,
           od_g1, od_g2, od_k_k, od_k_a, od_r_k, od_lnx_w, od_lnx_b, od_w_out):
    b, s, d = x.shape
    cl = ctx.shape[1]
    depth = ada_w.shape[0]
    heads = d // (2 * HEAD_DV)
    assert b + 1 <= MOD_ROWS and s % CHUNK == 0 and cl % CHUNK == 0 and s % GRID_W == 0
    dims = dict(B=b, S=s, C=cl, RL=b * s, RC=b * cl, H=heads, NQK=heads * HEAD_DK, NV=heads * HEAD_DV)
    rl, r = dims["RL"], dims["RL"] + dims["RC"]

    cond = jnp.concatenate([c, c_ctx[None, :], jnp.zeros((MOD_ROWS - b - 1, d), F32)], axis=0)
    tables = _ada_tables(cond, ada_w, ada_b)
    h = jnp.concatenate([x.reshape(rl, d), ctx.reshape(b * cl, d)], axis=0)

    for layer in range(depth):
        with_ctx = layer < depth - 1
        rows_out = r if with_ctx else rl
        mod = tables[layer].reshape(MOD_ROWS, 6, 1, d).transpose(1, 0, 2, 3)
        g = norm_g[layer]
        if layer % 2 == 0:
            e = layer // 2
            h_mix = _even_layer(h, mod[0:3], g[0:2], ev_w_in[e], ev_b_gate[e], ev_conv_qk[e], ev_gn_ret[e],
                                ev_gn_mlstm[e], ev_w_out[e], dims, rows_out)
        else:
            o = layer // 2
            h_mix = _rwkv_layer(h, mod[0:3], g[0:2], od_mu[o], od_w_rkv[o], od_w0[o], od_w1[o], od_w2[o],
                                od_a0[o], od_a1[o], od_a2[o], od_g1[o], od_g2[o], od_k_k[o], od_k_a[o], od_r_k[o],
                                od_lnx_w[o], od_lnx_b[o], od_w_out[o], dims, rows_out)
        h = _mlp(h_mix, g[2].reshape(1, d), mod[3], mod[4], mlp_w_in[layer].astype(BF16),
                 mlp_w_out[layer].astype(BF16), mod[5], g[3].reshape(1, d), dims, rows_out)
    return h[:rl].reshape(b, s, d)
```

```python
import functools
import math

import jax
import jax.numpy as jnp
from jax import lax
from jax.experimental import pallas as pl
from jax.experimental.pallas import tpu as pltpu

F32 = jnp.float32
BF16 = jnp.bfloat16

EPS = 1e-6
LNX_EPS = 64e-5
ROPE_BASE = 10000.0
GRID_W = 64
CHUNK = 128
RWKV_CHUNK = 64
HEAD_DK = 128
HEAD_DV = 256
RWKV_N = 64
INV_LEAF = 4
CONV_K = 3
LANE = 128
MOD_ROWS = 8
VMEM_BYTES = 64 * 1024 * 1024

NT_DIMS = (((1,), (1,)), ((), ()))
TN_DIMS = (((0,), (0,)), ((), ()))


def _dot(a, b):
    return jnp.dot(a, b, preferred_element_type=F32)


def _dot_nt(a, b):
    return lax.dot_general(a, b, NT_DIMS, preferred_element_type=F32)


def _dot_tn(a, b):
    return lax.dot_general(a, b, TN_DIMS, preferred_element_type=F32)


def _split_dot(m, x):
    hi = x.astype(BF16)
    lo = (x - hi.astype(F32)).astype(BF16)
    return _dot(m, hi) + _dot(m, lo)


def _split_dot_r(x, m):
    hi = x.astype(BF16)
    lo = (x - hi.astype(F32)).astype(BF16)
    return _dot(hi, m) + _dot(lo, m)


def _cparams(semantics, vmem_bytes):
    limit = int(min(max(vmem_bytes * 5 // 4 + (4 << 20), 16 << 20), VMEM_BYTES - (6 << 20)))
    return pltpu.CompilerParams(dimension_semantics=semantics, vmem_limit_bytes=limit)


def _pow2_tile(pref, *dims):
    t = pref
    while any(d % t for d in dims):
        t //= 2
    return t


def _round_up(x, m):
    return (x + m - 1) // m * m


CAST_BLOCK_ELEMS = 1 << 20


def _cast_kernel(x_ref, o_ref):
    o_ref[...] = x_ref[...].astype(o_ref.dtype)


def _to_bf16(w, lead, ncols=None):
    shape = w.shape[1:]
    w3 = w.reshape(w.shape[0], -1, shape[-1])
    rows = w3.shape[1]
    ncols = shape[-1] if ncols is None else ncols
    tc = _pow2_tile(4096, ncols)
    tr = _pow2_tile(max(CAST_BLOCK_ELEMS // tc, 16), rows)
    out = pl.pallas_call(
        _cast_kernel,
        grid=(rows // tr, ncols // tc),
        in_specs=[pl.BlockSpec((None, tr, tc), lambda i, j: (lead, i, j))],
        out_specs=pl.BlockSpec((tr, tc), lambda i, j: (i, j)),
        out_shape=jax.ShapeDtypeStruct((rows, ncols), BF16),
        compiler_params=_cparams(("parallel", "parallel"), 2 * tr * tc * 6),
        name="weight_cast",
    )(w3)
    return out.reshape(shape[:-1] + (ncols,))


def _ada_kernel(c_ref, w_ref, b_ref, o_ref):
    c = c_ref[...]
    s = c * jax.nn.sigmoid(c)
    o_ref[0] = _dot(s.astype(BF16), w_ref[0].astype(BF16)) + b_ref[0]


def _ada_tables(cond, ada_w, ada_b):
    depth, d, n = ada_w.shape
    tn = _pow2_tile(512, n)
    return pl.pallas_call(
        _ada_kernel,
        grid=(depth, n // tn),
        in_specs=[pl.BlockSpec((MOD_ROWS, d), lambda l, j: (0, 0)),
                  pl.BlockSpec((1, d, tn), lambda l, j: (l, 0, j)),
                  pl.BlockSpec((1, 1, tn), lambda l, j: (l, 0, j))],
        out_specs=pl.BlockSpec((1, MOD_ROWS, tn), lambda l, j: (l, 0, j)),
        out_shape=jax.ShapeDtypeStruct((depth, MOD_ROWS, n), F32),
        compiler_params=_cparams(("parallel", "parallel"), 2 * d * tn * 4 + d * tn * 2),
        name="ada_tables",
    )(cond, ada_w, ada_b.reshape(depth, 1, n))


ROW_BLOCK = 32


def _modulate_rows(h_ref, g_ref, sh_ref, sc_ref, u_scr, tm):
    g = g_ref[...]
    sh = sh_ref[0]
    sc = sc_ref[0]

    def body(r, carry):
        sl = pl.ds(pl.multiple_of(r * ROW_BLOCK, ROW_BLOCK), ROW_BLOCK)
        x = h_ref[sl, :]
        ms = jnp.mean(x * x, axis=-1, keepdims=True)
        u = (x * lax.rsqrt(ms + EPS) * g) * (1.0 + sc) + sh
        u_scr[sl, :] = u.astype(u_scr.dtype)
        return carry

    lax.fori_loop(0, tm // ROW_BLOCK, body, 0)


def _norm_residual_rows(acc_ref, h_ref, gate_ref, g_ref, o_ref, tm):
    g = g_ref[...]
    gate = gate_ref[0]

    def body(r, carry):
        sl = pl.ds(pl.multiple_of(r * ROW_BLOCK, ROW_BLOCK), ROW_BLOCK)
        a = acc_ref[sl, :]
        ms = jnp.mean(a * a, axis=-1, keepdims=True)
        o_ref[sl, :] = h_ref[sl, :] + gate * (a * lax.rsqrt(ms + EPS) * g)
        return carry

    lax.fori_loop(0, tm // ROW_BLOCK, body, 0)


def _mod_row_map(tm, n_lat_tiles, s, b):
    def f(i):
        return jnp.where(i < n_lat_tiles, (i * tm) // s, b)
    return f


def _inproj_kernel(h_ref, g_ref, sh_ref, sc_ref, w_ref, wg_ref, bg_ref, p_ref, gt_ref, u_scr, *, tm):
    @pl.when(pl.program_id(1) == 0)
    def _():
        _modulate_rows(h_ref, g_ref, sh_ref, sc_ref, u_scr, tm)
        gt_ref[...] = _dot(u_scr[...], wg_ref[...]) + bg_ref[...]

    p_ref[...] = _dot(u_scr[...], w_ref[...]).astype(p_ref.dtype)


def _inproj(h, g, shift, scale, w, wg, bg, dims):
    r, d = h.shape
    n = w.shape[1]
    tm = _pow2_tile(512, dims["S"], dims["RC"])
    tn = _pow2_tile(1024, n)
    modrow = _mod_row_map(tm, dims["RL"] // tm, dims["S"], dims["B"])
    vmem = 2 * tm * d * 4 + tm * d * 2 + 2 * d * tn * 2 + 2 * tm * tn * 2 + 2 * d * LANE * 2 + 2 * tm * LANE * 4
    return pl.pallas_call(
        functools.partial(_inproj_kernel, tm=tm),
        grid=(r // tm, n // tn),
        in_specs=[pl.BlockSpec((tm, d), lambda i, j: (i, 0)),
                  pl.BlockSpec((1, d), lambda i, j: (0, 0)),
                  pl.BlockSpec((1, 1, d), lambda i, j: (modrow(i), 0, 0)),
                  pl.BlockSpec((1, 1, d), lambda i, j: (modrow(i), 0, 0)),
                  pl.BlockSpec((d, tn), lambda i, j: (0, j)),
                  pl.BlockSpec((d, LANE), lambda i, j: (0, 0)),
                  pl.BlockSpec((1, LANE), lambda i, j: (0, 0))],
        out_specs=[pl.BlockSpec((tm, tn), lambda i, j: (i, j)),
                   pl.BlockSpec((tm, LANE), lambda i, j: (i, 0))],
        out_shape=[jax.ShapeDtypeStruct((r, n), BF16), jax.ShapeDtypeStruct((r, LANE), F32)],
        scratch_shapes=[pltpu.VMEM((tm, d), BF16)],
        compiler_params=_cparams(("parallel", "arbitrary"), vmem),
        name="even_inproj",
    )(h, g, shift, scale, w, wg, bg)


def _conv_kernel(x_ref, w_ref, post_ref, o_ref, *, rows, grid_mode):
    x = x_ref[...].astype(F32)
    t = lax.broadcasted_iota(jnp.int32, x.shape, 0)
    w = w_ref[...]
    if grid_mode:
        col = t % GRID_W
        has_prev, has_next = col > 0, col < GRID_W - 1
    else:
        has_prev, has_next = t > 0, t < rows - 1
    xl = jnp.where(has_prev, pltpu.roll(x, 1, 0), 0.0)
    xr = jnp.where(has_next, pltpu.roll(x, rows - 1, 0), 0.0)

    def tap_row(dr):
        return xl * w[3 * dr:3 * dr + 1] + x * w[3 * dr + 1:3 * dr + 2] + xr * w[3 * dr + 2:3 * dr + 3]

    y = tap_row(1)
    if grid_mode:
        z = jnp.zeros((GRID_W, x.shape[1]), F32)
        y = y + jnp.concatenate([z, tap_row(0)[:rows - GRID_W]], axis=0)
        y = y + jnp.concatenate([tap_row(2)[GRID_W:], z], axis=0)
    y = y * jax.nn.sigmoid(y)
    o_ref[...] = (y * post_ref[...]).astype(o_ref.dtype)


def _conv_qk(p, conv_w, post, dims):
    b, s, c = dims["B"], dims["S"], dims["C"]
    r = p.shape[0]
    nqk = dims["NQK"]
    ch = 2 * nqk
    tc = LANE
    col0 = 6 * nqk // tc
    w9 = conv_w.reshape(CONV_K * CONV_K, ch)

    def call(rows, blk0, grid_mode):
        return pl.pallas_call(
            functools.partial(_conv_kernel, rows=rows, grid_mode=grid_mode),
            grid=(b, ch // tc),
            in_specs=[pl.BlockSpec((rows, tc), lambda i, j: (blk0 + i, col0 + j)),
                      pl.BlockSpec((CONV_K * CONV_K, tc), lambda i, j: (0, j)),
                      pl.BlockSpec((1, tc), lambda i, j: (0, j))],
            out_specs=pl.BlockSpec((rows, tc), lambda i, j: (i, j)),
            out_shape=jax.ShapeDtypeStruct((b * rows, ch), BF16),
            compiler_params=_cparams(("parallel", "parallel"), 16 * rows * tc * 4),
            name="conv_lat" if grid_mode else "conv_ctx",
        )(p, w9, post)

    return jnp.concatenate([call(s, 0, True), call(c, (b * s) // c, False)], axis=0)


def _chunk_maps(dims, chunk, rev):
    b_, s, c = dims["B"], dims["S"], dims["C"]
    ncc, ncl = c // chunk, s // chunk
    nc = ncc + ncl

    def pos(n):
        if not rev:
            return n
        return jnp.where(n < ncc, ncc - 1 - n, ncc + nc - 1 - n)

    def rowblk(b, n):
        q = pos(n)
        return jnp.where(q < ncc, (b_ * s) // chunk + b * ncc + q, b * ncl + q - ncc)

    return nc, pos, rowblk


def _retention_kernel(q_ref, k_ref, v_ref, cos_ref, sin_ref, intra_ref, qd_ref, kd_ref, cd_ref, y_ref, s_scr,
                      *, heads):
    @pl.when(pl.program_id(1) == 0)
    def _():
        s_scr[...] = jnp.zeros_like(s_scr)

    cos = cos_ref[...]
    sin = sin_ref[...]
    lane = lax.broadcasted_iota(jnp.int32, cos.shape, 1)
    first = (lane % (HEAD_DK // 2)) < (HEAD_DK // 4)

    def rope(x):
        swapped = jnp.where(first, pltpu.roll(x, HEAD_DK - HEAD_DK // 4, 1), pltpu.roll(x, HEAD_DK // 4, 1))
        return x * cos + swapped * sin

    for h in range(heads):
        ks = slice(h * HEAD_DK, (h + 1) * HEAD_DK)
        vs = slice(h * HEAD_DV, (h + 1) * HEAD_DV)
        q = rope(q_ref[:, ks].astype(F32))
        k = rope(k_ref[:, ks].astype(F32)) * HEAD_DK ** -0.5
        v = v_ref[:, vs]
        s = s_scr[h]
        sc = _dot_nt(q.astype(BF16), k.astype(BF16)) * intra_ref[h]
        o = _dot(sc.astype(BF16), v) + _dot((q * qd_ref[h]).astype(BF16), s.astype(BF16))
        s_scr[h] = s * cd_ref[h] + _dot_tn((k * kd_ref[h]).astype(BF16), v)
        y_ref[:, vs] = o.astype(y_ref.dtype)


def _retention_tables(heads, rev):
    h = jnp.arange(heads, dtype=F32) / max(heads - 1, 1)
    lg = jnp.log1p(-jnp.exp2(-(5.0 + 7.0 * h)))
    if rev:
        lg = lg[::-1]
    pos = jnp.arange(CHUNK, dtype=F32)
    diff = pos[:, None] - pos[None, :]
    if rev:
        diff = -diff
    intra = jnp.where(diff >= 0, jnp.exp(jnp.maximum(diff, 0.0)[None] * lg[:, None, None]), 0.0)
    q_pow = (CHUNK - pos) if rev else (pos + 1.0)
    k_pow = pos if rev else (CHUNK - 1.0 - pos)
    qd = jnp.exp(q_pow[None, :] * lg[:, None])[..., None]
    kd = jnp.exp(k_pow[None, :] * lg[:, None])[..., None]
    cd = jnp.exp(CHUNK * lg)[:, None, None]
    return intra, qd, kd, cd


def _retention(p, cos, sin, dims, rev):
    heads, nqk, nv = dims["H"], dims["NQK"], dims["NV"]
    r = p.shape[0]
    nc, pos, rowblk = _chunk_maps(dims, CHUNK, rev)
    intra, qd, kd, cd = _retention_tables(heads, rev)
    full = lambda shape: pl.BlockSpec(shape, lambda b, n: (0,) * len(shape))
    vmem = 2 * CHUNK * (2 * nqk + 2 * nv) * 2 + 2 * heads * CHUNK * (CHUNK + 2 * LANE) * 4 + heads * HEAD_DK * HEAD_DV * 4
    return pl.pallas_call(
        functools.partial(_retention_kernel, heads=heads),
        grid=(dims["B"], nc),
        in_specs=[pl.BlockSpec((CHUNK, nqk), lambda b, n: (rowblk(b, n), 0)),
                  pl.BlockSpec((CHUNK, nqk), lambda b, n: (rowblk(b, n), 1)),
                  pl.BlockSpec((CHUNK, nv), lambda b, n: (rowblk(b, n), 1)),
                  pl.BlockSpec((CHUNK, HEAD_DK), lambda b, n: (pos(n), 0)),
                  pl.BlockSpec((CHUNK, HEAD_DK), lambda b, n: (pos(n), 0)),
                  full((heads, CHUNK, CHUNK)), full((heads, CHUNK, 1)), full((heads, CHUNK, 1)),
                  full((heads, 1, 1))],
        out_specs=pl.BlockSpec((CHUNK, nv), lambda b, n: (rowblk(b, n), 0)),
        out_shape=jax.ShapeDtypeStruct((r, nv), BF16),
        scratch_shapes=[pltpu.VMEM((heads, HEAD_DK, HEAD_DV), F32)],
        compiler_params=_cparams(("parallel", "arbitrary"), vmem),
        name="retention_bwd" if rev else "retention_fwd",
    )(p, p, p, cos, sin, intra, qd, kd, cd)


def _rope_tables(dims):
    s, c = dims["S"], dims["C"]
    quarter = HEAD_DK // 4
    inv = jnp.power(ROPE_BASE, -jnp.arange(quarter, dtype=F32) / quarter)
    t = jnp.arange(s)
    rows = (t // GRID_W).astype(F32)
    cols = (t % GRID_W).astype(F32)
    ar = rows[:, None] * inv[None, :]
    ac = cols[:, None] * inv[None, :]
    cos = jnp.concatenate([jnp.cos(ar), jnp.cos(ar), jnp.cos(ac), jnp.cos(ac)], axis=-1)
    sin = jnp.concatenate([-jnp.sin(ar), jnp.sin(ar), -jnp.sin(ac), jnp.sin(ac)], axis=-1)
    cos = jnp.concatenate([jnp.ones((c, HEAD_DK), F32), cos], axis=0)
    sin = jnp.concatenate([jnp.zeros((c, HEAD_DK), F32), sin], axis=0)
    return cos, sin


def _mlstm_kernel(q_ref, k_ref, v_ref, g_ref, y_ref, c_scr, n_scr, m_scr, *, heads, rev):
    @pl.when(pl.program_id(1) == 0)
    def _():
        c_scr[...] = jnp.zeros_like(c_scr)
        n_scr[...] = jnp.zeros_like(n_scr)
        m_scr[...] = jnp.zeros_like(m_scr)

    ti = lax.broadcasted_iota(jnp.int32, (CHUNK, CHUNK), 0)
    si = lax.broadcasted_iota(jnp.int32, (CHUNK, CHUNK), 1)
    causal = (si >= ti) if rev else (si <= ti)
    tri = causal.astype(BF16)
    g = g_ref[...]
    logf = jnp.minimum(g, 0.0) - jnp.log1p(jnp.exp(-jnp.abs(g)))
    cs = _split_dot(tri, logf)
    g_t = g.T
    cs_t = cs.T
    icol0 = 2 * heads if rev else 0
    fcol0 = icol0 + heads
    last = 0 if rev else CHUNK - 1

    for h in range(heads):
        ks = slice(h * HEAD_DK, (h + 1) * HEAD_DK)
        vs = slice(h * HEAD_DV, (h + 1) * HEAD_DV)
        ic, fc = icol0 + h, fcol0 + h
        b_col, b_row = cs[:, fc:fc + 1], cs_t[fc:fc + 1, :]
        i_col, i_row = g[:, ic:ic + 1], g_t[ic:ic + 1, :]
        b_last = cs[last:last + 1, fc:fc + 1]
        m = m_scr[h]
        q = q_ref[:, ks]
        k = k_ref[:, ks]
        v = v_ref[:, vs]
        c_mem = c_scr[h]
        n_mem = n_scr[h]

        d_log = jnp.where(causal, b_col - b_row + i_row, -jnp.inf)
        m_inter = b_col + m
        m_row = jnp.maximum(jnp.max(d_log, axis=-1, keepdims=True), m_inter)
        w_intra = jnp.exp(d_log - m_row)
        w_inter = jnp.exp(m_inter - m_row)
        sc = _dot_nt(q, k) * w_intra
        num = _dot(sc.astype(BF16), v) + w_inter * _dot(q, c_mem.astype(BF16))
        den = jnp.sum(sc, axis=-1, keepdims=True) + w_inter * jnp.sum(q.astype(F32) * n_mem, axis=-1, keepdims=True)
        y_ref[:, vs] = (num / jnp.maximum(jnp.abs(den), jnp.exp(-m_row))).astype(y_ref.dtype)

        gk = b_last - b_col + i_col
        m_new = jnp.maximum(b_last + m, jnp.max(gk, axis=0, keepdims=True))
        w_k = jnp.exp(gk - m_new)
        w_c = jnp.exp(b_last + m - m_new)
        kw = k.astype(F32) * w_k
        c_scr[h] = w_c * c_mem + _dot_tn(kw.astype(BF16), v)
        n_scr[h] = w_c * n_mem + jnp.sum(kw, axis=0, keepdims=True)
        m_scr[h] = m_new


def _mlstm(qk, p, gates, dims, rev):
    heads, nqk, nv = dims["H"], dims["NQK"], dims["NV"]
    r = p.shape[0]
    nc, _, rowblk = _chunk_maps(dims, CHUNK, rev)
    vmem = 2 * CHUNK * (2 * nqk + 2 * nv) * 2 + 2 * CHUNK * LANE * 4 + heads * HEAD_DK * (HEAD_DV + 8) * 4
    return pl.pallas_call(
        functools.partial(_mlstm_kernel, heads=heads, rev=rev),
        grid=(dims["B"], nc),
        in_specs=[pl.BlockSpec((CHUNK, nqk), lambda b, n: (rowblk(b, n), 0)),
                  pl.BlockSpec((CHUNK, nqk), lambda b, n: (rowblk(b, n), 1)),
                  pl.BlockSpec((CHUNK, nv), lambda b, n: (rowblk(b, n), 4)),
                  pl.BlockSpec((CHUNK, LANE), lambda b, n: (rowblk(b, n), 0))],
        out_specs=pl.BlockSpec((CHUNK, nv), lambda b, n: (rowblk(b, n), 0)),
        out_shape=jax.ShapeDtypeStruct((r, nv), BF16),
        scratch_shapes=[pltpu.VMEM((heads, HEAD_DK, HEAD_DV), F32),
                        pltpu.VMEM((heads, 1, HEAD_DK), F32),
                        pltpu.VMEM((heads, 1, 1), F32)],
        compiler_params=_cparams(("parallel", "arbitrary"), vmem),
        name="mlstm_bwd" if rev else "mlstm_fwd",
    )(qk, qk, p, gates)


def _merge_kernel(yrf_ref, yrb_ref, ymf_ref, ymb_ref, ag_ref, bo_ref, gr_ref, gm_ref, z_ref, *, heads):
    nv = heads * HEAD_DV
    for h in range(heads):
        vs = slice(h * HEAD_DV, (h + 1) * HEAD_DV)
        yr = yrf_ref[:, vs].astype(F32) + yrb_ref[:, vs].astype(F32)
        yr = yr * lax.rsqrt(jnp.mean(yr * yr, axis=-1, keepdims=True) + EPS) * gr_ref[:, vs]
        ag = ag_ref[:, vs].astype(F32)
        z_ref[:, vs] = (yr * (ag * jax.nn.sigmoid(ag))).astype(z_ref.dtype)
        ym = ymf_ref[:, vs].astype(F32) + ymb_ref[:, vs].astype(F32)
        ym = ym * lax.rsqrt(jnp.mean(ym * ym, axis=-1, keepdims=True) + EPS) * gm_ref[:, vs]
        bo = bo_ref[:, vs].astype(F32)
        z_ref[:, nv + h * HEAD_DV:nv + (h + 1) * HEAD_DV] = (ym * jax.nn.sigmoid(bo)).astype(z_ref.dtype)


def _merge(yrf, yrb, ymf, ymb, p, gn_ret, gn_ml, dims, rows):
    heads, nv = dims["H"], dims["NV"]
    tm = _pow2_tile(128, dims["S"], dims["RC"])
    ys = pl.BlockSpec((tm, nv), lambda i: (i, 0))
    return pl.pallas_call(
        functools.partial(_merge_kernel, heads=heads),
        grid=(rows // tm,),
        in_specs=[ys, ys, ys, ys,
                  pl.BlockSpec((tm, nv), lambda i: (i, 2)),
                  pl.BlockSpec((tm, nv), lambda i: (i, 5)),
                  pl.BlockSpec((1, nv), lambda i: (0, 0)),
                  pl.BlockSpec((1, nv), lambda i: (0, 0))],
        out_specs=pl.BlockSpec((tm, 2 * nv), lambda i: (i, 0)),
        out_shape=jax.ShapeDtypeStruct((rows, 2 * nv), BF16),
        compiler_params=_cparams(("parallel",), 2 * tm * nv * 2 * 8 + 8 * tm * nv * 4),
        name="even_merge",
    )(yrf, yrb, ymf, ymb, p, p, gn_ret, gn_ml)


def _outproj_kernel(z_ref, w_ref, h_ref, gate_ref, g_ref, o_ref, *, tm, tn):
    j = pl.program_id(1)
    o_ref[:, pl.ds(pl.multiple_of(j * tn, tn), tn)] = _dot(z_ref[...], w_ref[...])

    @pl.when(j == pl.num_programs(1) - 1)
    def _():
        _norm_residual_rows(o_ref, h_ref, gate_ref, g_ref, o_ref, tm)


def _outproj(z, w, h, gate, g, dims, rows):
    k, n = w.shape
    tm = _pow2_tile(512, dims["S"], dims["RC"])
    tn = _pow2_tile(512, n)
    modrow = _mod_row_map(tm, dims["RL"] // tm, dims["S"], dims["B"])
    vmem = 2 * tm * k * 2 + 2 * k * tn * 2 + tm * n * 4 * 3 + 2 * tm * tn * 4
    return pl.pallas_call(
        functools.partial(_outproj_kernel, tm=tm, tn=tn),
        grid=(rows // tm, n // tn),
        in_specs=[pl.BlockSpec((tm, k), lambda i, j: (i, 0)),
                  pl.BlockSpec((k, tn), lambda i, j: (0, j)),
                  pl.BlockSpec((tm, n), lambda i, j: (i, 0), pipeline_mode=pl.Buffered(1)),
                  pl.BlockSpec((1, 1, n), lambda i, j: (modrow(i), 0, 0)),
                  pl.BlockSpec((1, n), lambda i, j: (0, 0))],
        out_specs=pl.BlockSpec((tm, n), lambda i, j: (i, 0)),
        out_shape=jax.ShapeDtypeStruct((rows, n), F32),
        compiler_params=_cparams(("parallel", "arbitrary"), vmem),
        name="outproj_norm_residual",
    )(z, w, h, gate, g)


def _mlp_kernel(h_ref, gpre_ref, sh_ref, sc_ref, w1_ref, w2_ref, gate_ref, gpost_ref, o_ref, u_scr, *, tm):
    f = pl.program_id(1)

    @pl.when(f == 0)
    def _():
        _modulate_rows(h_ref, gpre_ref, sh_ref, sc_ref, u_scr, tm)
        o_ref[...] = jnp.zeros_like(o_ref)

    a = jnp.maximum(_dot(u_scr[...], w1_ref[...]), 0.0)
    o_ref[...] += _dot((a * a).astype(BF16), w2_ref[...])

    @pl.when(f == pl.num_programs(1) - 1)
    def _():
        _norm_residual_rows(o_ref, h_ref, gate_ref, gpost_ref, o_ref, tm)


def _mlp(h, g_pre, shift, scale, w1, w2, gate, g_post, dims, rows):
    d, ff = w1.shape
    tm = _pow2_tile(512, dims["S"], dims["RC"])
    tf = _pow2_tile(512, ff)
    modrow = _mod_row_map(tm, dims["RL"] // tm, dims["S"], dims["B"])
    mod_spec = pl.BlockSpec((1, 1, d), lambda i, f: (modrow(i), 0, 0))
    row_spec = pl.BlockSpec((1, d), lambda i, f: (0, 0))
    vmem = 3 * tm * d * 4 + tm * d * 2 + 4 * d * tf * 2 + 3 * tm * tf * 4
    return pl.pallas_call(
        functools.partial(_mlp_kernel, tm=tm),
        grid=(rows // tm, ff // tf),
        in_specs=[pl.BlockSpec((tm, d), lambda i, f: (i, 0), pipeline_mode=pl.Buffered(1)),
                  row_spec, mod_spec, mod_spec,
                  pl.BlockSpec((d, tf), lambda i, f: (0, f)),
                  pl.BlockSpec((tf, d), lambda i, f: (f, 0)),
                  mod_spec, row_spec],
        out_specs=pl.BlockSpec((tm, d), lambda i, f: (i, 0)),
        out_shape=jax.ShapeDtypeStruct((rows, d), F32),
        scratch_shapes=[pltpu.VMEM((tm, d), BF16)],
        compiler_params=_cparams(("parallel", "arbitrary"), vmem),
        name="mlp_block",
    )(h, g_pre, shift, scale, w1, w2, gate, g_post)


HALO = GRID_W


def _rwkv_prep_kernel(h_ref, ha_ref, hb_ref, g_ref, sh_ref, sc_ref, mu_ref, o_ref, rs_scr, ra_scr, rb_scr,
                      *, tm, tc, n_lat_tiles, s, c):
    i = pl.program_id(0)
    jc = pl.program_id(1)
    is_lat = i < n_lat_tiles
    cs = pl.ds(pl.multiple_of(jc * tc, tc), tc)

    @pl.when(jc == 0)
    def _():
        def rstd(x):
            return lax.rsqrt(jnp.mean(x * x, axis=-1, keepdims=True) + EPS)

        def body(r, carry):
            sl = pl.ds(pl.multiple_of(r * ROW_BLOCK, ROW_BLOCK), ROW_BLOCK)
            rs_scr[sl, :] = rstd(h_ref[sl, :])
            return carry

        lax.fori_loop(0, tm // ROW_BLOCK, body, 0)
        ra_scr[...] = rstd(ha_ref[...])
        rb_scr[...] = rstd(hb_ref[...])

    g = g_ref[...]
    sh = sh_ref[0]
    sc = sc_ref[0]

    def mod(x, rs):
        return (x * rs * g) * (1.0 + sc) + sh

    u = mod(h_ref[:, cs], rs_scr[...])
    t = lax.broadcasted_iota(jnp.int32, u.shape, 0)
    period = jnp.where(is_lat, GRID_W, c)
    tpos = t & (period - 1)

    def emit(shifted):
        xx = shifted - u
        for m in range(6):
            o_ref[m] = (u + xx * mu_ref[m]).astype(o_ref.dtype)

    use_prev = (jc == 0) | (jnp.logical_not(is_lat) & (jc == 1))
    use_next = (is_lat & (jc == 1)) | (jnp.logical_not(is_lat) & (jc >= 2))

    @pl.when(use_prev)
    def _():
        emit(jnp.where(tpos > 0, pltpu.roll(u, 1, 0), 0.0))

    @pl.when(use_next)
    def _():
        emit(jnp.where(tpos < period - 1, pltpu.roll(u, tm - 1, 0), 0.0))

    @pl.when(is_lat & (jc == 2))
    def _():
        first = (i * tm) % s == 0
        above = jnp.where(first, 0.0, mod(ha_ref[:, cs], ra_scr[...]))
        emit(jnp.concatenate([above, u[:tm - HALO]], axis=0))

    @pl.when(is_lat & (jc == 3))
    def _():
        lastt = ((i + 1) * tm) % s == 0
        below = jnp.where(lastt, 0.0, mod(hb_ref[:, cs], rb_scr[...]))
        emit(jnp.concatenate([u[HALO:], below], axis=0))


def _rwkv_prep(h, g, shift, scale, mu6, dims):
    r, d = h.shape
    s, c = dims["S"], dims["C"]
    tm = c
    assert tm % HALO == 0 and s % tm == 0 and tm & (tm - 1) == 0
    tc = d // 4
    hb = tm // HALO
    nblk = r // HALO
    n_lat_tiles = dims["RL"] // tm
    modrow = _mod_row_map(tm, n_lat_tiles, s, dims["B"])
    mod_spec = pl.BlockSpec((1, 1, tc), lambda i, j: (modrow(i), 0, j))
    vmem = 2 * tm * d * 4 + 4 * HALO * d * 4 + 2 * 6 * tm * tc * 2 + 12 * tm * tc * 4
    return pl.pallas_call(
        functools.partial(_rwkv_prep_kernel, tm=tm, tc=tc, n_lat_tiles=n_lat_tiles, s=s, c=c),
        grid=(r // tm, 4),
        in_specs=[pl.BlockSpec((tm, d), lambda i, j: (i, 0)),
                  pl.BlockSpec((HALO, d), lambda i, j: (jnp.maximum(i * hb - 1, 0), 0)),
                  pl.BlockSpec((HALO, d), lambda i, j: (jnp.minimum((i + 1) * hb, nblk - 1), 0)),
                  pl.BlockSpec((1, tc), lambda i, j: (0, j)),
                  mod_spec, mod_spec,
                  pl.BlockSpec((6, 1, tc), lambda i, j: (0, 0, j))],
        out_specs=pl.BlockSpec((6, tm, tc), lambda i, j: (0, i, j)),
        out_shape=jax.ShapeDtypeStruct((6, r, d), BF16),
        scratch_shapes=[pltpu.VMEM((tm, 1), F32), pltpu.VMEM((HALO, 1), F32), pltpu.VMEM((HALO, 1), F32)],
        compiler_params=_cparams(("parallel", "arbitrary"), vmem),
        name="rwkv_prep",
    )(h, h, h, g, shift, scale, mu6)


def _bmm_kernel(x_ref, w_ref, o_ref):
    o_ref[0] = _dot(x_ref[0], w_ref[0]).astype(o_ref.dtype)


def _bmm(x, w, lhs0, out_dtype, dims):
    ng, k, n = w.shape
    r = x.shape[1]
    tm = _pow2_tile(512, dims["S"], dims["RC"])
    tn = _pow2_tile(1024, n)
    osz = jnp.dtype(out_dtype).itemsize
    return pl.pallas_call(
        _bmm_kernel,
        grid=(ng, r // tm, n // tn),
        in_specs=[pl.BlockSpec((1, tm, k), lambda g, i, j: (lhs0 + g, i, 0)),
                  pl.BlockSpec((1, k, tn), lambda g, i, j: (g, 0, j))],
        out_specs=pl.BlockSpec((1, tm, tn), lambda g, i, j: (g, i, j)),
        out_shape=jax.ShapeDtypeStruct((ng, r, n), out_dtype),
        compiler_params=_cparams(("parallel", "parallel", "arbitrary"),
                                 2 * tm * k * 2 + 2 * k * tn * 2 + 2 * tm * tn * osz + tm * tn * 4),
        name="rwkv_bmm",
    )(x, w)


def _pair_ones():
    a = lax.broadcasted_iota(jnp.int32, (LANE, LANE), 0) // RWKV_N
    b = lax.broadcasted_iota(jnp.int32, (LANE, LANE), 1) // RWKV_N
    return (a == b).astype(BF16)


def _head_sums(x, ones):
    parts = [_split_dot_r(x[:, l:l + LANE], ones) for l in range(0, x.shape[1], LANE)]
    return parts[0] if len(parts) == 1 else jnp.concatenate(parts, axis=1)


def _rwkv_mid_kernel(k_ref, hid_ref, w2_ref, a2_ref, g2_ref, w0_ref, a0_ref, kk_w_ref,
                     lw_ref, a_ref, kk_ref, gate_ref, *, lp):
    ones = _pair_ones()
    hw = hid_ref[0]
    ha = hid_ref[1]
    hg = hid_ref[2]
    for d in range(2):
        tw = jnp.tanh(hw[:, d * lp:(d + 1) * lp]).astype(BF16)
        z = w0_ref[d] + _dot(tw, w2_ref[d])
        w_log = jnp.minimum(z, 0.0) - jnp.log1p(jnp.exp(-jnp.abs(z))) - 0.5
        lw_ref[d] = -jnp.exp(w_log)
        za = a0_ref[d] + _dot(ha[:, d * lp:(d + 1) * lp].astype(BF16), a2_ref[d])
        a_ref[d] = jax.nn.sigmoid(za).astype(a_ref.dtype)
    gate_ref[...] = _dot(jax.nn.sigmoid(hg).astype(BF16), g2_ref[...]).astype(gate_ref.dtype)
    kk = k_ref[0].astype(F32) * kk_w_ref[...]
    ss = _head_sums(kk * kk, ones)
    kk_ref[...] = (kk * lax.rsqrt(jnp.maximum(ss, 1e-24))).astype(kk_ref.dtype)


def _rwkv_mid(rkv, hid, w2, a2, g2, w0, a0, k_k, dims):
    _, r, d = rkv.shape
    lp = w2.shape[1]
    lw_tot = hid.shape[2]
    tm = _pow2_tile(256, dims["S"], dims["RC"])
    tc = _pow2_tile(1024, d)
    vmem = 2 * (tm * tc * 2 + 3 * tm * lw_tot * 4 + (4 * lp + g2.shape[0]) * tc * 2
                + tm * tc * (8 + 4 + 2 + 2)) + 10 * tm * tc * 4
    return pl.pallas_call(
        functools.partial(_rwkv_mid_kernel, lp=lp),
        grid=(r // tm, d // tc),
        in_specs=[pl.BlockSpec((1, tm, tc), lambda i, j: (1, i, j)),
                  pl.BlockSpec((3, tm, lw_tot), lambda i, j: (0, i, 0)),
                  pl.BlockSpec((2, lp, tc), lambda i, j: (0, 0, j)),
                  pl.BlockSpec((2, lp, tc), lambda i, j: (0, 0, j)),
                  pl.BlockSpec((g2.shape[0], tc), lambda i, j: (0, j)),
                  pl.BlockSpec((2, 1, tc), lambda i, j: (0, 0, j)),
                  pl.BlockSpec((2, 1, tc), lambda i, j: (0, 0, j)),
                  pl.BlockSpec((1, tc), lambda i, j: (0, j))],
        out_specs=[pl.BlockSpec((2, tm, tc), lambda i, j: (0, i, j)),
                   pl.BlockSpec((2, tm, tc), lambda i, j: (0, i, j)),
                   pl.BlockSpec((tm, tc), lambda i, j: (i, j)),
                   pl.BlockSpec((tm, tc), lambda i, j: (i, j))],
        out_shape=[jax.ShapeDtypeStruct((2, r, d), F32), jax.ShapeDtypeStruct((2, r, d), BF16),
                   jax.ShapeDtypeStruct((r, d), BF16), jax.ShapeDtypeStruct((r, d), BF16)],
        compiler_params=_cparams(("parallel", "arbitrary"), vmem),
        name="rwkv_mid",
    )(rkv, hid, w2, a2, g2, w0, a0, k_k)


def _rwkv_scan_kernel(r_ref, k_ref, v_ref, kk_ref, a_ref, lw_ref, ka_ref, y_ref, s_scr, *, rev, npairs):
    @pl.when(pl.program_id(2) == 0)
    def _():
        s_scr[...] = jnp.zeros_like(s_scr)

    ln = RWKV_CHUNK
    row = lax.broadcasted_iota(jnp.int32, (ln, LANE), 0)
    lane = lax.broadcasted_iota(jnp.int32, (ln, LANE), 1)
    col = lane % RWKV_N
    head0 = lane < RWKV_N
    strict = (col > row) if rev else (col < row)
    incl = (col >= row) if rev else (col <= row)
    eye = (col == row).astype(F32)

    def same_block(size):
        return (row // size) == (col // size)

    leaf = same_block(INV_LEAF)
    offs = [same_block(2 * sz) & jnp.logical_not(same_block(sz))
            for sz in (INV_LEAF << i for i in range(int(math.log2(ln // INV_LEAF))))]
    ti = lax.broadcasted_iota(jnp.int32, (ln, ln), 0)
    si = lax.broadcasted_iota(jnp.int32, (ln, ln), 1)
    tri = ((si >= ti) if rev else (si <= ti)).astype(BF16)
    ra = lax.broadcasted_iota(jnp.int32, (LANE, LANE), 0)
    rb = lax.broadcasted_iota(jnp.int32, (LANE, LANE), 1)
    same_head = (ra // RWKV_N) == (rb // RWKV_N)
    eye_full = ra == rb
    last = 0 if rev else ln - 1

    def bd(x):
        zero = jnp.zeros_like(x)
        return jnp.concatenate([jnp.where(head0, x, zero), jnp.where(head0, zero, x)], axis=0)

    def pm(a, b):
        return _dot(a.astype(BF16), bd(b.astype(BF16)))

    pairs = range(npairs)
    sls = [slice(p * LANE, (p + 1) * LANE) for p in pairs]

    def prepare(sl):
        lw = lw_ref[0, :, sl]
        hi = lw.astype(BF16)
        lo = (lw - hi.astype(F32)).astype(BF16)
        cc = _dot(tri, jnp.concatenate([hi, lo], axis=1))
        c = cc[:, :LANE] + cc[:, LANE:]
        a = a_ref[0, :, sl].astype(F32)
        kk = kk_ref[:, sl].astype(F32)
        winv = jnp.exp(-c)
        kd = k_ref[0, :, sl].astype(F32) * (1.0 + (a - 1.0) * ka_ref[:, sl])
        at = (-(kk * jnp.exp(c - lw))).astype(BF16)
        bt = (kk * a * winv).astype(BF16)
        kt = (kd * winv).astype(BF16)
        rt = (r_ref[0, :, sl].astype(F32) * jnp.exp(c)).astype(BF16)
        ctot = jnp.sum(jnp.where(eye_full, c[last:last + 1, :], 0.0), axis=1, keepdims=True)
        return at, bt, kt, rt, jnp.exp(ctot)

    prep = [prepare(sl) for sl in sls]
    res = [_dot_nt(jnp.concatenate([at, rt], axis=0), jnp.concatenate([bd(bt), bd(kt)], axis=0))
           for at, bt, kt, rt, _ in prep]
    mab = [jnp.where(strict, x[:ln, :LANE], 0.0) for x in res]
    mak = [jnp.where(strict, x[:ln, LANE:], 0.0).astype(BF16) for x in res]
    rbk = [jnp.concatenate([jnp.where(incl, x[ln:, :LANE], 0.0), jnp.where(incl, x[ln:, LANE:], 0.0)],
                           axis=1).astype(BF16) for x in res]

    md = [jnp.where(leaf, m, 0.0) for m in mab]
    sq = [pm(m, m) for m in md]
    tinv = [eye + m + pm(q, eye + m) for m, q in zip(md, sq)]
    for off in offs:
        half = [pm(t, jnp.where(off, m, 0.0)) for t, m in zip(tinv, mab)]
        tinv = [t + pm(hf, t) for t, hf in zip(tinv, half)]

    s0 = [s_scr[p] for p in pairs]
    s0b = [s.astype(BF16) for s in s0]
    vbd = [bd(v_ref[0, :, sl]) for sl in sls]
    x = [_dot(jnp.concatenate([pr[0], mk], axis=1), jnp.concatenate([sb, vb], axis=0))
         for pr, sb, mk, vb in zip(prep, s0b, mak, vbd)]
    ub = [pm(t, xx).astype(BF16) for t, xx in zip(tinv, x)]
    for p in pairs:
        at, bt, kt, rt, wtot = prep[p]
        y = _dot(jnp.concatenate([rt, rbk[p]], axis=1), jnp.concatenate([s0b[p], bd(ub[p]), vbd[p]], axis=0))
        y_ref[:, sls[p]] = y.astype(y_ref.dtype)
        upd = _dot_tn(jnp.concatenate([bt, kt], axis=0), jnp.concatenate([ub[p], v_ref[0, :, sls[p]]], axis=0))
        s_scr[p] = (s0[p] + jnp.where(same_head, upd, 0.0)) * wtot


def _rwkv_scan(rkv, kk, a, lw, k_a, dims, rev):
    _, r, d = rkv.shape
    npairs = min(32, d // LANE)
    pw = npairs * LANE
    ln = RWKV_CHUNK
    nc, _, rowblk = _chunk_maps(dims, ln, rev)
    dsel = 1 if rev else 0

    def spec3(lead):
        return pl.BlockSpec((1, ln, pw), lambda b, g, n: (lead, rowblk(b, n), g))

    return pl.pallas_call(
        functools.partial(_rwkv_scan_kernel, rev=rev, npairs=npairs),
        grid=(dims["B"], d // pw, nc),
        in_specs=[spec3(0), spec3(1), spec3(2),
                  pl.BlockSpec((ln, pw), lambda b, g, n: (rowblk(b, n), g)),
                  spec3(dsel), spec3(dsel),
                  pl.BlockSpec((1, pw), lambda b, g, n: (0, g))],
        out_specs=pl.BlockSpec((ln, pw), lambda b, g, n: (rowblk(b, n), g)),
        out_shape=jax.ShapeDtypeStruct((r, d), BF16),
        scratch_shapes=[pltpu.VMEM((npairs, LANE, LANE), F32)],
        compiler_params=_cparams(("parallel", "parallel", "arbitrary"),
                                 2 * ln * pw * (5 * 2 + 4 + 2) + npairs * LANE * LANE * 4 + (8 << 20)),
        name="rwkv_scan_bwd" if rev else "rwkv_scan_fwd",
    )(rkv, rkv, rkv, kk, a, lw, k_a)


def _rwkv_finish_kernel(yf_ref, yb_ref, r_ref, k_ref, v_ref, a_ref, gate_ref, ka_ref, rk_ref, lw_ref, lb_ref, z_ref):
    ones = _pair_ones()
    inv_n = 1.0 / RWKV_N
    y = yf_ref[...].astype(F32) + yb_ref[...].astype(F32)
    mean = _head_sums(y, ones) * inv_n
    yc = y - mean
    var = _head_sums(yc * yc, ones) * inv_n
    yn = yc * lax.rsqrt(var + LNX_EPS) * lw_ref[...] + lb_ref[...]
    k = k_ref[0].astype(F32)
    ka = ka_ref[...]
    kd0 = k * (1.0 + (a_ref[0].astype(F32) - 1.0) * ka)
    kd1 = k * (1.0 + (a_ref[1].astype(F32) - 1.0) * ka)
    kb = 0.5 * (kd0 + kd1)
    bonus = _head_sums(r_ref[0].astype(F32) * kb * rk_ref[...], ones) * v_ref[0].astype(F32)
    z_ref[...] = ((yn + bonus) * gate_ref[...].astype(F32)).astype(z_ref.dtype)


def _rwkv_finish(yf, yb, rkv, a, gate, k_a, r_k, lnx_w, lnx_b, dims, rows):
    d = yf.shape[1]
    tm = _pow2_tile(256, dims["S"], dims["RC"])
    tc = _pow2_tile(512, d)
    blk = pl.BlockSpec((tm, tc), lambda i, j: (i, j))
    vec = pl.BlockSpec((1, tc), lambda i, j: (0, j))

    def lead(g):
        return pl.BlockSpec((1, tm, tc), lambda i, j: (g, i, j))

    return pl.pallas_call(
        _rwkv_finish_kernel,
        grid=(rows // tm, d // tc),
        in_specs=[blk, blk, lead(0), lead(1), lead(2),
                  pl.BlockSpec((2, tm, tc), lambda i, j: (0, i, j)), blk, vec, vec, vec, vec],
        out_specs=blk,
        out_shape=jax.ShapeDtypeStruct((rows, d), BF16),
        compiler_params=_cparams(("parallel", "parallel"), 2 * tm * tc * 2 * 9 + 16 * tm * tc * 4),
        name="rwkv_finish",
    )(yf, yb, rkv, rkv, rkv, a, gate, k_a, r_k, lnx_w, lnx_b)


def _even_layer(h, mod, norm_g, w_main, w_gate_cols, b_gate, conv_qk, gn_ret, gn_ml, w_out, dims, rows_out):
    d = h.shape[1]
    heads, nqk, nv = dims["H"], dims["NQK"], dims["NV"]
    n_gate = 4 * heads
    w_gate = jnp.pad(w_gate_cols, ((0, 0), (0, LANE - n_gate))).astype(BF16)
    bg = jnp.pad(b_gate, (0, LANE - n_gate)).reshape(1, LANE)
    shift, scale, gate = mod[0], mod[1], mod[2]
    p, gates = _inproj(h, norm_g[0].reshape(1, d), shift, scale, w_main, w_gate, bg, dims)
    post = jnp.concatenate([jnp.ones((nqk,), F32), jnp.full((nqk,), HEAD_DK ** -0.5, F32)]).reshape(1, 2 * nqk)
    qk = _conv_qk(p, conv_qk, post, dims)
    cos, sin = _rope_tables(dims)
    yrf = _retention(p, cos, sin, dims, False)
    yrb = _retention(p, cos, sin, dims, True)
    ymf = _mlstm(qk, p, gates, dims, False)
    ymb = _mlstm(qk, p, gates, dims, True)
    z = _merge(yrf, yrb, ymf, ymb, p, gn_ret.reshape(1, nv), gn_ml.reshape(1, nv), dims, rows_out)
    return _outproj(z, w_out, h, gate, norm_g[1].reshape(1, d), dims, rows_out)


def _rwkv_layer(h, mod, norm_g, mu, w_rkv, w0, w1, w2, a0, a1, a2, g1, g2, k_k, k_a, r_k, lnx_w, lnx_b, w_out,
                dims, rows_out):
    d = h.shape[1]
    shift, scale, gate = mod[0], mod[1], mod[2]
    order = jnp.array([0, 2, 3, 1, 4, 5])
    mu6 = mu[order].reshape(6, 1, d)
    x6 = _rwkv_prep(h, norm_g[0].reshape(1, d), shift, scale, mu6, dims)
    rkv = _bmm(x6, w_rkv, 0, BF16, dims)
    lora = w1.shape[2]
    glora = g1.shape[1]
    lp = _round_up(lora, LANE)
    gp = _round_up(glora, LANE)
    lw_tot = max(2 * lp, gp)

    def two(w):
        w = jnp.pad(w, ((0, 0), (0, 0), (0, lp - lora)))
        w = jnp.concatenate([w[0], w[1]], axis=1)
        return jnp.pad(w, ((0, 0), (0, lw_tot - 2 * lp)))

    w_l1 = jnp.stack([two(w1), two(a1), jnp.pad(g1, ((0, 0), (0, lw_tot - glora)))]).astype(BF16)
    hid = _bmm(x6, w_l1, 3, F32, dims)
    pad_rows = lambda w, n: jnp.pad(w, ((0, 0),) * (w.ndim - 2) + ((0, n - w.shape[-2]), (0, 0))).astype(BF16)
    lw, a, kk, gmul = _rwkv_mid(rkv, hid, pad_rows(w2, lp), pad_rows(a2, lp), pad_rows(g2, lw_tot),
                                w0.reshape(2, 1, d), a0.reshape(2, 1, d), k_k.reshape(1, d), dims)
    ka = k_a.reshape(1, d)
    yf = _rwkv_scan(rkv, kk, a, lw, ka, dims, False)
    yb = _rwkv_scan(rkv, kk, a, lw, ka, dims, True)
    z = _rwkv_finish(yf, yb, rkv, a, gmul, ka, r_k.reshape(1, d), lnx_w.reshape(1, d), lnx_b.reshape(1, d),
                     dims, rows_out)
    return _outproj(z, w_out, h, gate, norm_g[1].reshape(1, d), dims, rows_out)


def kernel(x, c, ctx, c_ctx, ada_w, ada_b, norm_g, mlp_w_in, mlp_w_out, ev_w_in, ev_b_gate, ev_conv_qk,
           ev_gn_ret, ev_gn_mlstm, ev_w_out, od_mu, od_w_rkv, od_w0, od_w1, od_w2, od_a0, od_a1, od_a2,
           od_g1, od_g2, od_k_k, od_k_a, od_r_k, od_lnx_w, od_lnx_b, od_w_out):
    b, s, d = x.shape
    cl = ctx.shape[1]
    depth = ada_w.shape[0]
    heads = d // (2 * HEAD_DV)
    assert b + 1 <= MOD_ROWS and s % CHUNK == 0 and cl % CHUNK == 0 and s % GRID_W == 0
    dims = dict(B=b, S=s, C=cl, RL=b * s, RC=b * cl, H=heads, NQK=heads * HEAD_DK, NV=heads * HEAD_DV)
    rl, r = dims["RL"], dims["RL"] + dims["RC"]

    cond = jnp.concatenate([c, c_ctx[None, :], jnp.zeros((MOD_ROWS - b - 1, d), F32)], axis=0)
    tables = _ada_tables(cond, ada_w, ada_b)
    h = jnp.concatenate([x.reshape(rl, d), ctx.reshape(b * cl, d)], axis=0)

    for layer in range(depth):
        with_ctx = layer < depth - 1
        rows_out = r if with_ctx else rl
        mod = tables[layer].reshape(MOD_ROWS, 6, 1, d).transpose(1, 0, 2, 3)
        g = norm_g[layer]
        if layer % 2 == 0:
            e = layer // 2
            n_main = 4 * dims["NQK"] + 4 * dims["NV"]
            h_mix = _even_layer(h, mod[0:3], g[0:2], _to_bf16(ev_w_in, e, n_main), ev_w_in[e][:, n_main:],
                                ev_b_gate[e], ev_conv_qk[e], ev_gn_ret[e], ev_gn_mlstm[e], _to_bf16(ev_w_out, e),
                                dims, rows_out)
        else:
            o = layer // 2
            h_mix = _rwkv_layer(h, mod[0:3], g[0:2], od_mu[o], _to_bf16(od_w_rkv, o), od_w0[o], od_w1[o], od_w2[o],
                                od_a0[o], od_a1[o], od_a2[o], od_g1[o], od_g2[o], od_k_k[o], od_k_a[o], od_r_k[o],
                                od_lnx_w[o], od_lnx_b[o], _to_bf16(od_w_out, o), dims, rows_out)
        h = _mlp(h_mix, g[2].reshape(1, d), mod[3], mod[4], _to_bf16(mlp_w_in, layer),
                 _to_bf16(mlp_w_out, layer), mod[5], g[3].reshape(1, d), dims, rows_out)
    return h[:rl].reshape(b, s, d)
```

```python
import functools
import math

import jax
import jax.numpy as jnp
from jax import lax
from jax.experimental import pallas as pl
from jax.experimental.pallas import tpu as pltpu

F32 = jnp.float32
BF16 = jnp.bfloat16

EPS = 1e-6
LNX_EPS = 64e-5
ROPE_BASE = 10000.0
GRID_W = 64
CHUNK = 128
RWKV_CHUNK = 64
HEAD_DK = 128
HEAD_DV = 256
RWKV_N = 64
INV_LEAF = 4
CONV_K = 3
LANE = 128
MOD_ROWS = 8
VMEM_BYTES = 64 * 1024 * 1024

NT_DIMS = (((1,), (1,)), ((), ()))
TN_DIMS = (((0,), (0,)), ((), ()))


def _dot(a, b):
    return jnp.dot(a, b, preferred_element_type=F32)


def _dot_nt(a, b):
    return lax.dot_general(a, b, NT_DIMS, preferred_element_type=F32)


def _dot_tn(a, b):
    return lax.dot_general(a, b, TN_DIMS, preferred_element_type=F32)


def _split_dot(m, x):
    hi = x.astype(BF16)
    lo = (x - hi.astype(F32)).astype(BF16)
    return _dot(m, hi) + _dot(m, lo)


def _cparams(semantics, vmem_bytes):
    limit = int(min(max(vmem_bytes * 5 // 4 + (4 << 20), 16 << 20), VMEM_BYTES - (6 << 20)))
    return pltpu.CompilerParams(dimension_semantics=semantics, vmem_limit_bytes=limit)


def _pow2_tile(pref, *dims):
    t = pref
    while any(d % t for d in dims):
        t //= 2
    return t


def _round_up(x, m):
    return (x + m - 1) // m * m


CAST_BLOCK_ELEMS = 1 << 20


def _cast_kernel(x_ref, o_ref):
    o_ref[...] = x_ref[...].astype(o_ref.dtype)


def _to_bf16(w, lead, ncols=None):
    shape = w.shape[1:]
    w3 = w.reshape(w.shape[0], -1, shape[-1])
    rows = w3.shape[1]
    ncols = shape[-1] if ncols is None else ncols
    tc = _pow2_tile(4096, ncols)
    tr = _pow2_tile(max(CAST_BLOCK_ELEMS // tc, 16), rows)
    out = pl.pallas_call(
        _cast_kernel,
        grid=(rows // tr, ncols // tc),
        in_specs=[pl.BlockSpec((None, tr, tc), lambda i, j: (lead, i, j))],
        out_specs=pl.BlockSpec((tr, tc), lambda i, j: (i, j)),
        out_shape=jax.ShapeDtypeStruct((rows, ncols), BF16),
        compiler_params=_cparams(("parallel", "parallel"), 2 * tr * tc * 6),
        name="weight_cast",
    )(w3)
    return out.reshape(shape[:-1] + (ncols,))


def _ada_kernel(c_ref, w_ref, b_ref, o_ref):
    c = c_ref[...]
    s = c * jax.nn.sigmoid(c)
    o_ref[0] = _dot(s.astype(BF16), w_ref[0].astype(BF16)) + b_ref[0]


def _ada_tables(cond, ada_w, ada_b):
    depth, d, n = ada_w.shape
    tn = _pow2_tile(512, n)
    return pl.pallas_call(
        _ada_kernel,
        grid=(depth, n // tn),
        in_specs=[pl.BlockSpec((MOD_ROWS, d), lambda l, j: (0, 0)),
                  pl.BlockSpec((1, d, tn), lambda l, j: (l, 0, j)),
                  pl.BlockSpec((1, 1, tn), lambda l, j: (l, 0, j))],
        out_specs=pl.BlockSpec((1, MOD_ROWS, tn), lambda l, j: (l, 0, j)),
        out_shape=jax.ShapeDtypeStruct((depth, MOD_ROWS, n), F32),
        compiler_params=_cparams(("parallel", "parallel"), 2 * d * tn * 4 + d * tn * 2),
        name="ada_tables",
    )(cond, ada_w, ada_b.reshape(depth, 1, n))


ROW_BLOCK = 32
NORM_COLS = 512


def _col_slices(d):
    step = min(NORM_COLS, d)
    return [slice(c, c + step) for c in range(0, d, step)]


def _row_rstd(x_ref, sl, cols, d):
    acc = None
    for c in cols:
        x = x_ref[sl, c]
        acc = x * x if acc is None else acc + x * x
    return lax.rsqrt(jnp.sum(acc, axis=-1, keepdims=True) / d + EPS)


def _modulate_rows(h_ref, g_ref, sh_ref, sc_ref, u_scr, tm):
    d = h_ref.shape[1]
    cols = _col_slices(d)

    def body(r, carry):
        sl = pl.ds(pl.multiple_of(r * ROW_BLOCK, ROW_BLOCK), ROW_BLOCK)
        rs = _row_rstd(h_ref, sl, cols, d)
        for c in cols:
            u = (h_ref[sl, c] * rs * g_ref[:, c]) * (1.0 + sc_ref[0, :, c]) + sh_ref[0, :, c]
            u_scr[sl, c] = u.astype(u_scr.dtype)
        return carry

    lax.fori_loop(0, tm // ROW_BLOCK, body, 0)


def _norm_residual_rows(acc_ref, h_ref, gate_ref, g_ref, o_ref, tm):
    d = acc_ref.shape[1]
    cols = _col_slices(d)

    def body(r, carry):
        sl = pl.ds(pl.multiple_of(r * ROW_BLOCK, ROW_BLOCK), ROW_BLOCK)
        rs = _row_rstd(acc_ref, sl, cols, d)
        for c in cols:
            o_ref[sl, c] = h_ref[sl, c] + gate_ref[0, :, c] * (acc_ref[sl, c] * rs * g_ref[:, c])
        return carry

    lax.fori_loop(0, tm // ROW_BLOCK, body, 0)


def _mod_row_map(tm, n_lat_tiles, s, b):
    def f(i):
        return jnp.where(i < n_lat_tiles, (i * tm) // s, b)
    return f


def _inproj_kernel(h_ref, g_ref, sh_ref, sc_ref, w_ref, wg_ref, bg_ref, p_ref, gt_ref, u_scr, *, tm):
    @pl.when(pl.program_id(1) == 0)
    def _():
        _modulate_rows(h_ref, g_ref, sh_ref, sc_ref, u_scr, tm)
        gt_ref[...] = _dot(u_scr[...], wg_ref[...]) + bg_ref[...]

    p_ref[...] = _dot(u_scr[...], w_ref[...]).astype(p_ref.dtype)


def _inproj(h, g, shift, scale, w, wg, bg, dims):
    r, d = h.shape
    n = w.shape[1]
    tm = _pow2_tile(512, dims["S"], dims["RC"])
    tn = _pow2_tile(1024, n)
    modrow = _mod_row_map(tm, dims["RL"] // tm, dims["S"], dims["B"])
    vmem = 2 * tm * d * 4 + tm * d * 2 + 2 * d * tn * 2 + 2 * tm * tn * 2 + 2 * d * LANE * 2 + 2 * tm * LANE * 4
    return pl.pallas_call(
        functools.partial(_inproj_kernel, tm=tm),
        grid=(r // tm, n // tn),
        in_specs=[pl.BlockSpec((tm, d), lambda i, j: (i, 0)),
                  pl.BlockSpec((1, d), lambda i, j: (0, 0)),
                  pl.BlockSpec((1, 1, d), lambda i, j: (modrow(i), 0, 0)),
                  pl.BlockSpec((1, 1, d), lambda i, j: (modrow(i), 0, 0)),
                  pl.BlockSpec((d, tn), lambda i, j: (0, j)),
                  pl.BlockSpec((d, LANE), lambda i, j: (0, 0)),
                  pl.BlockSpec((1, LANE), lambda i, j: (0, 0))],
        out_specs=[pl.BlockSpec((tm, tn), lambda i, j: (i, j)),
                   pl.BlockSpec((tm, LANE), lambda i, j: (i, 0))],
        out_shape=[jax.ShapeDtypeStruct((r, n), BF16), jax.ShapeDtypeStruct((r, LANE), F32)],
        scratch_shapes=[pltpu.VMEM((tm, d), BF16)],
        compiler_params=_cparams(("parallel", "arbitrary"), vmem),
        name="even_inproj",
    )(h, g, shift, scale, w, wg, bg)


def _conv_kernel(x_ref, w_ref, post_ref, o_ref, *, rows, grid_mode):
    x = x_ref[...].astype(F32)
    t = lax.broadcasted_iota(jnp.int32, x.shape, 0)
    w = w_ref[...]
    if grid_mode:
        col = t % GRID_W
        has_prev, has_next = col > 0, col < GRID_W - 1
    else:
        has_prev, has_next = t > 0, t < rows - 1
    xl = jnp.where(has_prev, pltpu.roll(x, 1, 0), 0.0)
    xr = jnp.where(has_next, pltpu.roll(x, rows - 1, 0), 0.0)

    def tap_row(dr):
        return xl * w[3 * dr:3 * dr + 1] + x * w[3 * dr + 1:3 * dr + 2] + xr * w[3 * dr + 2:3 * dr + 3]

    y = tap_row(1)
    if grid_mode:
        z = jnp.zeros((GRID_W, x.shape[1]), F32)
        y = y + jnp.concatenate([z, tap_row(0)[:rows - GRID_W]], axis=0)
        y = y + jnp.concatenate([tap_row(2)[GRID_W:], z], axis=0)
    y = y * jax.nn.sigmoid(y)
    o_ref[...] = (y * post_ref[...]).astype(o_ref.dtype)


def _conv_qk(p, conv_w, post, dims):
    b, s, c = dims["B"], dims["S"], dims["C"]
    r = p.shape[0]
    nqk = dims["NQK"]
    ch = 2 * nqk
    tc = LANE
    col0 = 6 * nqk // tc
    w9 = conv_w.reshape(CONV_K * CONV_K, ch)

    def call(rows, blk0, grid_mode):
        return pl.pallas_call(
            functools.partial(_conv_kernel, rows=rows, grid_mode=grid_mode),
            grid=(b, ch // tc),
            in_specs=[pl.BlockSpec((rows, tc), lambda i, j: (blk0 + i, col0 + j)),
                      pl.BlockSpec((CONV_K * CONV_K, tc), lambda i, j: (0, j)),
                      pl.BlockSpec((1, tc), lambda i, j: (0, j))],
            out_specs=pl.BlockSpec((rows, tc), lambda i, j: (i, j)),
            out_shape=jax.ShapeDtypeStruct((b * rows, ch), BF16),
            compiler_params=_cparams(("parallel", "parallel"), 16 * rows * tc * 4),
            name="conv_lat" if grid_mode else "conv_ctx",
        )(p, w9, post)

    return jnp.concatenate([call(s, 0, True), call(c, (b * s) // c, False)], axis=0)


def _chunk_maps(dims, chunk, rev):
    b_, s, c = dims["B"], dims["S"], dims["C"]
    ncc, ncl = c // chunk, s // chunk
    nc = ncc + ncl

    def pos(n):
        if not rev:
            return n
        return jnp.where(n < ncc, ncc - 1 - n, ncc + nc - 1 - n)

    def rowblk(b, n):
        q = pos(n)
        return jnp.where(q < ncc, (b_ * s) // chunk + b * ncc + q, b * ncl + q - ncc)

    return nc, pos, rowblk


def _retention_kernel(q_ref, k_ref, v_ref, cos_ref, sin_ref, intra_ref, qd_ref, kd_ref, cd_ref, y_ref, s_scr,
                      *, heads):
    @pl.when(pl.program_id(1) == 0)
    def _():
        s_scr[...] = jnp.zeros_like(s_scr)

    cos = cos_ref[...]
    sin = sin_ref[...]
    lane = lax.broadcasted_iota(jnp.int32, cos.shape, 1)
    first = (lane % (HEAD_DK // 2)) < (HEAD_DK // 4)

    def rope(x):
        swapped = jnp.where(first, pltpu.roll(x, HEAD_DK - HEAD_DK // 4, 1), pltpu.roll(x, HEAD_DK // 4, 1))
        return x * cos + swapped * sin

    for h in range(heads):
        ks = slice(h * HEAD_DK, (h + 1) * HEAD_DK)
        vs = slice(h * HEAD_DV, (h + 1) * HEAD_DV)
        q = rope(q_ref[:, ks].astype(F32))
        k = rope(k_ref[:, ks].astype(F32)) * HEAD_DK ** -0.5
        v = v_ref[:, vs]
        s = s_scr[h]
        sc = _dot_nt(q.astype(BF16), k.astype(BF16)) * intra_ref[h]
        o = _dot(sc.astype(BF16), v) + _dot((q * qd_ref[h]).astype(BF16), s.astype(BF16))
        s_scr[h] = s * cd_ref[h] + _dot_tn((k * kd_ref[h]).astype(BF16), v)
        y_ref[:, vs] = o.astype(y_ref.dtype)


def _retention_tables(heads, rev):
    h = jnp.arange(heads, dtype=F32) / max(heads - 1, 1)
    lg = jnp.log1p(-jnp.exp2(-(5.0 + 7.0 * h)))
    if rev:
        lg = lg[::-1]
    pos = jnp.arange(CHUNK, dtype=F32)
    diff = pos[:, None] - pos[None, :]
    if rev:
        diff = -diff
    intra = jnp.where(diff >= 0, jnp.exp(jnp.maximum(diff, 0.0)[None] * lg[:, None, None]), 0.0)
    q_pow = (CHUNK - pos) if rev else (pos + 1.0)
    k_pow = pos if rev else (CHUNK - 1.0 - pos)
    qd = jnp.exp(q_pow[None, :] * lg[:, None])[..., None]
    kd = jnp.exp(k_pow[None, :] * lg[:, None])[..., None]
    cd = jnp.exp(CHUNK * lg)[:, None, None]
    return intra, qd, kd, cd


def _retention(p, cos, sin, dims, rev):
    heads, nqk, nv = dims["H"], dims["NQK"], dims["NV"]
    r = p.shape[0]
    nc, pos, rowblk = _chunk_maps(dims, CHUNK, rev)
    intra, qd, kd, cd = _retention_tables(heads, rev)
    full = lambda shape: pl.BlockSpec(shape, lambda b, n: (0,) * len(shape))
    vmem = 2 * CHUNK * (2 * nqk + 2 * nv) * 2 + 2 * heads * CHUNK * (CHUNK + 2 * LANE) * 4 + heads * HEAD_DK * HEAD_DV * 4
    return pl.pallas_call(
        functools.partial(_retention_kernel, heads=heads),
        grid=(dims["B"], nc),
        in_specs=[pl.BlockSpec((CHUNK, nqk), lambda b, n: (rowblk(b, n), 0)),
                  pl.BlockSpec((CHUNK, nqk), lambda b, n: (rowblk(b, n), 1)),
                  pl.BlockSpec((CHUNK, nv), lambda b, n: (rowblk(b, n), 1)),
                  pl.BlockSpec((CHUNK, HEAD_DK), lambda b, n: (pos(n), 0)),
                  pl.BlockSpec((CHUNK, HEAD_DK), lambda b, n: (pos(n), 0)),
                  full((heads, CHUNK, CHUNK)), full((heads, CHUNK, 1)), full((heads, CHUNK, 1)),
                  full((heads, 1, 1))],
        out_specs=pl.BlockSpec((CHUNK, nv), lambda b, n: (rowblk(b, n), 0)),
        out_shape=jax.ShapeDtypeStruct((r, nv), BF16),
        scratch_shapes=[pltpu.VMEM((heads, HEAD_DK, HEAD_DV), F32)],
        compiler_params=_cparams(("parallel", "arbitrary"), vmem),
        name="retention_bwd" if rev else "retention_fwd",
    )(p, p, p, cos, sin, intra, qd, kd, cd)


def _rope_tables(dims):
    s, c = dims["S"], dims["C"]
    quarter = HEAD_DK // 4
    inv = jnp.power(ROPE_BASE, -jnp.arange(quarter, dtype=F32) / quarter)
    t = jnp.arange(s)
    rows = (t // GRID_W).astype(F32)
    cols = (t % GRID_W).astype(F32)
    ar = rows[:, None] * inv[None, :]
    ac = cols[:, None] * inv[None, :]
    cos = jnp.concatenate([jnp.cos(ar), jnp.cos(ar), jnp.cos(ac), jnp.cos(ac)], axis=-1)
    sin = jnp.concatenate([-jnp.sin(ar), jnp.sin(ar), -jnp.sin(ac), jnp.sin(ac)], axis=-1)
    cos = jnp.concatenate([jnp.ones((c, HEAD_DK), F32), cos], axis=0)
    sin = jnp.concatenate([jnp.zeros((c, HEAD_DK), F32), sin], axis=0)
    return cos, sin


def _mlstm_kernel(q_ref, k_ref, v_ref, g_ref, y_ref, c_scr, n_scr, m_scr, *, heads, rev):
    @pl.when(pl.program_id(1) == 0)
    def _():
        c_scr[...] = jnp.zeros_like(c_scr)
        n_scr[...] = jnp.zeros_like(n_scr)
        m_scr[...] = jnp.zeros_like(m_scr)

    ti = lax.broadcasted_iota(jnp.int32, (CHUNK, CHUNK), 0)
    si = lax.broadcasted_iota(jnp.int32, (CHUNK, CHUNK), 1)
    causal = (si >= ti) if rev else (si <= ti)
    tri = causal.astype(BF16)
    g = g_ref[...]
    logf = jnp.minimum(g, 0.0) - jnp.log1p(jnp.exp(-jnp.abs(g)))
    cs = _split_dot(tri, logf)
    g_t = g.T
    cs_t = cs.T
    icol0 = 2 * heads if rev else 0
    fcol0 = icol0 + heads
    last = 0 if rev else CHUNK - 1

    for h in range(heads):
        ks = slice(h * HEAD_DK, (h + 1) * HEAD_DK)
        vs = slice(h * HEAD_DV, (h + 1) * HEAD_DV)
        ic, fc = icol0 + h, fcol0 + h
        b_col, b_row = cs[:, fc:fc + 1], cs_t[fc:fc + 1, :]
        i_col, i_row = g[:, ic:ic + 1], g_t[ic:ic + 1, :]
        b_last = cs[last:last + 1, fc:fc + 1]
        m = m_scr[h]
        q = q_ref[:, ks]
        k = k_ref[:, ks]
        v = v_ref[:, vs]
        c_mem = c_scr[h]
        n_mem = n_scr[h]

        d_log = jnp.where(causal, b_col - b_row + i_row, -jnp.inf)
        m_inter = b_col + m
        m_row = jnp.maximum(jnp.max(d_log, axis=-1, keepdims=True), m_inter)
        w_intra = jnp.exp(d_log - m_row)
        w_inter = jnp.exp(m_inter - m_row)
        sc = _dot_nt(q, k) * w_intra
        num = _dot(sc.astype(BF16), v) + w_inter * _dot(q, c_mem.astype(BF16))
        den = jnp.sum(sc, axis=-1, keepdims=True) + w_inter * jnp.sum(q.astype(F32) * n_mem, axis=-1, keepdims=True)
        y_ref[:, vs] = (num / jnp.maximum(jnp.abs(den), jnp.exp(-m_row))).astype(y_ref.dtype)

        gk = b_last - b_col + i_col
        m_new = jnp.maximum(b_last + m, jnp.max(gk, axis=0, keepdims=True))
        w_k = jnp.exp(gk - m_new)
        w_c = jnp.exp(b_last + m - m_new)
        kw = k.astype(F32) * w_k
        c_scr[h] = w_c * c_mem + _dot_tn(kw.astype(BF16), v)
        n_scr[h] = w_c * n_mem + jnp.sum(kw, axis=0, keepdims=True)
        m_scr[h] = m_new


def _mlstm(qk, p, gates, dims, rev):
    heads, nqk, nv = dims["H"], dims["NQK"], dims["NV"]
    r = p.shape[0]
    nc, _, rowblk = _chunk_maps(dims, CHUNK, rev)
    vmem = 2 * CHUNK * (2 * nqk + 2 * nv) * 2 + 2 * CHUNK * LANE * 4 + heads * HEAD_DK * (HEAD_DV + 8) * 4
    return pl.pallas_call(
        functools.partial(_mlstm_kernel, heads=heads, rev=rev),
        grid=(dims["B"], nc),
        in_specs=[pl.BlockSpec((CHUNK, nqk), lambda b, n: (rowblk(b, n), 0)),
                  pl.BlockSpec((CHUNK, nqk), lambda b, n: (rowblk(b, n), 1)),
                  pl.BlockSpec((CHUNK, nv), lambda b, n: (rowblk(b, n), 4)),
                  pl.BlockSpec((CHUNK, LANE), lambda b, n: (rowblk(b, n), 0))],
        out_specs=pl.BlockSpec((CHUNK, nv), lambda b, n: (rowblk(b, n), 0)),
        out_shape=jax.ShapeDtypeStruct((r, nv), BF16),
        scratch_shapes=[pltpu.VMEM((heads, HEAD_DK, HEAD_DV), F32),
                        pltpu.VMEM((heads, 1, HEAD_DK), F32),
                        pltpu.VMEM((heads, 1, 1), F32)],
        compiler_params=_cparams(("parallel", "arbitrary"), vmem),
        name="mlstm_bwd" if rev else "mlstm_fwd",
    )(qk, qk, p, gates)


def _merge_kernel(yrf_ref, yrb_ref, ymf_ref, ymb_ref, ag_ref, bo_ref, gr_ref, gm_ref, z_ref, *, heads):
    nv = heads * HEAD_DV
    for h in range(heads):
        vs = slice(h * HEAD_DV, (h + 1) * HEAD_DV)
        yr = yrf_ref[:, vs].astype(F32) + yrb_ref[:, vs].astype(F32)
        yr = yr * lax.rsqrt(jnp.mean(yr * yr, axis=-1, keepdims=True) + EPS) * gr_ref[:, vs]
        ag = ag_ref[:, vs].astype(F32)
        z_ref[:, vs] = (yr * (ag * jax.nn.sigmoid(ag))).astype(z_ref.dtype)
        ym = ymf_ref[:, vs].astype(F32) + ymb_ref[:, vs].astype(F32)
        ym = ym * lax.rsqrt(jnp.mean(ym * ym, axis=-1, keepdims=True) + EPS) * gm_ref[:, vs]
        bo = bo_ref[:, vs].astype(F32)
        z_ref[:, nv + h * HEAD_DV:nv + (h + 1) * HEAD_DV] = (ym * jax.nn.sigmoid(bo)).astype(z_ref.dtype)


def _merge(yrf, yrb, ymf, ymb, p, gn_ret, gn_ml, dims, rows):
    heads, nv = dims["H"], dims["NV"]
    tm = _pow2_tile(128, dims["S"], dims["RC"])
    ys = pl.BlockSpec((tm, nv), lambda i: (i, 0))
    return pl.pallas_call(
        functools.partial(_merge_kernel, heads=heads),
        grid=(rows // tm,),
        in_specs=[ys, ys, ys, ys,
                  pl.BlockSpec((tm, nv), lambda i: (i, 2)),
                  pl.BlockSpec((tm, nv), lambda i: (i, 5)),
                  pl.BlockSpec((1, nv), lambda i: (0, 0)),
                  pl.BlockSpec((1, nv), lambda i: (0, 0))],
        out_specs=pl.BlockSpec((tm, 2 * nv), lambda i: (i, 0)),
        out_shape=jax.ShapeDtypeStruct((rows, 2 * nv), BF16),
        compiler_params=_cparams(("parallel",), 2 * tm * nv * 2 * 8 + 8 * tm * nv * 4),
        name="even_merge",
    )(yrf, yrb, ymf, ymb, p, p, gn_ret, gn_ml)


def _outproj_kernel(z_ref, w_ref, h_ref, gate_ref, g_ref, o_ref, *, tm, tn):
    j = pl.program_id(1)
    o_ref[:, pl.ds(pl.multiple_of(j * tn, tn), tn)] = _dot(z_ref[...], w_ref[...])

    @pl.when(j == pl.num_programs(1) - 1)
    def _():
        _norm_residual_rows(o_ref, h_ref, gate_ref, g_ref, o_ref, tm)


def _outproj(z, w, h, gate, g, dims, rows):
    k, n = w.shape
    tm = _pow2_tile(512, dims["S"], dims["RC"])
    tn = _pow2_tile(512, n)
    modrow = _mod_row_map(tm, dims["RL"] // tm, dims["S"], dims["B"])
    vmem = 2 * tm * k * 2 + 2 * k * tn * 2 + tm * n * 4 * 3 + 2 * tm * tn * 4
    return pl.pallas_call(
        functools.partial(_outproj_kernel, tm=tm, tn=tn),
        grid=(rows // tm, n // tn),
        in_specs=[pl.BlockSpec((tm, k), lambda i, j: (i, 0)),
                  pl.BlockSpec((k, tn), lambda i, j: (0, j)),
                  pl.BlockSpec((tm, n), lambda i, j: (i, 0), pipeline_mode=pl.Buffered(1)),
                  pl.BlockSpec((1, 1, n), lambda i, j: (modrow(i), 0, 0)),
                  pl.BlockSpec((1, n), lambda i, j: (0, 0))],
        out_specs=pl.BlockSpec((tm, n), lambda i, j: (i, 0)),
        out_shape=jax.ShapeDtypeStruct((rows, n), F32),
        compiler_params=_cparams(("parallel", "arbitrary"), vmem),
        name="outproj_norm_residual",
    )(z, w, h, gate, g)


def _mlp_kernel(h_ref, gpre_ref, sh_ref, sc_ref, w1_ref, w2_ref, gate_ref, gpost_ref, o_ref, u_scr, *, tm):
    f = pl.program_id(1)

    @pl.when(f == 0)
    def _():
        _modulate_rows(h_ref, gpre_ref, sh_ref, sc_ref, u_scr, tm)
        o_ref[...] = jnp.zeros_like(o_ref)

    a = jnp.maximum(_dot(u_scr[...], w1_ref[...]), 0.0)
    o_ref[...] += _dot((a * a).astype(BF16), w2_ref[...])

    @pl.when(f == pl.num_programs(1) - 1)
    def _():
        _norm_residual_rows(o_ref, h_ref, gate_ref, gpost_ref, o_ref, tm)


def _mlp(h, g_pre, shift, scale, w1, w2, gate, g_post, dims, rows):
    d, ff = w1.shape
    tm = _pow2_tile(512, dims["S"], dims["RC"])
    tf = _pow2_tile(512, ff)
    modrow = _mod_row_map(tm, dims["RL"] // tm, dims["S"], dims["B"])
    mod_spec = pl.BlockSpec((1, 1, d), lambda i, f: (modrow(i), 0, 0))
    row_spec = pl.BlockSpec((1, d), lambda i, f: (0, 0))
    vmem = 3 * tm * d * 4 + tm * d * 2 + 4 * d * tf * 2 + 3 * tm * tf * 4
    return pl.pallas_call(
        functools.partial(_mlp_kernel, tm=tm),
        grid=(rows // tm, ff // tf),
        in_specs=[pl.BlockSpec((tm, d), lambda i, f: (i, 0), pipeline_mode=pl.Buffered(1)),
                  row_spec, mod_spec, mod_spec,
                  pl.BlockSpec((d, tf), lambda i, f: (0, f)),
                  pl.BlockSpec((tf, d), lambda i, f: (f, 0)),
                  mod_spec, row_spec],
        out_specs=pl.BlockSpec((tm, d), lambda i, f: (i, 0)),
        out_shape=jax.ShapeDtypeStruct((rows, d), F32),
        scratch_shapes=[pltpu.VMEM((tm, d), BF16)],
        compiler_params=_cparams(("parallel", "arbitrary"), vmem),
        name="mlp_block",
    )(h, g_pre, shift, scale, w1, w2, gate, g_post)


HALO = GRID_W


def _rwkv_prep_kernel(h_ref, ha_ref, hb_ref, g_ref, sh_ref, sc_ref, mu_ref, o_ref, rs_scr, ra_scr, rb_scr,
                      *, tm, tc, n_lat_tiles, s, c):
    i = pl.program_id(0)
    jc = pl.program_id(1)
    is_lat = i < n_lat_tiles
    cs = pl.ds(pl.multiple_of(jc * tc, tc), tc)

    @pl.when(jc == 0)
    def _():
        def rstd(x):
            return lax.rsqrt(jnp.mean(x * x, axis=-1, keepdims=True) + EPS)

        def body(r, carry):
            sl = pl.ds(pl.multiple_of(r * ROW_BLOCK, ROW_BLOCK), ROW_BLOCK)
            rs_scr[sl, :] = rstd(h_ref[sl, :])
            return carry

        lax.fori_loop(0, tm // ROW_BLOCK, body, 0)
        ra_scr[...] = rstd(ha_ref[...])
        rb_scr[...] = rstd(hb_ref[...])

    g = g_ref[...]
    sh = sh_ref[0]
    sc = sc_ref[0]

    def mod(x, rs):
        return (x * rs * g) * (1.0 + sc) + sh

    u = mod(h_ref[:, cs], rs_scr[...])
    t = lax.broadcasted_iota(jnp.int32, u.shape, 0)
    period = jnp.where(is_lat, GRID_W, c)
    tpos = t & (period - 1)

    def emit(shifted):
        xx = shifted - u
        for m in range(6):
            o_ref[m] = (u + xx * mu_ref[m]).astype(o_ref.dtype)

    use_prev = (jc == 0) | (jnp.logical_not(is_lat) & (jc == 1))
    use_next = (is_lat & (jc == 1)) | (jnp.logical_not(is_lat) & (jc >= 2))

    @pl.when(use_prev)
    def _():
        emit(jnp.where(tpos > 0, pltpu.roll(u, 1, 0), 0.0))

    @pl.when(use_next)
    def _():
        emit(jnp.where(tpos < period - 1, pltpu.roll(u, tm - 1, 0), 0.0))

    @pl.when(is_lat & (jc == 2))
    def _():
        first = (i * tm) % s == 0
        above = jnp.where(first, 0.0, mod(ha_ref[:, cs], ra_scr[...]))
        emit(jnp.concatenate([above, u[:tm - HALO]], axis=0))

    @pl.when(is_lat & (jc == 3))
    def _():
        lastt = ((i + 1) * tm) % s == 0
        below = jnp.where(lastt, 0.0, mod(hb_ref[:, cs], rb_scr[...]))
        emit(jnp.concatenate([u[HALO:], below], axis=0))


def _rwkv_prep(h, g, shift, scale, mu6, dims):
    r, d = h.shape
    s, c = dims["S"], dims["C"]
    tm = c
    assert tm % HALO == 0 and s % tm == 0 and tm & (tm - 1) == 0
    tc = d // 4
    hb = tm // HALO
    nblk = r // HALO
    n_lat_tiles = dims["RL"] // tm
    modrow = _mod_row_map(tm, n_lat_tiles, s, dims["B"])
    mod_spec = pl.BlockSpec((1, 1, tc), lambda i, j: (modrow(i), 0, j))
    vmem = 2 * tm * d * 4 + 4 * HALO * d * 4 + 2 * 6 * tm * tc * 2 + 12 * tm * tc * 4
    return pl.pallas_call(
        functools.partial(_rwkv_prep_kernel, tm=tm, tc=tc, n_lat_tiles=n_lat_tiles, s=s, c=c),
        grid=(r // tm, 4),
        in_specs=[pl.BlockSpec((tm, d), lambda i, j: (i, 0)),
                  pl.BlockSpec((HALO, d), lambda i, j: (jnp.maximum(i * hb - 1, 0), 0)),
                  pl.BlockSpec((HALO, d), lambda i, j: (jnp.minimum((i + 1) * hb, nblk - 1), 0)),
                  pl.BlockSpec((1, tc), lambda i, j: (0, j)),
                  mod_spec, mod_spec,
                  pl.BlockSpec((6, 1, tc), lambda i, j: (0, 0, j))],
        out_specs=pl.BlockSpec((6, tm, tc), lambda i, j: (0, i, j)),
        out_shape=jax.ShapeDtypeStruct((6, r, d), BF16),
        scratch_shapes=[pltpu.VMEM((tm, 1), F32), pltpu.VMEM((HALO, 1), F32), pltpu.VMEM((HALO, 1), F32)],
        compiler_params=_cparams(("parallel", "arbitrary"), vmem),
        name="rwkv_prep",
    )(h, h, h, g, shift, scale, mu6)


def _bmm_kernel(x_ref, w_ref, o_ref):
    o_ref[0] = _dot(x_ref[0], w_ref[0]).astype(o_ref.dtype)


def _bmm(x, w, lhs0, out_dtype, dims):
    ng, k, n = w.shape
    r = x.shape[1]
    tm = _pow2_tile(512, dims["S"], dims["RC"])
    tn = _pow2_tile(1024, n)
    osz = jnp.dtype(out_dtype).itemsize
    return pl.pallas_call(
        _bmm_kernel,
        grid=(ng, r // tm, n // tn),
        in_specs=[pl.BlockSpec((1, tm, k), lambda g, i, j: (lhs0 + g, i, 0)),
                  pl.BlockSpec((1, k, tn), lambda g, i, j: (g, 0, j))],
        out_specs=pl.BlockSpec((1, tm, tn), lambda g, i, j: (g, i, j)),
        out_shape=jax.ShapeDtypeStruct((ng, r, n), out_dtype),
        compiler_params=_cparams(("parallel", "parallel", "arbitrary"),
                                 2 * tm * k * 2 + 2 * k * tn * 2 + 2 * tm * tn * osz + tm * tn * 4),
        name="rwkv_bmm",
    )(x, w)


def _pair_ones():
    a = lax.broadcasted_iota(jnp.int32, (LANE, LANE), 0) // RWKV_N
    b = lax.broadcasted_iota(jnp.int32, (LANE, LANE), 1) // RWKV_N
    return (a == b).astype(BF16)


def _head_sums(x, ones):
    parts = [_dot(x[:, l:l + LANE].astype(BF16), ones) for l in range(0, x.shape[1], LANE)]
    return parts[0] if len(parts) == 1 else jnp.concatenate(parts, axis=1)


def _rwkv_mid_kernel(k_ref, hid_ref, w2_ref, a2_ref, g2_ref, w0_ref, a0_ref, kk_w_ref,
                     lw_ref, a_ref, kk_ref, gate_ref, *, lp):
    ones = _pair_ones()
    hw = hid_ref[0]
    ha = hid_ref[1]
    hg = hid_ref[2]
    for d in range(2):
        tw = jnp.tanh(hw[:, d * lp:(d + 1) * lp]).astype(BF16)
        z = w0_ref[d] + _dot(tw, w2_ref[d])
        lw_ref[d] = -math.exp(-0.5) * jax.nn.sigmoid(z)
        za = a0_ref[d] + _dot(ha[:, d * lp:(d + 1) * lp].astype(BF16), a2_ref[d])
        a_ref[d] = jax.nn.sigmoid(za).astype(a_ref.dtype)
    gate_ref[...] = _dot(jax.nn.sigmoid(hg).astype(BF16), g2_ref[...]).astype(gate_ref.dtype)
    kk = k_ref[0].astype(F32) * kk_w_ref[...]
    ss = _head_sums(kk * kk, ones)
    kk_ref[...] = (kk * lax.rsqrt(jnp.maximum(ss, 1e-24))).astype(kk_ref.dtype)


def _rwkv_mid(rkv, hid, w2, a2, g2, w0, a0, k_k, dims):
    _, r, d = rkv.shape
    lp = w2.shape[1]
    lw_tot = hid.shape[2]
    tm = _pow2_tile(256, dims["S"], dims["RC"])
    tc = _pow2_tile(1024, d)
    vmem = 2 * (tm * tc * 2 + 3 * tm * lw_tot * 4 + (4 * lp + g2.shape[0]) * tc * 2
                + tm * tc * (8 + 4 + 2 + 2)) + 10 * tm * tc * 4
    return pl.pallas_call(
        functools.partial(_rwkv_mid_kernel, lp=lp),
        grid=(r // tm, d // tc),
        in_specs=[pl.BlockSpec((1, tm, tc), lambda i, j: (1, i, j)),
                  pl.BlockSpec((3, tm, lw_tot), lambda i, j: (0, i, 0)),
                  pl.BlockSpec((2, lp, tc), lambda i, j: (0, 0, j)),
                  pl.BlockSpec((2, lp, tc), lambda i, j: (0, 0, j)),
                  pl.BlockSpec((g2.shape[0], tc), lambda i, j: (0, j)),
                  pl.BlockSpec((2, 1, tc), lambda i, j: (0, 0, j)),
                  pl.BlockSpec((2, 1, tc), lambda i, j: (0, 0, j)),
                  pl.BlockSpec((1, tc), lambda i, j: (0, j))],
        out_specs=[pl.BlockSpec((2, tm, tc), lambda i, j: (0, i, j)),
                   pl.BlockSpec((2, tm, tc), lambda i, j: (0, i, j)),
                   pl.BlockSpec((tm, tc), lambda i, j: (i, j)),
                   pl.BlockSpec((tm, tc), lambda i, j: (i, j))],
        out_shape=[jax.ShapeDtypeStruct((2, r, d), F32), jax.ShapeDtypeStruct((2, r, d), BF16),
                   jax.ShapeDtypeStruct((r, d), BF16), jax.ShapeDtypeStruct((r, d), BF16)],
        compiler_params=_cparams(("parallel", "arbitrary"), vmem),
        name="rwkv_mid",
    )(rkv, hid, w2, a2, g2, w0, a0, k_k)


def _rwkv_scan_kernel(r_ref, k_ref, v_ref, kk_ref, a_ref, lw_ref, ka_ref, y_ref, s_scr, *, rev, npairs):
    @pl.when(pl.program_id(2) == 0)
    def _():
        s_scr[...] = jnp.zeros_like(s_scr)

    ln = RWKV_CHUNK
    row = lax.broadcasted_iota(jnp.int32, (ln, LANE), 0)
    lane = lax.broadcasted_iota(jnp.int32, (ln, LANE), 1)
    col = lane % RWKV_N
    head0 = lane < RWKV_N
    strict = (col > row) if rev else (col < row)
    incl = (col >= row) if rev else (col <= row)
    eye = (col == row).astype(F32)

    def same_block(size):
        return (row // size) == (col // size)

    leaf = same_block(INV_LEAF)
    offs = [same_block(2 * sz) & jnp.logical_not(same_block(sz))
            for sz in (INV_LEAF << i for i in range(int(math.log2(ln // INV_LEAF))))]
    ti = lax.broadcasted_iota(jnp.int32, (ln, ln), 0)
    si = lax.broadcasted_iota(jnp.int32, (ln, ln), 1)
    tri = ((si >= ti) if rev else (si <= ti)).astype(BF16)
    ra = lax.broadcasted_iota(jnp.int32, (LANE, LANE), 0)
    rb = lax.broadcasted_iota(jnp.int32, (LANE, LANE), 1)
    same_head = (ra // RWKV_N) == (rb // RWKV_N)
    eye_full = ra == rb
    last = 0 if rev else ln - 1

    def bd(x):
        zero = jnp.zeros_like(x)
        return jnp.concatenate([jnp.where(head0, x, zero), jnp.where(head0, zero, x)], axis=0)

    def pm(a, b):
        return _dot(a.astype(BF16), bd(b.astype(BF16)))

    pairs = range(npairs)
    sls = [slice(p * LANE, (p + 1) * LANE) for p in pairs]

    def prepare(sl):
        lw = lw_ref[0, :, sl]
        hi = lw.astype(BF16)
        lo = (lw - hi.astype(F32)).astype(BF16)
        cc = _dot(tri, jnp.concatenate([hi, lo], axis=1))
        c = cc[:, :LANE] + cc[:, LANE:]
        a = a_ref[0, :, sl].astype(F32)
        kk = kk_ref[:, sl].astype(F32)
        winv = jnp.exp(-c)
        kd = k_ref[0, :, sl].astype(F32) * (1.0 + (a - 1.0) * ka_ref[:, sl])
        at = (-(kk * jnp.exp(c - lw))).astype(BF16)
        bt = (kk * a * winv).astype(BF16)
        kt = (kd * winv).astype(BF16)
        rt = (r_ref[0, :, sl].astype(F32) * jnp.exp(c)).astype(BF16)
        ctot = jnp.sum(jnp.where(eye_full, c[last:last + 1, :], 0.0), axis=1, keepdims=True)
        return at, bt, kt, rt, jnp.exp(ctot)

    prep = [prepare(sl) for sl in sls]
    res = [_dot_nt(jnp.concatenate([at, rt], axis=0), jnp.concatenate([bd(bt), bd(kt)], axis=0))
           for at, bt, kt, rt, _ in prep]
    mab = [jnp.where(strict, x[:ln, :LANE], 0.0) for x in res]
    mak = [jnp.where(strict, x[:ln, LANE:], 0.0).astype(BF16) for x in res]
    rbk = [jnp.concatenate([jnp.where(incl, x[ln:, :LANE], 0.0), jnp.where(incl, x[ln:, LANE:], 0.0)],
                           axis=1).astype(BF16) for x in res]

    md = [jnp.where(leaf, m, 0.0) for m in mab]
    sq = [pm(m, m) for m in md]
    tinv = [eye + m + pm(q, eye + m) for m, q in zip(md, sq)]
    for off in offs:
        half = [pm(t, jnp.where(off, m, 0.0)) for t, m in zip(tinv, mab)]
        tinv = [t + pm(hf, t) for t, hf in zip(tinv, half)]

    s0 = [s_scr[p] for p in pairs]
    s0b = [s.astype(BF16) for s in s0]
    vbd = [bd(v_ref[0, :, sl]) for sl in sls]
    x = [_dot(jnp.concatenate([pr[0], mk], axis=1), jnp.concatenate([sb, vb], axis=0))
         for pr, sb, mk, vb in zip(prep, s0b, mak, vbd)]
    ub = [pm(t, xx).astype(BF16) for t, xx in zip(tinv, x)]
    for p in pairs:
        at, bt, kt, rt, wtot = prep[p]
        y = _dot(jnp.concatenate([rt, rbk[p]], axis=1), jnp.concatenate([s0b[p], bd(ub[p]), vbd[p]], axis=0))
        y_ref[:, sls[p]] = y.astype(y_ref.dtype)
        upd = _dot_tn(jnp.concatenate([bt, kt], axis=0), jnp.concatenate([ub[p], v_ref[0, :, sls[p]]], axis=0))
        s_scr[p] = (s0[p] + jnp.where(same_head, upd, 0.0)) * wtot


def _rwkv_scan(rkv, kk, a, lw, k_a, dims, rev):
    _, r, d = rkv.shape
    npairs = min(32, d // LANE)
    pw = npairs * LANE
    ln = RWKV_CHUNK
    nc, _, rowblk = _chunk_maps(dims, ln, rev)
    dsel = 1 if rev else 0

    def spec3(lead):
        return pl.BlockSpec((1, ln, pw), lambda b, g, n: (lead, rowblk(b, n), g))

    return pl.pallas_call(
        functools.partial(_rwkv_scan_kernel, rev=rev, npairs=npairs),
        grid=(dims["B"], d // pw, nc),
        in_specs=[spec3(0), spec3(1), spec3(2),
                  pl.BlockSpec((ln, pw), lambda b, g, n: (rowblk(b, n), g)),
                  spec3(dsel), spec3(dsel),
                  pl.BlockSpec((1, pw), lambda b, g, n: (0, g))],
        out_specs=pl.BlockSpec((ln, pw), lambda b, g, n: (rowblk(b, n), g)),
        out_shape=jax.ShapeDtypeStruct((r, d), BF16),
        scratch_shapes=[pltpu.VMEM((npairs, LANE, LANE), F32)],
        compiler_params=_cparams(("parallel", "parallel", "arbitrary"),
                                 2 * ln * pw * (5 * 2 + 4 + 2) + npairs * LANE * LANE * 4 + (8 << 20)),
        name="rwkv_scan_bwd" if rev else "rwkv_scan_fwd",
    )(rkv, rkv, rkv, kk, a, lw, k_a)


def _rwkv_finish_kernel(yf_ref, yb_ref, r_ref, k_ref, v_ref, a_ref, gate_ref, ka_ref, rk_ref, lw_ref, lb_ref, z_ref):
    ones = _pair_ones()
    inv_n = 1.0 / RWKV_N
    y = yf_ref[...].astype(F32) + yb_ref[...].astype(F32)
    mean = _head_sums(y, ones) * inv_n
    yc = y - mean
    var = _head_sums(yc * yc, ones) * inv_n
    yn = yc * lax.rsqrt(var + LNX_EPS) * lw_ref[...] + lb_ref[...]
    k = k_ref[0].astype(F32)
    ka = ka_ref[...]
    kd0 = k * (1.0 + (a_ref[0].astype(F32) - 1.0) * ka)
    kd1 = k * (1.0 + (a_ref[1].astype(F32) - 1.0) * ka)
    kb = 0.5 * (kd0 + kd1)
    bonus = _head_sums(r_ref[0].astype(F32) * kb * rk_ref[...], ones) * v_ref[0].astype(F32)
    z_ref[...] = ((yn + bonus) * gate_ref[...].astype(F32)).astype(z_ref.dtype)


def _rwkv_finish(yf, yb, rkv, a, gate, k_a, r_k, lnx_w, lnx_b, dims, rows):
    d = yf.shape[1]
    tm = _pow2_tile(256, dims["S"], dims["RC"])
    tc = _pow2_tile(512, d)
    blk = pl.BlockSpec((tm, tc), lambda i, j: (i, j))
    vec = pl.BlockSpec((1, tc), lambda i, j: (0, j))

    def lead(g):
        return pl.BlockSpec((1, tm, tc), lambda i, j: (g, i, j))

    return pl.pallas_call(
        _rwkv_finish_kernel,
        grid=(rows // tm, d // tc),
        in_specs=[blk, blk, lead(0), lead(1), lead(2),
                  pl.BlockSpec((2, tm, tc), lambda i, j: (0, i, j)), blk, vec, vec, vec, vec],
        out_specs=blk,
        out_shape=jax.ShapeDtypeStruct((rows, d), BF16),
        compiler_params=_cparams(("parallel", "parallel"), 2 * tm * tc * 2 * 9 + 16 * tm * tc * 4),
        name="rwkv_finish",
    )(yf, yb, rkv, rkv, rkv, a, gate, k_a, r_k, lnx_w, lnx_b)


def _even_layer(h, mod, norm_g, w_main, w_gate_cols, b_gate, conv_qk, gn_ret, gn_ml, w_out, dims, rows_out):
    d = h.shape[1]
    heads, nqk, nv = dims["H"], dims["NQK"], dims["NV"]
    n_gate = 4 * heads
    w_gate = jnp.pad(w_gate_cols, ((0, 0), (0, LANE - n_gate))).astype(BF16)
    bg = jnp.pad(b_gate, (0, LANE - n_gate)).reshape(1, LANE)
    shift, scale, gate = mod[0], mod[1], mod[2]
    p, gates = _inproj(h, norm_g[0].reshape(1, d), shift, scale, w_main, w_gate, bg, dims)
    post = jnp.concatenate([jnp.ones((nqk,), F32), jnp.full((nqk,), HEAD_DK ** -0.5, F32)]).reshape(1, 2 * nqk)
    qk = _conv_qk(p, conv_qk, post, dims)
    cos, sin = _rope_tables(dims)
    yrf = _retention(p, cos, sin, dims, False)
    yrb = _retention(p, cos, sin, dims, True)
    ymf = _mlstm(qk, p, gates, dims, False)
    ymb = _mlstm(qk, p, gates, dims, True)
    z = _merge(yrf, yrb, ymf, ymb, p, gn_ret.reshape(1, nv), gn_ml.reshape(1, nv), dims, rows_out)
    return _outproj(z, w_out, h, gate, norm_g[1].reshape(1, d), dims, rows_out)


def _rwkv_layer(h, mod, norm_g, mu, w_rkv, w0, w1, w2, a0, a1, a2, g1, g2, k_k, k_a, r_k, lnx_w, lnx_b, w_out,
                dims, rows_out):
    d = h.shape[1]
    shift, scale, gate = mod[0], mod[1], mod[2]
    order = jnp.array([0, 2, 3, 1, 4, 5])
    mu6 = mu[order].reshape(6, 1, d)
    x6 = _rwkv_prep(h, norm_g[0].reshape(1, d), shift, scale, mu6, dims)
    rkv = _bmm(x6, w_rkv, 0, BF16, dims)
    lora = w1.shape[2]
    glora = g1.shape[1]
    lp = _round_up(lora, LANE)
    gp = _round_up(glora, LANE)
    lw_tot = max(2 * lp, gp)

    def two(w):
        w = jnp.pad(w, ((0, 0), (0, 0), (0, lp - lora)))
        w = jnp.concatenate([w[0], w[1]], axis=1)
        return jnp.pad(w, ((0, 0), (0, lw_tot - 2 * lp)))

    w_l1 = jnp.stack([two(w1), two(a1), jnp.pad(g1, ((0, 0), (0, lw_tot - glora)))]).astype(BF16)
    hid = _bmm(x6, w_l1, 3, F32, dims)
    pad_rows = lambda w, n: jnp.pad(w, ((0, 0),) * (w.ndim - 2) + ((0, n - w.shape[-2]), (0, 0))).astype(BF16)
    lw, a, kk, gmul = _rwkv_mid(rkv, hid, pad_rows(w2, lp), pad_rows(a2, lp), pad_rows(g2, lw_tot),
                                w0.reshape(2, 1, d), a0.reshape(2, 1, d), k_k.reshape(1, d), dims)
    ka = k_a.reshape(1, d)
    yf = _rwkv_scan(rkv, kk, a, lw, ka, dims, False)
    yb = _rwkv_scan(rkv, kk, a, lw, ka, dims, True)
    z = _rwkv_finish(yf, yb, rkv, a, gmul, ka, r_k.reshape(1, d), lnx_w.reshape(1, d), lnx_b.reshape(1, d),
                     dims, rows_out)
    return _outproj(z, w_out, h, gate, norm_g[1].reshape(1, d), dims, rows_out)


def kernel(x, c, ctx, c_ctx, ada_w, ada_b, norm_g, mlp_w_in, mlp_w_out, ev_w_in, ev_b_gate, ev_conv_qk,
           ev_gn_ret, ev_gn_mlstm, ev_w_out, od_mu, od_w_rkv, od_w0, od_w1, od_w2, od_a0, od_a1, od_a2,
           od_g1, od_g2, od_k_k, od_k_a, od_r_k, od_lnx_w, od_lnx_b, od_w_out):
    b, s, d = x.shape
    cl = ctx.shape[1]
    depth = ada_w.shape[0]
    heads = d // (2 * HEAD_DV)
    assert b + 1 <= MOD_ROWS and s % CHUNK == 0 and cl % CHUNK == 0 and s % GRID_W == 0
    dims = dict(B=b, S=s, C=cl, RL=b * s, RC=b * cl, H=heads, NQK=heads * HEAD_DK, NV=heads * HEAD_DV)
    rl, r = dims["RL"], dims["RL"] + dims["RC"]

    cond = jnp.concatenate([c, c_ctx[None, :], jnp.zeros((MOD_ROWS - b - 1, d), F32)], axis=0)
    tables = _ada_tables(cond, ada_w, ada_b)
    h = jnp.concatenate([x.reshape(rl, d), ctx.reshape(b * cl, d)], axis=0)

    for layer in range(depth):
        with_ctx = layer < depth - 1
        rows_out = r if with_ctx else rl
        mod = tables[layer].reshape(MOD_ROWS, 6, 1, d).transpose(1, 0, 2, 3)
        g = norm_g[layer]
        if layer % 2 == 0:
            e = layer // 2
            n_main = 4 * dims["NQK"] + 4 * dims["NV"]
            h_mix = _even_layer(h, mod[0:3], g[0:2], ev_w_in[e][:, :n_main].astype(BF16), ev_w_in[e][:, n_main:],
                                ev_b_gate[e], ev_conv_qk[e], ev_gn_ret[e], ev_gn_mlstm[e], _to_bf16(ev_w_out, e),
                                dims, rows_out)
        else:
            o = layer // 2
            h_mix = _rwkv_layer(h, mod[0:3], g[0:2], od_mu[o], _to_bf16(od_w_rkv, o), od_w0[o], od_w1[o], od_w2[o],
                                od_a0[o], od_a1[o], od_a2[o], od_g1[o], od_g2[o], od_k_k[o], od_k_a[o], od_r_k[o],
                                od_lnx_w[o], od_lnx_b[o], _to_bf16(od_w_out, o), dims, rows_out)
        h = _mlp(h_mix, g[2].reshape(1, d), mod[3], mod[4], _to_bf16(mlp_w_in, layer),
                 _to_bf16(mlp_w_out, layer), mod[5], g[3].reshape(1, d), dims, rows_out)
    return h[:rl].reshape(b, s, d)
```

```python
import functools
import math

import jax
import jax.numpy as jnp
from jax import lax
from jax.experimental import pallas as pl
from jax.experimental.pallas import tpu as pltpu

F32 = jnp.float32
BF16 = jnp.bfloat16

EPS = 1e-6
LNX_EPS = 64e-5
ROPE_BASE = 10000.0
GRID_W = 64
CHUNK = 128
RWKV_CHUNK = 64
HEAD_DK = 128
HEAD_DV = 256
RWKV_N = 64
INV_LEAF = 4
CONV_K = 3
LANE = 128
MOD_ROWS = 8
VMEM_BYTES = 64 * 1024 * 1024

NT_DIMS = (((1,), (1,)), ((), ()))
TN_DIMS = (((0,), (0,)), ((), ()))


def _dot(a, b):
    return jnp.dot(a, b, preferred_element_type=F32)


def _dot_nt(a, b):
    return lax.dot_general(a, b, NT_DIMS, preferred_element_type=F32)


def _dot_tn(a, b):
    return lax.dot_general(a, b, TN_DIMS, preferred_element_type=F32)


def _split_dot(m, x):
    hi = x.astype(BF16)
    lo = (x - hi.astype(F32)).astype(BF16)
    return _dot(m, hi) + _dot(m, lo)


def _cparams(semantics, vmem_bytes):
    limit = int(min(max(vmem_bytes * 5 // 4 + (4 << 20), 16 << 20), VMEM_BYTES - (6 << 20)))
    return pltpu.CompilerParams(dimension_semantics=semantics, vmem_limit_bytes=limit)


def _pow2_tile(pref, *dims):
    t = pref
    while any(d % t for d in dims):
        t //= 2
    return t


def _round_up(x, m):
    return (x + m - 1) // m * m


CAST_BLOCK_ELEMS = 1 << 20


def _cast_kernel(x_ref, o_ref):
    o_ref[...] = x_ref[...].astype(o_ref.dtype)


def _to_bf16(w, lead, ncols=None):
    shape = w.shape[1:]
    w3 = w.reshape(w.shape[0], -1, shape[-1])
    rows = w3.shape[1]
    ncols = shape[-1] if ncols is None else ncols
    tc = _pow2_tile(4096, ncols)
    tr = _pow2_tile(max(CAST_BLOCK_ELEMS // tc, 16), rows)
    out = pl.pallas_call(
        _cast_kernel,
        grid=(rows // tr, ncols // tc),
        in_specs=[pl.BlockSpec((None, tr, tc), lambda i, j: (lead, i, j))],
        out_specs=pl.BlockSpec((tr, tc), lambda i, j: (i, j)),
        out_shape=jax.ShapeDtypeStruct((rows, ncols), BF16),
        compiler_params=_cparams(("parallel", "parallel"), 2 * tr * tc * 6),
        name="weight_cast",
    )(w3)
    return out.reshape(shape[:-1] + (ncols,))


def _ada_kernel(c_ref, w_ref, b_ref, o_ref):
    c = c_ref[...]
    s = c * jax.nn.sigmoid(c)
    o_ref[0] = _dot(s.astype(BF16), w_ref[0].astype(BF16)) + b_ref[0]


def _ada_tables(cond, ada_w, ada_b):
    depth, d, n = ada_w.shape
    tn = _pow2_tile(512, n)
    return pl.pallas_call(
        _ada_kernel,
        grid=(depth, n // tn),
        in_specs=[pl.BlockSpec((MOD_ROWS, d), lambda l, j: (0, 0)),
                  pl.BlockSpec((1, d, tn), lambda l, j: (l, 0, j)),
                  pl.BlockSpec((1, 1, tn), lambda l, j: (l, 0, j))],
        out_specs=pl.BlockSpec((1, MOD_ROWS, tn), lambda l, j: (l, 0, j)),
        out_shape=jax.ShapeDtypeStruct((depth, MOD_ROWS, n), F32),
        compiler_params=_cparams(("parallel", "parallel"), 2 * d * tn * 4 + d * tn * 2),
        name="ada_tables",
    )(cond, ada_w, ada_b.reshape(depth, 1, n))


ROW_BLOCK = 32
NORM_COLS = 512


def _col_slices(d):
    step = min(NORM_COLS, d)
    return [slice(c, c + step) for c in range(0, d, step)]


def _row_rstd(x_ref, sl, cols, d):
    acc = None
    for c in cols:
        x = x_ref[sl, c]
        acc = x * x if acc is None else acc + x * x
    return lax.rsqrt(jnp.sum(acc, axis=-1, keepdims=True) / d + EPS)


def _modulate_rows(h_ref, g_ref, sh_ref, sc_ref, u_scr, tm):
    d = h_ref.shape[1]
    cols = _col_slices(d)

    def body(r, carry):
        sl = pl.ds(pl.multiple_of(r * ROW_BLOCK, ROW_BLOCK), ROW_BLOCK)
        rs = _row_rstd(h_ref, sl, cols, d)
        for c in cols:
            u = (h_ref[sl, c] * rs * g_ref[:, c]) * (1.0 + sc_ref[0, :, c]) + sh_ref[0, :, c]
            u_scr[sl, c] = u.astype(u_scr.dtype)
        return carry

    lax.fori_loop(0, tm // ROW_BLOCK, body, 0)


def _norm_residual_rows(acc_ref, h_ref, gate_ref, g_ref, o_ref, tm):
    d = acc_ref.shape[1]
    cols = _col_slices(d)

    def body(r, carry):
        sl = pl.ds(pl.multiple_of(r * ROW_BLOCK, ROW_BLOCK), ROW_BLOCK)
        rs = _row_rstd(acc_ref, sl, cols, d)
        for c in cols:
            o_ref[sl, c] = h_ref[sl, c] + gate_ref[0, :, c] * (acc_ref[sl, c] * rs * g_ref[:, c])
        return carry

    lax.fori_loop(0, tm // ROW_BLOCK, body, 0)


def _mod_row_map(tm, n_lat_tiles, s, b):
    def f(i):
        return jnp.where(i < n_lat_tiles, (i * tm) // s, b)
    return f


def _inproj_kernel(h_ref, g_ref, sh_ref, sc_ref, w_ref, wg_ref, bg_ref, p_ref, gt_ref, u_scr, *, tm):
    @pl.when(pl.program_id(1) == 0)
    def _():
        _modulate_rows(h_ref, g_ref, sh_ref, sc_ref, u_scr, tm)
        gt_ref[...] = _dot(u_scr[...], wg_ref[...]) + bg_ref[...]

    p_ref[...] = _dot(u_scr[...], w_ref[...]).astype(p_ref.dtype)


def _inproj(h, g, shift, scale, w, n, wg, bg, dims):
    r, d = h.shape
    tm = _pow2_tile(512, dims["S"], dims["RC"])
    tn = _pow2_tile(1024, n)
    modrow = _mod_row_map(tm, dims["RL"] // tm, dims["S"], dims["B"])
    vmem = 2 * tm * d * 4 + tm * d * 2 + 2 * d * tn * 2 + 2 * tm * tn * 2 + 2 * d * LANE * 2 + 2 * tm * LANE * 4
    return pl.pallas_call(
        functools.partial(_inproj_kernel, tm=tm),
        grid=(r // tm, n // tn),
        in_specs=[pl.BlockSpec((tm, d), lambda i, j: (i, 0)),
                  pl.BlockSpec((1, d), lambda i, j: (0, 0)),
                  pl.BlockSpec((1, 1, d), lambda i, j: (modrow(i), 0, 0)),
                  pl.BlockSpec((1, 1, d), lambda i, j: (modrow(i), 0, 0)),
                  pl.BlockSpec((d, tn), lambda i, j: (0, j)),
                  pl.BlockSpec((d, LANE), lambda i, j: (0, 0)),
                  pl.BlockSpec((1, LANE), lambda i, j: (0, 0))],
        out_specs=[pl.BlockSpec((tm, tn), lambda i, j: (i, j)),
                   pl.BlockSpec((tm, LANE), lambda i, j: (i, 0))],
        out_shape=[jax.ShapeDtypeStruct((r, n), BF16), jax.ShapeDtypeStruct((r, LANE), F32)],
        scratch_shapes=[pltpu.VMEM((tm, d), BF16)],
        compiler_params=_cparams(("parallel", "arbitrary"), vmem),
        name="even_inproj",
    )(h, g, shift, scale, w, wg, bg)


def _conv_kernel(x_ref, w_ref, post_ref, o_ref, *, rows, grid_mode):
    x = x_ref[...].astype(F32)
    t = lax.broadcasted_iota(jnp.int32, x.shape, 0)
    w = w_ref[...]
    if grid_mode:
        col = t % GRID_W
        has_prev, has_next = col > 0, col < GRID_W - 1
    else:
        has_prev, has_next = t > 0, t < rows - 1
    xl = jnp.where(has_prev, pltpu.roll(x, 1, 0), 0.0)
    xr = jnp.where(has_next, pltpu.roll(x, rows - 1, 0), 0.0)

    def tap_row(dr):
        return xl * w[3 * dr:3 * dr + 1] + x * w[3 * dr + 1:3 * dr + 2] + xr * w[3 * dr + 2:3 * dr + 3]

    y = tap_row(1)
    if grid_mode:
        z = jnp.zeros((GRID_W, x.shape[1]), F32)
        y = y + jnp.concatenate([z, tap_row(0)[:rows - GRID_W]], axis=0)
        y = y + jnp.concatenate([tap_row(2)[GRID_W:], z], axis=0)
    y = y * jax.nn.sigmoid(y)
    o_ref[...] = (y * post_ref[...]).astype(o_ref.dtype)


def _conv_qk(p, conv_w, post, dims):
    b, s, c = dims["B"], dims["S"], dims["C"]
    r = p.shape[0]
    nqk = dims["NQK"]
    ch = 2 * nqk
    tc = LANE
    col0 = 6 * nqk // tc
    w9 = conv_w.reshape(CONV_K * CONV_K, ch)

    def call(rows, blk0, grid_mode):
        return pl.pallas_call(
            functools.partial(_conv_kernel, rows=rows, grid_mode=grid_mode),
            grid=(b, ch // tc),
            in_specs=[pl.BlockSpec((rows, tc), lambda i, j: (blk0 + i, col0 + j)),
                      pl.BlockSpec((CONV_K * CONV_K, tc), lambda i, j: (0, j)),
                      pl.BlockSpec((1, tc), lambda i, j: (0, j))],
            out_specs=pl.BlockSpec((rows, tc), lambda i, j: (i, j)),
            out_shape=jax.ShapeDtypeStruct((b * rows, ch), BF16),
            compiler_params=_cparams(("parallel", "parallel"), 16 * rows * tc * 4),
            name="conv_lat" if grid_mode else "conv_ctx",
        )(p, w9, post)

    return jnp.concatenate([call(s, 0, True), call(c, (b * s) // c, False)], axis=0)


def _chunk_maps(dims, chunk, rev):
    b_, s, c = dims["B"], dims["S"], dims["C"]
    ncc, ncl = c // chunk, s // chunk
    nc = ncc + ncl

    def pos(n):
        if not rev:
            return n
        return jnp.where(n < ncc, ncc - 1 - n, ncc + nc - 1 - n)

    def rowblk(b, n):
        q = pos(n)
        return jnp.where(q < ncc, (b_ * s) // chunk + b * ncc + q, b * ncl + q - ncc)

    return nc, pos, rowblk


def _retention_kernel(q_ref, k_ref, v_ref, cos_ref, sin_ref, intra_ref, qd_ref, kd_ref, cd_ref, y_ref, s_scr,
                      *, heads):
    @pl.when(pl.program_id(1) == 0)
    def _():
        s_scr[...] = jnp.zeros_like(s_scr)

    cos = cos_ref[...]
    sin = sin_ref[...]
    lane = lax.broadcasted_iota(jnp.int32, cos.shape, 1)
    first = (lane % (HEAD_DK // 2)) < (HEAD_DK // 4)

    def rope(x):
        swapped = jnp.where(first, pltpu.roll(x, HEAD_DK - HEAD_DK // 4, 1), pltpu.roll(x, HEAD_DK // 4, 1))
        return x * cos + swapped * sin

    for h in range(heads):
        ks = slice(h * HEAD_DK, (h + 1) * HEAD_DK)
        vs = slice(h * HEAD_DV, (h + 1) * HEAD_DV)
        q = rope(q_ref[:, ks].astype(F32))
        k = rope(k_ref[:, ks].astype(F32)) * HEAD_DK ** -0.5
        v = v_ref[:, vs]
        s = s_scr[h]
        sc = _dot_nt(q.astype(BF16), k.astype(BF16)) * intra_ref[h]
        o = _dot(sc.astype(BF16), v) + _dot((q * qd_ref[h]).astype(BF16), s.astype(BF16))
        s_scr[h] = s * cd_ref[h] + _dot_tn((k * kd_ref[h]).astype(BF16), v)
        y_ref[:, vs] = o.astype(y_ref.dtype)


def _retention_tables(heads, rev):
    h = jnp.arange(heads, dtype=F32) / max(heads - 1, 1)
    lg = jnp.log1p(-jnp.exp2(-(5.0 + 7.0 * h)))
    if rev:
        lg = lg[::-1]
    pos = jnp.arange(CHUNK, dtype=F32)
    diff = pos[:, None] - pos[None, :]
    if rev:
        diff = -diff
    intra = jnp.where(diff >= 0, jnp.exp(jnp.maximum(diff, 0.0)[None] * lg[:, None, None]), 0.0)
    q_pow = (CHUNK - pos) if rev else (pos + 1.0)
    k_pow = pos if rev else (CHUNK - 1.0 - pos)
    qd = jnp.exp(q_pow[None, :] * lg[:, None])[..., None]
    kd = jnp.exp(k_pow[None, :] * lg[:, None])[..., None]
    cd = jnp.exp(CHUNK * lg)[:, None, None]
    return intra, qd, kd, cd


def _retention(p, cos, sin, dims, rev):
    heads, nqk, nv = dims["H"], dims["NQK"], dims["NV"]
    r = p.shape[0]
    nc, pos, rowblk = _chunk_maps(dims, CHUNK, rev)
    intra, qd, kd, cd = _retention_tables(heads, rev)
    full = lambda shape: pl.BlockSpec(shape, lambda b, n: (0,) * len(shape))
    vmem = 2 * CHUNK * (2 * nqk + 2 * nv) * 2 + 2 * heads * CHUNK * (CHUNK + 2 * LANE) * 4 + heads * HEAD_DK * HEAD_DV * 4
    return pl.pallas_call(
        functools.partial(_retention_kernel, heads=heads),
        grid=(dims["B"], nc),
        in_specs=[pl.BlockSpec((CHUNK, nqk), lambda b, n: (rowblk(b, n), 0)),
                  pl.BlockSpec((CHUNK, nqk), lambda b, n: (rowblk(b, n), 1)),
                  pl.BlockSpec((CHUNK, nv), lambda b, n: (rowblk(b, n), 1)),
                  pl.BlockSpec((CHUNK, HEAD_DK), lambda b, n: (pos(n), 0)),
                  pl.BlockSpec((CHUNK, HEAD_DK), lambda b, n: (pos(n), 0)),
                  full((heads, CHUNK, CHUNK)), full((heads, CHUNK, 1)), full((heads, CHUNK, 1)),
                  full((heads, 1, 1))],
        out_specs=pl.BlockSpec((CHUNK, nv), lambda b, n: (rowblk(b, n), 0)),
        out_shape=jax.ShapeDtypeStruct((r, nv), BF16),
        scratch_shapes=[pltpu.VMEM((heads, HEAD_DK, HEAD_DV), F32)],
        compiler_params=_cparams(("parallel", "arbitrary"), vmem),
        name="retention_bwd" if rev else "retention_fwd",
    )(p, p, p, cos, sin, intra, qd, kd, cd)


def _rope_tables(dims):
    s, c = dims["S"], dims["C"]
    quarter = HEAD_DK // 4
    inv = jnp.power(ROPE_BASE, -jnp.arange(quarter, dtype=F32) / quarter)
    t = jnp.arange(s)
    rows = (t // GRID_W).astype(F32)
    cols = (t % GRID_W).astype(F32)
    ar = rows[:, None] * inv[None, :]
    ac = cols[:, None] * inv[None, :]
    cos = jnp.concatenate([jnp.cos(ar), jnp.cos(ar), jnp.cos(ac), jnp.cos(ac)], axis=-1)
    sin = jnp.concatenate([-jnp.sin(ar), jnp.sin(ar), -jnp.sin(ac), jnp.sin(ac)], axis=-1)
    cos = jnp.concatenate([jnp.ones((c, HEAD_DK), F32), cos], axis=0)
    sin = jnp.concatenate([jnp.zeros((c, HEAD_DK), F32), sin], axis=0)
    return cos, sin


def _mlstm_kernel(q_ref, k_ref, v_ref, g_ref, y_ref, c_scr, n_scr, m_scr, *, heads, rev):
    @pl.when(pl.program_id(1) == 0)
    def _():
        c_scr[...] = jnp.zeros_like(c_scr)
        n_scr[...] = jnp.zeros_like(n_scr)
        m_scr[...] = jnp.zeros_like(m_scr)

    ti = lax.broadcasted_iota(jnp.int32, (CHUNK, CHUNK), 0)
    si = lax.broadcasted_iota(jnp.int32, (CHUNK, CHUNK), 1)
    causal = (si >= ti) if rev else (si <= ti)
    tri = causal.astype(BF16)
    g = g_ref[...]
    logf = jnp.minimum(g, 0.0) - jnp.log1p(jnp.exp(-jnp.abs(g)))
    cs = _split_dot(tri, logf)
    g_t = g.T
    cs_t = cs.T
    icol0 = 2 * heads if rev else 0
    fcol0 = icol0 + heads
    last = 0 if rev else CHUNK - 1

    for h in range(heads):
        ks = slice(h * HEAD_DK, (h + 1) * HEAD_DK)
        vs = slice(h * HEAD_DV, (h + 1) * HEAD_DV)
        ic, fc = icol0 + h, fcol0 + h
        b_col, b_row = cs[:, fc:fc + 1], cs_t[fc:fc + 1, :]
        i_col, i_row = g[:, ic:ic + 1], g_t[ic:ic + 1, :]
        b_last = cs[last:last + 1, fc:fc + 1]
        m = m_scr[h]
        q = q_ref[:, ks]
        k = k_ref[:, ks]
        v = v_ref[:, vs]
        c_mem = c_scr[h]
        n_mem = n_scr[h]

        d_log = jnp.where(causal, b_col - b_row + i_row, -jnp.inf)
        m_inter = b_col + m
        m_row = jnp.maximum(jnp.max(d_log, axis=-1, keepdims=True), m_inter)
        w_intra = jnp.exp(d_log - m_row)
        w_inter = jnp.exp(m_inter - m_row)
        sc = _dot_nt(q, k) * w_intra
        num = _dot(sc.astype(BF16), v) + w_inter * _dot(q, c_mem.astype(BF16))
        den = jnp.sum(sc, axis=-1, keepdims=True) + w_inter * jnp.sum(q.astype(F32) * n_mem, axis=-1, keepdims=True)
        y_ref[:, vs] = (num / jnp.maximum(jnp.abs(den), jnp.exp(-m_row))).astype(y_ref.dtype)

        gk = b_last - b_col + i_col
        m_new = jnp.maximum(b_last + m, jnp.max(gk, axis=0, keepdims=True))
        w_k = jnp.exp(gk - m_new)
        w_c = jnp.exp(b_last + m - m_new)
        kw = k.astype(F32) * w_k
        c_scr[h] = w_c * c_mem + _dot_tn(kw.astype(BF16), v)
        n_scr[h] = w_c * n_mem + jnp.sum(kw, axis=0, keepdims=True)
        m_scr[h] = m_new


def _mlstm(qk, p, gates, dims, rev):
    heads, nqk, nv = dims["H"], dims["NQK"], dims["NV"]
    r = p.shape[0]
    nc, _, rowblk = _chunk_maps(dims, CHUNK, rev)
    vmem = 2 * CHUNK * (2 * nqk + 2 * nv) * 2 + 2 * CHUNK * LANE * 4 + heads * HEAD_DK * (HEAD_DV + 8) * 4
    return pl.pallas_call(
        functools.partial(_mlstm_kernel, heads=heads, rev=rev),
        grid=(dims["B"], nc),
        in_specs=[pl.BlockSpec((CHUNK, nqk), lambda b, n: (rowblk(b, n), 0)),
                  pl.BlockSpec((CHUNK, nqk), lambda b, n: (rowblk(b, n), 1)),
                  pl.BlockSpec((CHUNK, nv), lambda b, n: (rowblk(b, n), 4)),
                  pl.BlockSpec((CHUNK, LANE), lambda b, n: (rowblk(b, n), 0))],
        out_specs=pl.BlockSpec((CHUNK, nv), lambda b, n: (rowblk(b, n), 0)),
        out_shape=jax.ShapeDtypeStruct((r, nv), BF16),
        scratch_shapes=[pltpu.VMEM((heads, HEAD_DK, HEAD_DV), F32),
                        pltpu.VMEM((heads, 1, HEAD_DK), F32),
                        pltpu.VMEM((heads, 1, 1), F32)],
        compiler_params=_cparams(("parallel", "arbitrary"), vmem),
        name="mlstm_bwd" if rev else "mlstm_fwd",
    )(qk, qk, p, gates)


def _merge_kernel(yrf_ref, yrb_ref, ymf_ref, ymb_ref, ag_ref, bo_ref, gr_ref, gm_ref, z_ref, *, heads):
    nv = heads * HEAD_DV
    for h in range(heads):
        vs = slice(h * HEAD_DV, (h + 1) * HEAD_DV)
        yr = yrf_ref[:, vs].astype(F32) + yrb_ref[:, vs].astype(F32)
        yr = yr * lax.rsqrt(jnp.mean(yr * yr, axis=-1, keepdims=True) + EPS) * gr_ref[:, vs]
        ag = ag_ref[:, vs].astype(F32)
        z_ref[:, vs] = (yr * (ag * jax.nn.sigmoid(ag))).astype(z_ref.dtype)
        ym = ymf_ref[:, vs].astype(F32) + ymb_ref[:, vs].astype(F32)
        ym = ym * lax.rsqrt(jnp.mean(ym * ym, axis=-1, keepdims=True) + EPS) * gm_ref[:, vs]
        bo = bo_ref[:, vs].astype(F32)
        z_ref[:, nv + h * HEAD_DV:nv + (h + 1) * HEAD_DV] = (ym * jax.nn.sigmoid(bo)).astype(z_ref.dtype)


def _merge(yrf, yrb, ymf, ymb, p, gn_ret, gn_ml, dims, rows):
    heads, nv = dims["H"], dims["NV"]
    tm = _pow2_tile(128, dims["S"], dims["RC"])
    ys = pl.BlockSpec((tm, nv), lambda i: (i, 0))
    return pl.pallas_call(
        functools.partial(_merge_kernel, heads=heads),
        grid=(rows // tm,),
        in_specs=[ys, ys, ys, ys,
                  pl.BlockSpec((tm, nv), lambda i: (i, 2)),
                  pl.BlockSpec((tm, nv), lambda i: (i, 5)),
                  pl.BlockSpec((1, nv), lambda i: (0, 0)),
                  pl.BlockSpec((1, nv), lambda i: (0, 0))],
        out_specs=pl.BlockSpec((tm, 2 * nv), lambda i: (i, 0)),
        out_shape=jax.ShapeDtypeStruct((rows, 2 * nv), BF16),
        compiler_params=_cparams(("parallel",), 2 * tm * nv * 2 * 8 + 8 * tm * nv * 4),
        name="even_merge",
    )(yrf, yrb, ymf, ymb, p, p, gn_ret, gn_ml)


def _outproj_kernel(z_ref, w_ref, h_ref, gate_ref, g_ref, o_ref, *, tm, tn):
    j = pl.program_id(1)
    o_ref[:, pl.ds(pl.multiple_of(j * tn, tn), tn)] = _dot(z_ref[...], w_ref[...])

    @pl.when(j == pl.num_programs(1) - 1)
    def _():
        _norm_residual_rows(o_ref, h_ref, gate_ref, g_ref, o_ref, tm)


def _outproj(z, w, h, gate, g, dims, rows):
    k, n = w.shape
    tm = _pow2_tile(512, dims["S"], dims["RC"])
    tn = _pow2_tile(512, n)
    modrow = _mod_row_map(tm, dims["RL"] // tm, dims["S"], dims["B"])
    vmem = 2 * tm * k * 2 + 2 * k * tn * 2 + tm * n * 4 * 4 + 2 * tm * tn * 4
    return pl.pallas_call(
        functools.partial(_outproj_kernel, tm=tm, tn=tn),
        grid=(rows // tm, n // tn),
        in_specs=[pl.BlockSpec((tm, k), lambda i, j: (i, 0)),
                  pl.BlockSpec((k, tn), lambda i, j: (0, j)),
                  pl.BlockSpec((tm, n), lambda i, j: (i, 0)),
                  pl.BlockSpec((1, 1, n), lambda i, j: (modrow(i), 0, 0)),
                  pl.BlockSpec((1, n), lambda i, j: (0, 0))],
        out_specs=pl.BlockSpec((tm, n), lambda i, j: (i, 0)),
        out_shape=jax.ShapeDtypeStruct((rows, n), F32),
        compiler_params=_cparams(("parallel", "arbitrary"), vmem),
        name="outproj_norm_residual",
    )(z, w, h, gate, g)


def _mlp_kernel(h_ref, gpre_ref, sh_ref, sc_ref, w1_ref, w2_ref, gate_ref, gpost_ref, o_ref, u_scr, *, tm):
    f = pl.program_id(1)

    @pl.when(f == 0)
    def _():
        _modulate_rows(h_ref, gpre_ref, sh_ref, sc_ref, u_scr, tm)
        o_ref[...] = jnp.zeros_like(o_ref)

    a = jnp.maximum(_dot(u_scr[...], w1_ref[...]), 0.0)
    o_ref[...] += _dot((a * a).astype(BF16), w2_ref[...])

    @pl.when(f == pl.num_programs(1) - 1)
    def _():
        _norm_residual_rows(o_ref, h_ref, gate_ref, gpost_ref, o_ref, tm)


def _mlp(h, g_pre, shift, scale, w1, w2, gate, g_post, dims, rows):
    d, ff = w1.shape
    tm = _pow2_tile(512, dims["S"], dims["RC"])
    tf = _pow2_tile(512, ff)
    modrow = _mod_row_map(tm, dims["RL"] // tm, dims["S"], dims["B"])
    mod_spec = pl.BlockSpec((1, 1, d), lambda i, f: (modrow(i), 0, 0))
    row_spec = pl.BlockSpec((1, d), lambda i, f: (0, 0))
    vmem = 4 * tm * d * 4 + tm * d * 2 + 4 * d * tf * 2 + 3 * tm * tf * 4
    return pl.pallas_call(
        functools.partial(_mlp_kernel, tm=tm),
        grid=(rows // tm, ff // tf),
        in_specs=[pl.BlockSpec((tm, d), lambda i, f: (i, 0)),
                  row_spec, mod_spec, mod_spec,
                  pl.BlockSpec((d, tf), lambda i, f: (0, f)),
                  pl.BlockSpec((tf, d), lambda i, f: (f, 0)),
                  mod_spec, row_spec],
        out_specs=pl.BlockSpec((tm, d), lambda i, f: (i, 0)),
        out_shape=jax.ShapeDtypeStruct((rows, d), F32),
        scratch_shapes=[pltpu.VMEM((tm, d), BF16)],
        compiler_params=_cparams(("parallel", "arbitrary"), vmem),
        name="mlp_block",
    )(h, g_pre, shift, scale, w1, w2, gate, g_post)


HALO = GRID_W


def _rwkv_prep_kernel(h_ref, ha_ref, hb_ref, g_ref, sh_ref, sc_ref, mu_ref, o_ref, rs_scr, ra_scr, rb_scr,
                      *, tm, tc, n_lat_tiles, s, c):
    i = pl.program_id(0)
    jc = pl.program_id(1)
    is_lat = i < n_lat_tiles
    cs = pl.ds(pl.multiple_of(jc * tc, tc), tc)

    @pl.when(jc == 0)
    def _():
        def rstd(x):
            return lax.rsqrt(jnp.mean(x * x, axis=-1, keepdims=True) + EPS)

        def body(r, carry):
            sl = pl.ds(pl.multiple_of(r * ROW_BLOCK, ROW_BLOCK), ROW_BLOCK)
            rs_scr[sl, :] = rstd(h_ref[sl, :])
            return carry

        lax.fori_loop(0, tm // ROW_BLOCK, body, 0)
        ra_scr[...] = rstd(ha_ref[...])
        rb_scr[...] = rstd(hb_ref[...])

    g = g_ref[...]
    sh = sh_ref[0]
    sc = sc_ref[0]

    def mod(x, rs):
        return (x * rs * g) * (1.0 + sc) + sh

    u = mod(h_ref[:, cs], rs_scr[...])
    t = lax.broadcasted_iota(jnp.int32, u.shape, 0)
    period = jnp.where(is_lat, GRID_W, c)
    tpos = t & (period - 1)

    def emit(shifted):
        xx = shifted - u
        for m in range(6):
            o_ref[m] = (u + xx * mu_ref[m]).astype(o_ref.dtype)

    use_prev = (jc == 0) | (jnp.logical_not(is_lat) & (jc == 1))
    use_next = (is_lat & (jc == 1)) | (jnp.logical_not(is_lat) & (jc >= 2))

    @pl.when(use_prev)
    def _():
        emit(jnp.where(tpos > 0, pltpu.roll(u, 1, 0), 0.0))

    @pl.when(use_next)
    def _():
        emit(jnp.where(tpos < period - 1, pltpu.roll(u, tm - 1, 0), 0.0))

    @pl.when(is_lat & (jc == 2))
    def _():
        first = (i * tm) % s == 0
        above = jnp.where(first, 0.0, mod(ha_ref[:, cs], ra_scr[...]))
        emit(jnp.concatenate([above, u[:tm - HALO]], axis=0))

    @pl.when(is_lat & (jc == 3))
    def _():
        lastt = ((i + 1) * tm) % s == 0
        below = jnp.where(lastt, 0.0, mod(hb_ref[:, cs], rb_scr[...]))
        emit(jnp.concatenate([u[HALO:], below], axis=0))


def _rwkv_prep(h, g, shift, scale, mu6, dims):
    r, d = h.shape
    s, c = dims["S"], dims["C"]
    tm = c
    assert tm % HALO == 0 and s % tm == 0 and tm & (tm - 1) == 0
    tc = d // 4
    hb = tm // HALO
    nblk = r // HALO
    n_lat_tiles = dims["RL"] // tm
    modrow = _mod_row_map(tm, n_lat_tiles, s, dims["B"])
    mod_spec = pl.BlockSpec((1, 1, tc), lambda i, j: (modrow(i), 0, j))
    vmem = 2 * tm * d * 4 + 4 * HALO * d * 4 + 2 * 6 * tm * tc * 2 + 12 * tm * tc * 4
    return pl.pallas_call(
        functools.partial(_rwkv_prep_kernel, tm=tm, tc=tc, n_lat_tiles=n_lat_tiles, s=s, c=c),
        grid=(r // tm, 4),
        in_specs=[pl.BlockSpec((tm, d), lambda i, j: (i, 0)),
                  pl.BlockSpec((HALO, d), lambda i, j: (jnp.maximum(i * hb - 1, 0), 0)),
                  pl.BlockSpec((HALO, d), lambda i, j: (jnp.minimum((i + 1) * hb, nblk - 1), 0)),
                  pl.BlockSpec((1, tc), lambda i, j: (0, j)),
                  mod_spec, mod_spec,
                  pl.BlockSpec((6, 1, tc), lambda i, j: (0, 0, j))],
        out_specs=pl.BlockSpec((6, tm, tc), lambda i, j: (0, i, j)),
        out_shape=jax.ShapeDtypeStruct((6, r, d), BF16),
        scratch_shapes=[pltpu.VMEM((tm, 1), F32), pltpu.VMEM((HALO, 1), F32), pltpu.VMEM((HALO, 1), F32)],
        compiler_params=_cparams(("parallel", "arbitrary"), vmem),
        name="rwkv_prep",
    )(h, h, h, g, shift, scale, mu6)


def _bmm_kernel(x_ref, w_ref, o_ref):
    o_ref[0] = _dot(x_ref[0], w_ref[0]).astype(o_ref.dtype)


def _bmm(x, w, lhs0, out_dtype, dims):
    ng, k, n = w.shape
    r = x.shape[1]
    tm = _pow2_tile(512, dims["S"], dims["RC"])
    tn = _pow2_tile(1024, n)
    osz = jnp.dtype(out_dtype).itemsize
    return pl.pallas_call(
        _bmm_kernel,
        grid=(ng, r // tm, n // tn),
        in_specs=[pl.BlockSpec((1, tm, k), lambda g, i, j: (lhs0 + g, i, 0)),
                  pl.BlockSpec((1, k, tn), lambda g, i, j: (g, 0, j))],
        out_specs=pl.BlockSpec((1, tm, tn), lambda g, i, j: (g, i, j)),
        out_shape=jax.ShapeDtypeStruct((ng, r, n), out_dtype),
        compiler_params=_cparams(("parallel", "parallel", "arbitrary"),
                                 2 * tm * k * 2 + 2 * k * tn * 2 + 2 * tm * tn * osz + tm * tn * 4),
        name="rwkv_bmm",
    )(x, w)


def _pair_ones():
    a = lax.broadcasted_iota(jnp.int32, (LANE, LANE), 0) // RWKV_N
    b = lax.broadcasted_iota(jnp.int32, (LANE, LANE), 1) // RWKV_N
    return (a == b).astype(BF16)


def _head_sums(x, ones):
    parts = [_dot(x[:, l:l + LANE].astype(BF16), ones) for l in range(0, x.shape[1], LANE)]
    return parts[0] if len(parts) == 1 else jnp.concatenate(parts, axis=1)


def _rwkv_mid_kernel(k_ref, hid_ref, w2_ref, a2_ref, g2_ref, w0_ref, a0_ref, kk_w_ref,
                     lw_ref, a_ref, kk_ref, gate_ref, *, lp):
    ones = _pair_ones()
    hw = hid_ref[0]
    ha = hid_ref[1]
    hg = hid_ref[2]
    for d in range(2):
        tw = jnp.tanh(hw[:, d * lp:(d + 1) * lp]).astype(BF16)
        z = w0_ref[d] + _dot(tw, w2_ref[d])
        lw_ref[d] = -math.exp(-0.5) * jax.nn.sigmoid(z)
        za = a0_ref[d] + _dot(ha[:, d * lp:(d + 1) * lp].astype(BF16), a2_ref[d])
        a_ref[d] = jax.nn.sigmoid(za).astype(a_ref.dtype)
    gate_ref[...] = _dot(jax.nn.sigmoid(hg).astype(BF16), g2_ref[...]).astype(gate_ref.dtype)
    kk = k_ref[0].astype(F32) * kk_w_ref[...]
    ss = _head_sums(kk * kk, ones)
    kk_ref[...] = (kk * lax.rsqrt(jnp.maximum(ss, 1e-24))).astype(kk_ref.dtype)


def _rwkv_mid(rkv, hid, w2, a2, g2, w0, a0, k_k, dims):
    _, r, d = rkv.shape
    lp = w2.shape[1]
    lw_tot = hid.shape[2]
    tm = _pow2_tile(256, dims["S"], dims["RC"])
    tc = _pow2_tile(1024, d)
    vmem = 2 * (tm * tc * 2 + 3 * tm * lw_tot * 4 + (4 * lp + g2.shape[0]) * tc * 2
                + tm * tc * (8 + 4 + 2 + 2)) + 10 * tm * tc * 4
    return pl.pallas_call(
        functools.partial(_rwkv_mid_kernel, lp=lp),
        grid=(r // tm, d // tc),
        in_specs=[pl.BlockSpec((1, tm, tc), lambda i, j: (1, i, j)),
                  pl.BlockSpec((3, tm, lw_tot), lambda i, j: (0, i, 0)),
                  pl.BlockSpec((2, lp, tc), lambda i, j: (0, 0, j)),
                  pl.BlockSpec((2, lp, tc), lambda i, j: (0, 0, j)),
                  pl.BlockSpec((g2.shape[0], tc), lambda i, j: (0, j)),
                  pl.BlockSpec((2, 1, tc), lambda i, j: (0, 0, j)),
                  pl.BlockSpec((2, 1, tc), lambda i, j: (0, 0, j)),
                  pl.BlockSpec((1, tc), lambda i, j: (0, j))],
        out_specs=[pl.BlockSpec((2, tm, tc), lambda i, j: (0, i, j)),
                   pl.BlockSpec((2, tm, tc), lambda i, j: (0, i, j)),
                   pl.BlockSpec((tm, tc), lambda i, j: (i, j)),
                   pl.BlockSpec((tm, tc), lambda i, j: (i, j))],
        out_shape=[jax.ShapeDtypeStruct((2, r, d), F32), jax.ShapeDtypeStruct((2, r, d), BF16),
                   jax.ShapeDtypeStruct((r, d), BF16), jax.ShapeDtypeStruct((r, d), BF16)],
        compiler_params=_cparams(("parallel", "arbitrary"), vmem),
        name="rwkv_mid",
    )(rkv, hid, w2, a2, g2, w0, a0, k_k)


def _rwkv_scan_kernel(r_ref, k_ref, v_ref, kk_ref, a_ref, lw_ref, ka_ref, y_ref, s_scr, *, rev, npairs):
    @pl.when(pl.program_id(2) == 0)
    def _():
        s_scr[...] = jnp.zeros_like(s_scr)

    ln = RWKV_CHUNK
    row = lax.broadcasted_iota(jnp.int32, (ln, LANE), 0)
    lane = lax.broadcasted_iota(jnp.int32, (ln, LANE), 1)
    col = lane % RWKV_N
    head0 = lane < RWKV_N
    strict = (col > row) if rev else (col < row)
    incl = (col >= row) if rev else (col <= row)
    eye = (col == row).astype(F32)

    def same_block(size):
        return (row // size) == (col // size)

    leaf = same_block(INV_LEAF)
    offs = [same_block(2 * sz) & jnp.logical_not(same_block(sz))
            for sz in (INV_LEAF << i for i in range(int(math.log2(ln // INV_LEAF))))]
    ti = lax.broadcasted_iota(jnp.int32, (ln, ln), 0)
    si = lax.broadcasted_iota(jnp.int32, (ln, ln), 1)
    tri = ((si >= ti) if rev else (si <= ti)).astype(BF16)
    ra = lax.broadcasted_iota(jnp.int32, (LANE, LANE), 0)
    rb = lax.broadcasted_iota(jnp.int32, (LANE, LANE), 1)
    same_head = (ra // RWKV_N) == (rb // RWKV_N)
    eye_full = ra == rb
    last = 0 if rev else ln - 1

    def bd(x):
        zero = jnp.zeros_like(x)
        return jnp.concatenate([jnp.where(head0, x, zero), jnp.where(head0, zero, x)], axis=0)

    def pm(a, b):
        return _dot(a.astype(BF16), bd(b.astype(BF16)))

    sls = [slice(p * LANE, (p + 1) * LANE) for p in range(npairs)]

    def prepare(sl):
        lw = lw_ref[0, :, sl]
        hi = lw.astype(BF16)
        lo = (lw - hi.astype(F32)).astype(BF16)
        cc = _dot(tri, jnp.concatenate([hi, lo], axis=1))
        c = cc[:, :LANE] + cc[:, LANE:]
        a = a_ref[0, :, sl].astype(F32)
        kk = kk_ref[:, sl].astype(F32)
        winv = jnp.exp(-c)
        kd = k_ref[0, :, sl].astype(F32) * (1.0 + (a - 1.0) * ka_ref[:, sl])
        at = (-(kk * jnp.exp(c - lw))).astype(BF16)
        bt = (kk * a * winv).astype(BF16)
        kt = (kd * winv).astype(BF16)
        rt = (r_ref[0, :, sl].astype(F32) * jnp.exp(c)).astype(BF16)
        ctot = jnp.sum(jnp.where(eye_full, c[last:last + 1, :], 0.0), axis=1, keepdims=True)
        return at, bt, kt, rt, jnp.exp(ctot)

    pairs = range(npairs)
    prep = [prepare(sl) for sl in sls]
    ar = [jnp.concatenate([at, rt], axis=0) for at, _, _, rt, _ in prep]
    res = [_dot_nt(lhs, jnp.concatenate([bd(bt), bd(kt)], axis=0)) for lhs, (_, bt, kt, _, _) in zip(ar, prep)]
    mab = [jnp.where(strict, x[:ln, :LANE], 0.0) for x in res]
    mrk = [jnp.concatenate([jnp.where(strict, x[:ln, LANE:], 0.0), jnp.where(incl, x[ln:, LANE:], 0.0)],
                           axis=0).astype(BF16) for x in res]
    rbm = [jnp.where(incl, x[ln:, :LANE], 0.0).astype(BF16) for x in res]

    md = [jnp.where(leaf, m, 0.0) for m in mab]
    sq = [pm(m, m) for m in md]
    tinv = [eye + m + pm(q, eye + m) for m, q in zip(md, sq)]
    for off in offs:
        half = [pm(t, jnp.where(off, m, 0.0)) for t, m in zip(tinv, mab)]
        tinv = [t + pm(hf, t) for t, hf in zip(tinv, half)]

    s0 = [s_scr[p] for p in pairs]
    s0b = [s.astype(BF16) for s in s0]
    xy = [_dot(lhs, sb) + _dot(mk, bd(v_ref[0, :, sl])) for lhs, sb, mk, sl in zip(ar, s0b, mrk, sls)]
    ub = [pm(t, q[:ln]).astype(BF16) for t, q in zip(tinv, xy)]
    for p in pairs:
        at, bt, kt, rt, wtot = prep[p]
        y = xy[p][ln:] + _dot(rbm[p], bd(ub[p]))
        y_ref[:, sls[p]] = y.astype(y_ref.dtype)
        upd = _dot_tn(jnp.concatenate([bt, kt], axis=0), jnp.concatenate([ub[p], v_ref[0, :, sls[p]]], axis=0))
        s_scr[p] = (s0[p] + jnp.where(same_head, upd, 0.0)) * wtot


def _rwkv_scan(rkv, kk, a, lw, k_a, dims, rev):
    _, r, d = rkv.shape
    npairs = min(32, d // LANE)
    pw = npairs * LANE
    ln = RWKV_CHUNK
    nc, _, rowblk = _chunk_maps(dims, ln, rev)
    dsel = 1 if rev else 0

    def spec3(lead):
        return pl.BlockSpec((1, ln, pw), lambda b, g, n: (lead, rowblk(b, n), g))

    return pl.pallas_call(
        functools.partial(_rwkv_scan_kernel, rev=rev, npairs=npairs),
        grid=(dims["B"], d // pw, nc),
        in_specs=[spec3(0), spec3(1), spec3(2),
                  pl.BlockSpec((ln, pw), lambda b, g, n: (rowblk(b, n), g)),
                  spec3(dsel), spec3(dsel),
                  pl.BlockSpec((1, pw), lambda b, g, n: (0, g))],
        out_specs=pl.BlockSpec((ln, pw), lambda b, g, n: (rowblk(b, n), g)),
        out_shape=jax.ShapeDtypeStruct((r, d), BF16),
        scratch_shapes=[pltpu.VMEM((npairs, LANE, LANE), F32)],
        compiler_params=_cparams(("parallel", "parallel", "arbitrary"),
                                 2 * ln * pw * (5 * 2 + 4 + 2) + npairs * LANE * LANE * 4 + (8 << 20)),
        name="rwkv_scan_bwd" if rev else "rwkv_scan_fwd",
    )(rkv, rkv, rkv, kk, a, lw, k_a)


def _rwkv_finish_kernel(yf_ref, yb_ref, r_ref, k_ref, v_ref, a_ref, gate_ref, ka_ref, rk_ref, lw_ref, lb_ref, z_ref):
    ones = _pair_ones()
    inv_n = 1.0 / RWKV_N
    y = yf_ref[...].astype(F32) + yb_ref[...].astype(F32)
    mean = _head_sums(y, ones) * inv_n
    yc = y - mean
    var = _head_sums(yc * yc, ones) * inv_n
    yn = yc * lax.rsqrt(var + LNX_EPS) * lw_ref[...] + lb_ref[...]
    k = k_ref[0].astype(F32)
    ka = ka_ref[...]
    kd0 = k * (1.0 + (a_ref[0].astype(F32) - 1.0) * ka)
    kd1 = k * (1.0 + (a_ref[1].astype(F32) - 1.0) * ka)
    kb = 0.5 * (kd0 + kd1)
    bonus = _head_sums(r_ref[0].astype(F32) * kb * rk_ref[...], ones) * v_ref[0].astype(F32)
    z_ref[...] = ((yn + bonus) * gate_ref[...].astype(F32)).astype(z_ref.dtype)


def _rwkv_finish(yf, yb, rkv, a, gate, k_a, r_k, lnx_w, lnx_b, dims, rows):
    d = yf.shape[1]
    tm = _pow2_tile(256, dims["S"], dims["RC"])
    tc = _pow2_tile(512, d)
    blk = pl.BlockSpec((tm, tc), lambda i, j: (i, j))
    vec = pl.BlockSpec((1, tc), lambda i, j: (0, j))

    def lead(g):
        return pl.BlockSpec((1, tm, tc), lambda i, j: (g, i, j))

    return pl.pallas_call(
        _rwkv_finish_kernel,
        grid=(rows // tm, d // tc),
        in_specs=[blk, blk, lead(0), lead(1), lead(2),
                  pl.BlockSpec((2, tm, tc), lambda i, j: (0, i, j)), blk, vec, vec, vec, vec],
        out_specs=blk,
        out_shape=jax.ShapeDtypeStruct((rows, d), BF16),
        compiler_params=_cparams(("parallel", "parallel"), 2 * tm * tc * 2 * 9 + 16 * tm * tc * 4),
        name="rwkv_finish",
    )(yf, yb, rkv, rkv, rkv, a, gate, k_a, r_k, lnx_w, lnx_b)


def _even_layer(h, mod, norm_g, w_in, b_gate, conv_qk, gn_ret, gn_ml, w_out, dims, rows_out):
    d = h.shape[1]
    heads, nqk, nv = dims["H"], dims["NQK"], dims["NV"]
    n_main = 4 * nqk + 4 * nv
    n_gate = 4 * heads
    w_gate = jnp.pad(w_in[:, n_main:], ((0, 0), (0, LANE - n_gate)))
    bg = jnp.pad(b_gate, (0, LANE - n_gate)).reshape(1, LANE)
    shift, scale, gate = mod[0], mod[1], mod[2]
    p, gates = _inproj(h, norm_g[0].reshape(1, d), shift, scale, w_in, n_main, w_gate, bg, dims)
    post = jnp.concatenate([jnp.ones((nqk,), F32), jnp.full((nqk,), HEAD_DK ** -0.5, F32)]).reshape(1, 2 * nqk)
    qk = _conv_qk(p, conv_qk, post, dims)
    cos, sin = _rope_tables(dims)
    yrf = _retention(p, cos, sin, dims, False)
    yrb = _retention(p, cos, sin, dims, True)
    ymf = _mlstm(qk, p, gates, dims, False)
    ymb = _mlstm(qk, p, gates, dims, True)
    z = _merge(yrf, yrb, ymf, ymb, p, gn_ret.reshape(1, nv), gn_ml.reshape(1, nv), dims, rows_out)
    return _outproj(z, w_out, h, gate, norm_g[1].reshape(1, d), dims, rows_out)


def _rwkv_layer(h, mod, norm_g, mu, w_rkv, w0, w1, w2, a0, a1, a2, g1, g2, k_k, k_a, r_k, lnx_w, lnx_b, w_out,
                dims, rows_out):
    d = h.shape[1]
    shift, scale, gate = mod[0], mod[1], mod[2]
    order = jnp.array([0, 2, 3, 1, 4, 5])
    mu6 = mu[order].reshape(6, 1, d)
    x6 = _rwkv_prep(h, norm_g[0].reshape(1, d), shift, scale, mu6, dims)
    rkv = _bmm(x6, w_rkv, 0, BF16, dims)
    lora = w1.shape[2]
    glora = g1.shape[1]
    lp = _round_up(lora, LANE)
    gp = _round_up(glora, LANE)
    lw_tot = max(2 * lp, gp)

    def two(w):
        w = jnp.pad(w, ((0, 0), (0, 0), (0, lp - lora)))
        w = jnp.concatenate([w[0], w[1]], axis=1)
        return jnp.pad(w, ((0, 0), (0, lw_tot - 2 * lp)))

    w_l1 = jnp.stack([two(w1), two(a1), jnp.pad(g1, ((0, 0), (0, lw_tot - glora)))]).astype(BF16)
    hid = _bmm(x6, w_l1, 3, F32, dims)
    pad_rows = lambda w, n: jnp.pad(w, ((0, 0),) * (w.ndim - 2) + ((0, n - w.shape[-2]), (0, 0))).astype(BF16)
    lw, a, kk, gmul = _rwkv_mid(rkv, hid, pad_rows(w2, lp), pad_rows(a2, lp), pad_rows(g2, lw_tot),
                                w0.reshape(2, 1, d), a0.reshape(2, 1, d), k_k.reshape(1, d), dims)
    ka = k_a.reshape(1, d)
    yf = _rwkv_scan(rkv, kk, a, lw, ka, dims, False)
    yb = _rwkv_scan(rkv, kk, a, lw, ka, dims, True)
    z = _rwkv_finish(yf, yb, rkv, a, gmul, ka, r_k.reshape(1, d), lnx_w.reshape(1, d), lnx_b.reshape(1, d),
                     dims, rows_out)
    return _outproj(z, w_out, h, gate, norm_g[1].reshape(1, d), dims, rows_out)


def kernel(x, c, ctx, c_ctx, ada_w, ada_b, norm_g, mlp_w_in, mlp_w_out, ev_w_in, ev_b_gate, ev_conv_qk,
           ev_gn_ret, ev_gn_mlstm, ev_w_out, od_mu, od_w_rkv, od_w0, od_w1, od_w2, od_a0, od_a1, od_a2,
           od_g1, od_g2, od_k_k, od_k_a, od_r_k, od_lnx_w, od_lnx_b, od_w_out):
    b, s, d = x.shape
    cl = ctx.shape[1]
    depth = ada_w.shape[0]
    heads = d // (2 * HEAD_DV)
    assert b + 1 <= MOD_ROWS and s % CHUNK == 0 and cl % CHUNK == 0 and s % GRID_W == 0
    dims = dict(B=b, S=s, C=cl, RL=b * s, RC=b * cl, H=heads, NQK=heads * HEAD_DK, NV=heads * HEAD_DV)
    rl, r = dims["RL"], dims["RL"] + dims["RC"]

    cond = jnp.concatenate([c, c_ctx[None, :], jnp.zeros((MOD_ROWS - b - 1, d), F32)], axis=0)
    tables = _ada_tables(cond, ada_w, ada_b)
    h = jnp.concatenate([x.reshape(rl, d), ctx.reshape(b * cl, d)], axis=0)

    for layer in range(depth):
        with_ctx = layer < depth - 1
        rows_out = r if with_ctx else rl
        mod = tables[layer].reshape(MOD_ROWS, 6, 1, d).transpose(1, 0, 2, 3)
        g = norm_g[layer]
        if layer % 2 == 0:
            e = layer // 2
            h_mix = _even_layer(h, mod[0:3], g[0:2], ev_w_in[e].astype(BF16), ev_b_gate[e], ev_conv_qk[e],
                                ev_gn_ret[e], ev_gn_mlstm[e], _to_bf16(ev_w_out, e), dims, rows_out)
        else:
            o = layer // 2
            h_mix = _rwkv_layer(h, mod[0:3], g[0:2], od_mu[o], _to_bf16(od_w_rkv, o), od_w0[o], od_w1[o], od_w2[o],
                                od_a0[o], od_a1[o], od_a2[o], od_g1[o], od_g2[o], od_k_k[o], od_k_a[o], od_r_k[o],
                                od_lnx_w[o], od_lnx_b[o], _to_bf16(od_w_out, o), dims, rows_out)
        h = _mlp(h_mix, g[2].reshape(1, d), mod[3], mod[4], _to_bf16(mlp_w_in, layer),
                 _to_bf16(mlp_w_out, layer), mod[5], g[3].reshape(1, d), dims, rows_out)
    return h[:rl].reshape(b, s, d)
```

```python
import functools
import math

import jax
import jax.numpy as jnp
from jax import lax
from jax.experimental import pallas as pl
from jax.experimental.pallas import tpu as pltpu

F32 = jnp.float32
BF16 = jnp.bfloat16

EPS = 1e-6
LNX_EPS = 64e-5
ROPE_BASE = 10000.0
GRID_W = 64
CHUNK = 128
RWKV_CHUNK = 64
HEAD_DK = 128
HEAD_DV = 256
RWKV_N = 64
INV_LEAF = 4
CONV_K = 3
LANE = 128
MOD_ROWS = 8
VMEM_BYTES = 64 * 1024 * 1024

NT_DIMS = (((1,), (1,)), ((), ()))
TN_DIMS = (((0,), (0,)), ((), ()))


def _dot(a, b):
    return jnp.dot(a, b, preferred_element_type=F32)


def _dot_nt(a, b):
    return lax.dot_general(a, b, NT_DIMS, preferred_element_type=F32)


def _dot_tn(a, b):
    return lax.dot_general(a, b, TN_DIMS, preferred_element_type=F32)


def _split_dot(m, x):
    hi = x.astype(BF16)
    lo = (x - hi.astype(F32)).astype(BF16)
    return _dot(m, hi) + _dot(m, lo)


def _cparams(semantics, vmem_bytes):
    limit = int(min(max(vmem_bytes * 5 // 4 + (4 << 20), 16 << 20), VMEM_BYTES - (6 << 20)))
    return pltpu.CompilerParams(dimension_semantics=semantics, vmem_limit_bytes=limit)


def _pow2_tile(pref, *dims):
    t = pref
    while any(d % t for d in dims):
        t //= 2
    return t


def _round_up(x, m):
    return (x + m - 1) // m * m


CAST_BLOCK_ELEMS = 1 << 20


def _cast_kernel(x_ref, o_ref):
    o_ref[...] = x_ref[...].astype(o_ref.dtype)


def _to_bf16(w, lead, ncols=None):
    shape = w.shape[1:]
    w3 = w.reshape(w.shape[0], -1, shape[-1])
    rows = w3.shape[1]
    ncols = shape[-1] if ncols is None else ncols
    tc = _pow2_tile(4096, ncols)
    tr = _pow2_tile(max(CAST_BLOCK_ELEMS // tc, 16), rows)
    out = pl.pallas_call(
        _cast_kernel,
        grid=(rows // tr, ncols // tc),
        in_specs=[pl.BlockSpec((None, tr, tc), lambda i, j: (lead, i, j))],
        out_specs=pl.BlockSpec((tr, tc), lambda i, j: (i, j)),
        out_shape=jax.ShapeDtypeStruct((rows, ncols), BF16),
        compiler_params=_cparams(("parallel", "parallel"), 2 * tr * tc * 6),
        name="weight_cast",
    )(w3)
    return out.reshape(shape[:-1] + (ncols,))


def _ada_kernel(c_ref, w_ref, b_ref, o_ref):
    c = c_ref[...]
    s = c * jax.nn.sigmoid(c)
    o_ref[0] = _dot(s.astype(BF16), w_ref[0].astype(BF16)) + b_ref[0]


def _ada_tables(cond, ada_w, ada_b):
    depth, d, n = ada_w.shape
    tn = _pow2_tile(512, n)
    return pl.pallas_call(
        _ada_kernel,
        grid=(depth, n // tn),
        in_specs=[pl.BlockSpec((MOD_ROWS, d), lambda l, j: (0, 0)),
                  pl.BlockSpec((1, d, tn), lambda l, j: (l, 0, j)),
                  pl.BlockSpec((1, 1, tn), lambda l, j: (l, 0, j))],
        out_specs=pl.BlockSpec((1, MOD_ROWS, tn), lambda l, j: (l, 0, j)),
        out_shape=jax.ShapeDtypeStruct((depth, MOD_ROWS, n), F32),
        compiler_params=_cparams(("parallel", "parallel"), 2 * d * tn * 4 + d * tn * 2),
        name="ada_tables",
    )(cond, ada_w, ada_b.reshape(depth, 1, n))


ROW_BLOCK = 32
NORM_COLS = 512


def _col_slices(d):
    step = min(NORM_COLS, d)
    return [slice(c, c + step) for c in range(0, d, step)]


def _row_rstd(x_ref, sl, cols, d):
    acc = None
    for c in cols:
        x = x_ref[sl, c]
        acc = x * x if acc is None else acc + x * x
    return lax.rsqrt(jnp.sum(acc, axis=-1, keepdims=True) / d + EPS)


def _modulate_rows(h_ref, g_ref, sh_ref, sc_ref, u_scr, tm):
    d = h_ref.shape[1]
    cols = _col_slices(d)

    def body(r, carry):
        sl = pl.ds(pl.multiple_of(r * ROW_BLOCK, ROW_BLOCK), ROW_BLOCK)
        rs = _row_rstd(h_ref, sl, cols, d)
        for c in cols:
            u = (h_ref[sl, c] * rs * g_ref[:, c]) * (1.0 + sc_ref[0, :, c]) + sh_ref[0, :, c]
            u_scr[sl, c] = u.astype(u_scr.dtype)
        return carry

    lax.fori_loop(0, tm // ROW_BLOCK, body, 0)


def _norm_residual_rows(acc_ref, h_ref, gate_ref, g_ref, o_ref, tm):
    d = acc_ref.shape[1]
    cols = _col_slices(d)

    def body(r, carry):
        sl = pl.ds(pl.multiple_of(r * ROW_BLOCK, ROW_BLOCK), ROW_BLOCK)
        rs = _row_rstd(acc_ref, sl, cols, d)
        for c in cols:
            o_ref[sl, c] = h_ref[sl, c] + gate_ref[0, :, c] * (acc_ref[sl, c] * rs * g_ref[:, c])
        return carry

    lax.fori_loop(0, tm // ROW_BLOCK, body, 0)


def _mod_row_map(tm, n_lat_tiles, s, b):
    def f(i):
        return jnp.where(i < n_lat_tiles, (i * tm) // s, b)
    return f


def _inproj_kernel(h_ref, g_ref, sh_ref, sc_ref, w_ref, wg_ref, bg_ref, p_ref, gt_ref, u_scr, *, tm):
    @pl.when(pl.program_id(1) == 0)
    def _():
        _modulate_rows(h_ref, g_ref, sh_ref, sc_ref, u_scr, tm)
        gt_ref[...] = _dot(u_scr[...], wg_ref[...]) + bg_ref[...]

    p_ref[...] = _dot(u_scr[...], w_ref[...]).astype(p_ref.dtype)


def _inproj(h, g, shift, scale, w, n, wg, bg, dims):
    r, d = h.shape
    tm = _pow2_tile(512, dims["S"], dims["RC"])
    tn = _pow2_tile(1024, n)
    modrow = _mod_row_map(tm, dims["RL"] // tm, dims["S"], dims["B"])
    vmem = 2 * tm * d * 4 + tm * d * 2 + 2 * d * tn * 2 + 2 * tm * tn * 2 + 2 * d * LANE * 2 + 2 * tm * LANE * 4
    return pl.pallas_call(
        functools.partial(_inproj_kernel, tm=tm),
        grid=(r // tm, n // tn),
        in_specs=[pl.BlockSpec((tm, d), lambda i, j: (i, 0)),
                  pl.BlockSpec((1, d), lambda i, j: (0, 0)),
                  pl.BlockSpec((1, 1, d), lambda i, j: (modrow(i), 0, 0)),
                  pl.BlockSpec((1, 1, d), lambda i, j: (modrow(i), 0, 0)),
                  pl.BlockSpec((d, tn), lambda i, j: (0, j)),
                  pl.BlockSpec((d, LANE), lambda i, j: (0, 0)),
                  pl.BlockSpec((1, LANE), lambda i, j: (0, 0))],
        out_specs=[pl.BlockSpec((tm, tn), lambda i, j: (i, j)),
                   pl.BlockSpec((tm, LANE), lambda i, j: (i, 0))],
        out_shape=[jax.ShapeDtypeStruct((r, n), BF16), jax.ShapeDtypeStruct((r, LANE), F32)],
        scratch_shapes=[pltpu.VMEM((tm, d), BF16)],
        compiler_params=_cparams(("parallel", "arbitrary"), vmem),
        name="even_inproj",
    )(h, g, shift, scale, w, wg, bg)


def _conv_kernel(x_ref, w_ref, post_ref, o_ref, *, rows, grid_mode):
    x = x_ref[...].astype(F32)
    t = lax.broadcasted_iota(jnp.int32, x.shape, 0)
    w = w_ref[...]
    if grid_mode:
        col = t % GRID_W
        has_prev, has_next = col > 0, col < GRID_W - 1
    else:
        has_prev, has_next = t > 0, t < rows - 1
    xl = jnp.where(has_prev, pltpu.roll(x, 1, 0), 0.0)
    xr = jnp.where(has_next, pltpu.roll(x, rows - 1, 0), 0.0)

    def tap_row(dr):
        return xl * w[3 * dr:3 * dr + 1] + x * w[3 * dr + 1:3 * dr + 2] + xr * w[3 * dr + 2:3 * dr + 3]

    y = tap_row(1)
    if grid_mode:
        z = jnp.zeros((GRID_W, x.shape[1]), F32)
        y = y + jnp.concatenate([z, tap_row(0)[:rows - GRID_W]], axis=0)
        y = y + jnp.concatenate([tap_row(2)[GRID_W:], z], axis=0)
    y = y * jax.nn.sigmoid(y)
    o_ref[...] = (y * post_ref[...]).astype(o_ref.dtype)


def _conv_qk(p, conv_w, post, dims):
    b, s, c = dims["B"], dims["S"], dims["C"]
    nqk = dims["NQK"]
    ch = 2 * nqk
    tc = LANE
    col0 = 6 * nqk // tc
    w9 = conv_w.reshape(CONV_K * CONV_K, ch)

    def call(rows, blk0, grid_mode):
        return pl.pallas_call(
            functools.partial(_conv_kernel, rows=rows, grid_mode=grid_mode),
            grid=(b, ch // tc),
            in_specs=[pl.BlockSpec((rows, tc), lambda i, j: (blk0 + i, col0 + j)),
                      pl.BlockSpec((CONV_K * CONV_K, tc), lambda i, j: (0, j)),
                      pl.BlockSpec((1, tc), lambda i, j: (0, j))],
            out_specs=pl.BlockSpec((rows, tc), lambda i, j: (i, j)),
            out_shape=jax.ShapeDtypeStruct((b * rows, ch), BF16),
            compiler_params=_cparams(("parallel", "parallel"), 16 * rows * tc * 4),
            name="conv_lat" if grid_mode else "conv_ctx",
        )(p, w9, post)

    return jnp.concatenate([call(s, 0, True), call(c, (b * s) // c, False)], axis=0)


def _chunk_maps(dims, chunk, rev):
    b_, s, c = dims["B"], dims["S"], dims["C"]
    ncc, ncl = c // chunk, s // chunk
    nc = ncc + ncl

    def pos(n):
        if not rev:
            return n
        return jnp.where(n < ncc, ncc - 1 - n, ncc + nc - 1 - n)

    def rowblk(b, n):
        q = pos(n)
        return jnp.where(q < ncc, (b_ * s) // chunk + b * ncc + q, b * ncl + q - ncc)

    return nc, pos, rowblk


def _retention_kernel(q_ref, k_ref, v_ref, cos_ref, sin_ref, intra_ref, qd_ref, kd_ref, cd_ref, y_ref, s_scr,
                      *, heads):
    @pl.when(pl.program_id(1) == 0)
    def _():
        s_scr[...] = jnp.zeros_like(s_scr)

    cos = cos_ref[...]
    sin = sin_ref[...]
    lane = lax.broadcasted_iota(jnp.int32, cos.shape, 1)
    first = (lane % (HEAD_DK // 2)) < (HEAD_DK // 4)

    def rope(x):
        swapped = jnp.where(first, pltpu.roll(x, HEAD_DK - HEAD_DK // 4, 1), pltpu.roll(x, HEAD_DK // 4, 1))
        return x * cos + swapped * sin

    for h in range(heads):
        ks = slice(h * HEAD_DK, (h + 1) * HEAD_DK)
        vs = slice(h * HEAD_DV, (h + 1) * HEAD_DV)
        q = rope(q_ref[:, ks].astype(F32))
        k = rope(k_ref[:, ks].astype(F32)) * HEAD_DK ** -0.5
        v = v_ref[:, vs]
        s = s_scr[h]
        sc = _dot_nt(q.astype(BF16), k.astype(BF16)) * intra_ref[h]
        o = _dot(sc.astype(BF16), v) + _dot((q * qd_ref[h]).astype(BF16), s.astype(BF16))
        s_scr[h] = s * cd_ref[h] + _dot_tn((k * kd_ref[h]).astype(BF16), v)
        y_ref[:, vs] = o.astype(y_ref.dtype)


def _retention_tables(heads, rev):
    h = jnp.arange(heads, dtype=F32) / max(heads - 1, 1)
    lg = jnp.log1p(-jnp.exp2(-(5.0 + 7.0 * h)))
    if rev:
        lg = lg[::-1]
    pos = jnp.arange(CHUNK, dtype=F32)
    diff = pos[:, None] - pos[None, :]
    if rev:
        diff = -diff
    intra = jnp.where(diff >= 0, jnp.exp(jnp.maximum(diff, 0.0)[None] * lg[:, None, None]), 0.0)
    q_pow = (CHUNK - pos) if rev else (pos + 1.0)
    k_pow = pos if rev else (CHUNK - 1.0 - pos)
    qd = jnp.exp(q_pow[None, :] * lg[:, None])[..., None]
    kd = jnp.exp(k_pow[None, :] * lg[:, None])[..., None]
    cd = jnp.exp(CHUNK * lg)[:, None, None]
    return intra, qd, kd, cd


def _retention(p, cos, sin, dims, rev):
    heads, nqk, nv = dims["H"], dims["NQK"], dims["NV"]
    r = p.shape[0]
    nc, pos, rowblk = _chunk_maps(dims, CHUNK, rev)
    intra, qd, kd, cd = _retention_tables(heads, rev)
    full = lambda shape: pl.BlockSpec(shape, lambda b, n: (0,) * len(shape))
    vmem = (2 * CHUNK * (2 * nqk + 2 * nv) * 2 + 2 * heads * CHUNK * (CHUNK + 2 * LANE) * 4
            + heads * HEAD_DK * HEAD_DV * 4)
    return pl.pallas_call(
        functools.partial(_retention_kernel, heads=heads),
        grid=(dims["B"], nc),
        in_specs=[pl.BlockSpec((CHUNK, nqk), lambda b, n: (rowblk(b, n), 0)),
                  pl.BlockSpec((CHUNK, nqk), lambda b, n: (rowblk(b, n), 1)),
                  pl.BlockSpec((CHUNK, nv), lambda b, n: (rowblk(b, n), 1)),
                  pl.BlockSpec((CHUNK, HEAD_DK), lambda b, n: (pos(n), 0)),
                  pl.BlockSpec((CHUNK, HEAD_DK), lambda b, n: (pos(n), 0)),
                  full((heads, CHUNK, CHUNK)), full((heads, CHUNK, 1)), full((heads, CHUNK, 1)),
                  full((heads, 1, 1))],
        out_specs=pl.BlockSpec((CHUNK, nv), lambda b, n: (rowblk(b, n), 0)),
        out_shape=jax.ShapeDtypeStruct((r, nv), BF16),
        scratch_shapes=[pltpu.VMEM((heads, HEAD_DK, HEAD_DV), F32)],
        compiler_params=_cparams(("parallel", "arbitrary"), vmem),
        name="retention_bwd" if rev else "retention_fwd",
    )(p, p, p, cos, sin, intra, qd, kd, cd)


def _rope_tables(dims):
    s, c = dims["S"], dims["C"]
    quarter = HEAD_DK // 4
    inv = jnp.power(ROPE_BASE, -jnp.arange(quarter, dtype=F32) / quarter)
    t = jnp.arange(s)
    rows = (t // GRID_W).astype(F32)
    cols = (t % GRID_W).astype(F32)
    ar = rows[:, None] * inv[None, :]
    ac = cols[:, None] * inv[None, :]
    cos = jnp.concatenate([jnp.cos(ar), jnp.cos(ar), jnp.cos(ac), jnp.cos(ac)], axis=-1)
    sin = jnp.concatenate([-jnp.sin(ar), jnp.sin(ar), -jnp.sin(ac), jnp.sin(ac)], axis=-1)
    cos = jnp.concatenate([jnp.ones((c, HEAD_DK), F32), cos], axis=0)
    sin = jnp.concatenate([jnp.zeros((c, HEAD_DK), F32), sin], axis=0)
    return cos, sin


def _mlstm_kernel(q_ref, k_ref, v_ref, g_ref, y_ref, c_scr, n_scr, m_scr, *, heads, rev):
    @pl.when(pl.program_id(1) == 0)
    def _():
        c_scr[...] = jnp.zeros_like(c_scr)
        n_scr[...] = jnp.zeros_like(n_scr)
        m_scr[...] = jnp.zeros_like(m_scr)

    ti = lax.broadcasted_iota(jnp.int32, (CHUNK, CHUNK), 0)
    si = lax.broadcasted_iota(jnp.int32, (CHUNK, CHUNK), 1)
    causal = (si >= ti) if rev else (si <= ti)
    tri = causal.astype(BF16)
    g = g_ref[...]
    logf = jnp.minimum(g, 0.0) - jnp.log1p(jnp.exp(-jnp.abs(g)))
    cs = _split_dot(tri, logf)
    g_t = g.T
    cs_t = cs.T
    icol0 = 2 * heads if rev else 0
    fcol0 = icol0 + heads
    last = 0 if rev else CHUNK - 1

    for h in range(heads):
        ks = slice(h * HEAD_DK, (h + 1) * HEAD_DK)
        vs = slice(h * HEAD_DV, (h + 1) * HEAD_DV)
        ic, fc = icol0 + h, fcol0 + h
        b_col, b_row = cs[:, fc:fc + 1], cs_t[fc:fc + 1, :]
        i_col, i_row = g[:, ic:ic + 1], g_t[ic:ic + 1, :]
        b_last = cs[last:last + 1, fc:fc + 1]
        m = m_scr[h]
        q = q_ref[:, ks]
        k = k_ref[:, ks]
        v = v_ref[:, vs]
        c_mem = c_scr[h]
        n_mem = n_scr[h]

        d_log = jnp.where(causal, b_col - b_row + i_row, -jnp.inf)
        m_inter = b_col + m
        m_row = jnp.maximum(jnp.max(d_log, axis=-1, keepdims=True), m_inter)
        w_intra = jnp.exp(d_log - m_row)
        w_inter = jnp.exp(m_inter - m_row)
        sc = _dot_nt(q, k) * w_intra
        num = _dot(sc.astype(BF16), v) + w_inter * _dot(q, c_mem.astype(BF16))
        den = jnp.sum(sc, axis=-1, keepdims=True) + w_inter * jnp.sum(q.astype(F32) * n_mem, axis=-1, keepdims=True)
        y_ref[:, vs] = (num / jnp.maximum(jnp.abs(den), jnp.exp(-m_row))).astype(y_ref.dtype)

        gk = b_last - b_col + i_col
        m_new = jnp.maximum(b_last + m, jnp.max(gk, axis=0, keepdims=True))
        w_k = jnp.exp(gk - m_new)
        w_c = jnp.exp(b_last + m - m_new)
        kw = k.astype(F32) * w_k
        c_scr[h] = w_c * c_mem + _dot_tn(kw.astype(BF16), v)
        n_scr[h] = w_c * n_mem + jnp.sum(kw, axis=0, keepdims=True)
        m_scr[h] = m_new


def _mlstm(qk, p, gates, dims, rev):
    heads, nqk, nv = dims["H"], dims["NQK"], dims["NV"]
    r = p.shape[0]
    nc, _, rowblk = _chunk_maps(dims, CHUNK, rev)
    vmem = 2 * CHUNK * (2 * nqk + 2 * nv) * 2 + 2 * CHUNK * LANE * 4 + heads * HEAD_DK * (HEAD_DV + 8) * 4
    return pl.pallas_call(
        functools.partial(_mlstm_kernel, heads=heads, rev=rev),
        grid=(dims["B"], nc),
        in_specs=[pl.BlockSpec((CHUNK, nqk), lambda b, n: (rowblk(b, n), 0)),
                  pl.BlockSpec((CHUNK, nqk), lambda b, n: (rowblk(b, n), 1)),
                  pl.BlockSpec((CHUNK, nv), lambda b, n: (rowblk(b, n), 4)),
                  pl.BlockSpec((CHUNK, LANE), lambda b, n: (rowblk(b, n), 0))],
        out_specs=pl.BlockSpec((CHUNK, nv), lambda b, n: (rowblk(b, n), 0)),
        out_shape=jax.ShapeDtypeStruct((r, nv), BF16),
        scratch_shapes=[pltpu.VMEM((heads, HEAD_DK, HEAD_DV), F32),
                        pltpu.VMEM((heads, 1, HEAD_DK), F32),
                        pltpu.VMEM((heads, 1, 1), F32)],
        compiler_params=_cparams(("parallel", "arbitrary"), vmem),
        name="mlstm_bwd" if rev else "mlstm_fwd",
    )(qk, qk, p, gates)


def _merge_kernel(yrf_ref, yrb_ref, ymf_ref, ymb_ref, ag_ref, bo_ref, gr_ref, gm_ref, z_ref, *, heads):
    nv = heads * HEAD_DV
    for h in range(heads):
        vs = slice(h * HEAD_DV, (h + 1) * HEAD_DV)
        yr = yrf_ref[:, vs].astype(F32) + yrb_ref[:, vs].astype(F32)
        yr = yr * lax.rsqrt(jnp.mean(yr * yr, axis=-1, keepdims=True) + EPS) * gr_ref[:, vs]
        ag = ag_ref[:, vs].astype(F32)
        z_ref[:, vs] = (yr * (ag * jax.nn.sigmoid(ag))).astype(z_ref.dtype)
        ym = ymf_ref[:, vs].astype(F32) + ymb_ref[:, vs].astype(F32)
        ym = ym * lax.rsqrt(jnp.mean(ym * ym, axis=-1, keepdims=True) + EPS) * gm_ref[:, vs]
        bo = bo_ref[:, vs].astype(F32)
        z_ref[:, nv + h * HEAD_DV:nv + (h + 1) * HEAD_DV] = (ym * jax.nn.sigmoid(bo)).astype(z_ref.dtype)


def _merge(yrf, yrb, ymf, ymb, p, gn_ret, gn_ml, dims, rows):
    heads, nv = dims["H"], dims["NV"]
    tm = _pow2_tile(128, dims["S"], dims["RC"])
    ys = pl.BlockSpec((tm, nv), lambda i: (i, 0))
    return pl.pallas_call(
        functools.partial(_merge_kernel, heads=heads),
        grid=(rows // tm,),
        in_specs=[ys, ys, ys, ys,
                  pl.BlockSpec((tm, nv), lambda i: (i, 2)),
                  pl.BlockSpec((tm, nv), lambda i: (i, 5)),
                  pl.BlockSpec((1, nv), lambda i: (0, 0)),
                  pl.BlockSpec((1, nv), lambda i: (0, 0))],
        out_specs=pl.BlockSpec((tm, 2 * nv), lambda i: (i, 0)),
        out_shape=jax.ShapeDtypeStruct((rows, 2 * nv), BF16),
        compiler_params=_cparams(("parallel",), 2 * tm * nv * 2 * 8 + 8 * tm * nv * 4),
        name="even_merge",
    )(yrf, yrb, ymf, ymb, p, p, gn_ret, gn_ml)


def _outproj_kernel(z_ref, w_ref, h_ref, gate_ref, g_ref, o_ref, *, tm, tn):
    j = pl.program_id(1)
    o_ref[:, pl.ds(pl.multiple_of(j * tn, tn), tn)] = _dot(z_ref[...], w_ref[...])

    @pl.when(j == pl.num_programs(1) - 1)
    def _():
        _norm_residual_rows(o_ref, h_ref, gate_ref, g_ref, o_ref, tm)


def _outproj(z, w, h, gate, g, dims, rows):
    k, n = w.shape
    tm = _pow2_tile(512, dims["S"], dims["RC"])
    tn = _pow2_tile(512, n)
    modrow = _mod_row_map(tm, dims["RL"] // tm, dims["S"], dims["B"])
    vmem = 2 * tm * k * 2 + 2 * k * tn * 2 + tm * n * 4 * 4 + 2 * tm * tn * 4
    return pl.pallas_call(
        functools.partial(_outproj_kernel, tm=tm, tn=tn),
        grid=(rows // tm, n // tn),
        in_specs=[pl.BlockSpec((tm, k), lambda i, j: (i, 0)),
                  pl.BlockSpec((k, tn), lambda i, j: (0, j)),
                  pl.BlockSpec((tm, n), lambda i, j: (i, 0)),
                  pl.BlockSpec((1, 1, n), lambda i, j: (modrow(i), 0, 0)),
                  pl.BlockSpec((1, n), lambda i, j: (0, 0))],
        out_specs=pl.BlockSpec((tm, n), lambda i, j: (i, 0)),
        out_shape=jax.ShapeDtypeStruct((rows, n), F32),
        compiler_params=_cparams(("parallel", "arbitrary"), vmem),
        name="outproj_norm_residual",
    )(z, w, h, gate, g)


def _mlp_kernel(h_ref, gpre_ref, sh_ref, sc_ref, w1_ref, w2_ref, gate_ref, gpost_ref, o_ref, u_scr, *, tm):
    f = pl.program_id(1)

    @pl.when(f == 0)
    def _():
        _modulate_rows(h_ref, gpre_ref, sh_ref, sc_ref, u_scr, tm)
        o_ref[...] = jnp.zeros_like(o_ref)

    a = jnp.maximum(_dot(u_scr[...], w1_ref[...]), 0.0)
    o_ref[...] += _dot((a * a).astype(BF16), w2_ref[...])

    @pl.when(f == pl.num_programs(1) - 1)
    def _():
        _norm_residual_rows(o_ref, h_ref, gate_ref, gpost_ref, o_ref, tm)


def _mlp(h, g_pre, shift, scale, w1, w2, gate, g_post, dims, rows):
    d, ff = w1.shape
    tm = _pow2_tile(512, dims["S"], dims["RC"])
    tf = _pow2_tile(512, ff)
    modrow = _mod_row_map(tm, dims["RL"] // tm, dims["S"], dims["B"])
    mod_spec = pl.BlockSpec((1, 1, d), lambda i, f: (modrow(i), 0, 0))
    row_spec = pl.BlockSpec((1, d), lambda i, f: (0, 0))
    vmem = 4 * tm * d * 4 + tm * d * 2 + 4 * d * tf * 2 + 3 * tm * tf * 4
    return pl.pallas_call(
        functools.partial(_mlp_kernel, tm=tm),
        grid=(rows // tm, ff // tf),
        in_specs=[pl.BlockSpec((tm, d), lambda i, f: (i, 0)),
                  row_spec, mod_spec, mod_spec,
                  pl.BlockSpec((d, tf), lambda i, f: (0, f)),
                  pl.BlockSpec((tf, d), lambda i, f: (f, 0)),
                  mod_spec, row_spec],
        out_specs=pl.BlockSpec((tm, d), lambda i, f: (i, 0)),
        out_shape=jax.ShapeDtypeStruct((rows, d), F32),
        scratch_shapes=[pltpu.VMEM((tm, d), BF16)],
        compiler_params=_cparams(("parallel", "arbitrary"), vmem),
        name="mlp_block",
    )(h, g_pre, shift, scale, w1, w2, gate, g_post)


HALO = GRID_W


def _rwkv_prep_kernel(h_ref, ha_ref, hb_ref, g_ref, sh_ref, sc_ref, mu_ref, o_ref, rs_scr, ra_scr, rb_scr,
                      *, tm, tc, n_lat_tiles, s, c):
    i = pl.program_id(0)
    jc = pl.program_id(1)
    is_lat = i < n_lat_tiles
    cs = pl.ds(pl.multiple_of(jc * tc, tc), tc)

    @pl.when(jc == 0)
    def _():
        def rstd(x):
            return lax.rsqrt(jnp.mean(x * x, axis=-1, keepdims=True) + EPS)

        def body(r, carry):
            sl = pl.ds(pl.multiple_of(r * ROW_BLOCK, ROW_BLOCK), ROW_BLOCK)
            rs_scr[sl, :] = rstd(h_ref[sl, :])
            return carry

        lax.fori_loop(0, tm // ROW_BLOCK, body, 0)
        ra_scr[...] = rstd(ha_ref[...])
        rb_scr[...] = rstd(hb_ref[...])

    g = g_ref[...]
    sh = sh_ref[0]
    sc = sc_ref[0]

    def mod(x, rs):
        return (x * rs * g) * (1.0 + sc) + sh

    u = mod(h_ref[:, cs], rs_scr[...])
    t = lax.broadcasted_iota(jnp.int32, u.shape, 0)
    period = jnp.where(is_lat, GRID_W, c)
    tpos = t & (period - 1)

    def emit(shifted):
        xx = shifted - u
        for m in range(6):
            o_ref[m] = (u + xx * mu_ref[m]).astype(o_ref.dtype)

    use_prev = (jc == 0) | (jnp.logical_not(is_lat) & (jc == 1))
    use_next = (is_lat & (jc == 1)) | (jnp.logical_not(is_lat) & (jc >= 2))

    @pl.when(use_prev)
    def _():
        emit(jnp.where(tpos > 0, pltpu.roll(u, 1, 0), 0.0))

    @pl.when(use_next)
    def _():
        emit(jnp.where(tpos < period - 1, pltpu.roll(u, tm - 1, 0), 0.0))

    @pl.when(is_lat & (jc == 2))
    def _():
        first = (i * tm) % s == 0
        above = jnp.where(first, 0.0, mod(ha_ref[:, cs], ra_scr[...]))
        emit(jnp.concatenate([above, u[:tm - HALO]], axis=0))

    @pl.when(is_lat & (jc == 3))
    def _():
        lastt = ((i + 1) * tm) % s == 0
        below = jnp.where(lastt, 0.0, mod(hb_ref[:, cs], rb_scr[...]))
        emit(jnp.concatenate([u[HALO:], below], axis=0))


def _rwkv_prep(h, g, shift, scale, mu6, dims):
    r, d = h.shape
    s, c = dims["S"], dims["C"]
    tm = c
    assert tm % HALO == 0 and s % tm == 0 and tm & (tm - 1) == 0
    tc = d // 4
    hb = tm // HALO
    nblk = r // HALO
    n_lat_tiles = dims["RL"] // tm
    modrow = _mod_row_map(tm, n_lat_tiles, s, dims["B"])
    mod_spec = pl.BlockSpec((1, 1, tc), lambda i, j: (modrow(i), 0, j))
    vmem = 2 * tm * d * 4 + 4 * HALO * d * 4 + 2 * 6 * tm * tc * 2 + 12 * tm * tc * 4
    return pl.pallas_call(
        functools.partial(_rwkv_prep_kernel, tm=tm, tc=tc, n_lat_tiles=n_lat_tiles, s=s, c=c),
        grid=(r // tm, 4),
        in_specs=[pl.BlockSpec((tm, d), lambda i, j: (i, 0)),
                  pl.BlockSpec((HALO, d), lambda i, j: (jnp.maximum(i * hb - 1, 0), 0)),
                  pl.BlockSpec((HALO, d), lambda i, j: (jnp.minimum((i + 1) * hb, nblk - 1), 0)),
                  pl.BlockSpec((1, tc), lambda i, j: (0, j)),
                  mod_spec, mod_spec,
                  pl.BlockSpec((6, 1, tc), lambda i, j: (0, 0, j))],
        out_specs=pl.BlockSpec((6, tm, tc), lambda i, j: (0, i, j)),
        out_shape=jax.ShapeDtypeStruct((6, r, d), BF16),
        scratch_shapes=[pltpu.VMEM((tm, 1), F32), pltpu.VMEM((HALO, 1), F32), pltpu.VMEM((HALO, 1), F32)],
        compiler_params=_cparams(("parallel", "arbitrary"), vmem),
        name="rwkv_prep",
    )(h, h, h, g, shift, scale, mu6)


BMM_CAST_ROWS = 256


def _bmm_kernel(x_ref, w_ref, o_ref, w_scr):
    @pl.when(pl.program_id(2) == 0)
    def _():
        def body(c, carry):
            sl = pl.ds(pl.multiple_of(c * BMM_CAST_ROWS, BMM_CAST_ROWS), BMM_CAST_ROWS)
            w_scr[sl, :] = w_ref[0, sl, :].astype(w_scr.dtype)
            return carry

        lax.fori_loop(0, w_scr.shape[0] // BMM_CAST_ROWS, body, 0)

    o_ref[0] = _dot(x_ref[0], w_scr[...]).astype(o_ref.dtype)


def _bmm(x, w, lhs0, w0, ng, out_dtype, dims):
    _, k, n = w.shape
    r = x.shape[1]
    tm = _pow2_tile(512, dims["S"], dims["RC"])
    tn = _pow2_tile(1024, n)
    osz = jnp.dtype(out_dtype).itemsize
    wsz = jnp.dtype(w.dtype).itemsize
    assert k % BMM_CAST_ROWS == 0
    return pl.pallas_call(
        _bmm_kernel,
        grid=(ng, n // tn, r // tm),
        in_specs=[pl.BlockSpec((1, tm, k), lambda g, j, i: (lhs0 + g, i, 0)),
                  pl.BlockSpec((1, k, tn), lambda g, j, i: (w0 + g, 0, j))],
        out_specs=pl.BlockSpec((1, tm, tn), lambda g, j, i: (g, i, j)),
        out_shape=jax.ShapeDtypeStruct((ng, r, n), out_dtype),
        scratch_shapes=[pltpu.VMEM((k, tn), BF16)],
        compiler_params=_cparams(("parallel", "parallel", "arbitrary"),
                                 2 * tm * k * 2 + 2 * k * tn * wsz + k * tn * 2 + 2 * tm * tn * osz + tm * tn * 4),
        name="rwkv_bmm",
    )(x, w)


def _pair_ones():
    a = lax.broadcasted_iota(jnp.int32, (LANE, LANE), 0) // RWKV_N
    b = lax.broadcasted_iota(jnp.int32, (LANE, LANE), 1) // RWKV_N
    return (a == b).astype(BF16)


def _head_sums(x, ones):
    parts = [_dot(x[:, l:l + LANE].astype(BF16), ones) for l in range(0, x.shape[1], LANE)]
    return parts[0] if len(parts) == 1 else jnp.concatenate(parts, axis=1)


def _rwkv_mid_kernel(k_ref, hid_ref, w2_ref, a2_ref, g2_ref, w0_ref, a0_ref, kk_w_ref,
                     lw_ref, a_ref, kk_ref, gate_ref, *, lp):
    ones = _pair_ones()
    hw = hid_ref[0]
    ha = hid_ref[1]
    hg = hid_ref[2]
    for d in range(2):
        tw = jnp.tanh(hw[:, d * lp:(d + 1) * lp]).astype(BF16)
        z = w0_ref[d] + _dot(tw, w2_ref[d])
        lw_ref[d] = -math.exp(-0.5) * jax.nn.sigmoid(z)
        za = a0_ref[d] + _dot(ha[:, d * lp:(d + 1) * lp].astype(BF16), a2_ref[d])
        a_ref[d] = jax.nn.sigmoid(za).astype(a_ref.dtype)
    gate_ref[...] = _dot(jax.nn.sigmoid(hg).astype(BF16), g2_ref[...]).astype(gate_ref.dtype)
    kk = k_ref[0].astype(F32) * kk_w_ref[...]
    ss = _head_sums(kk * kk, ones)
    kk_ref[...] = (kk * lax.rsqrt(jnp.maximum(ss, 1e-24))).astype(kk_ref.dtype)


def _rwkv_mid(rkv, hid, w2, a2, g2, w0, a0, k_k, dims):
    _, r, d = rkv.shape
    lp = w2.shape[1]
    lw_tot = hid.shape[2]
    tm = _pow2_tile(256, dims["S"], dims["RC"])
    tc = _pow2_tile(1024, d)
    vmem = 2 * (tm * tc * 2 + 3 * tm * lw_tot * 4 + (4 * lp + g2.shape[0]) * tc * 2
                + tm * tc * (8 + 4 + 2 + 2)) + 10 * tm * tc * 4
    return pl.pallas_call(
        functools.partial(_rwkv_mid_kernel, lp=lp),
        grid=(r // tm, d // tc),
        in_specs=[pl.BlockSpec((1, tm, tc), lambda i, j: (1, i, j)),
                  pl.BlockSpec((3, tm, lw_tot), lambda i, j: (0, i, 0)),
                  pl.BlockSpec((2, lp, tc), lambda i, j: (0, 0, j)),
                  pl.BlockSpec((2, lp, tc), lambda i, j: (0, 0, j)),
                  pl.BlockSpec((g2.shape[0], tc), lambda i, j: (0, j)),
                  pl.BlockSpec((2, 1, tc), lambda i, j: (0, 0, j)),
                  pl.BlockSpec((2, 1, tc), lambda i, j: (0, 0, j)),
                  pl.BlockSpec((1, tc), lambda i, j: (0, j))],
        out_specs=[pl.BlockSpec((2, tm, tc), lambda i, j: (0, i, j)),
                   pl.BlockSpec((2, tm, tc), lambda i, j: (0, i, j)),
                   pl.BlockSpec((tm, tc), lambda i, j: (i, j)),
                   pl.BlockSpec((tm, tc), lambda i, j: (i, j))],
        out_shape=[jax.ShapeDtypeStruct((2, r, d), F32), jax.ShapeDtypeStruct((2, r, d), BF16),
                   jax.ShapeDtypeStruct((r, d), BF16), jax.ShapeDtypeStruct((r, d), BF16)],
        compiler_params=_cparams(("parallel", "arbitrary"), vmem),
        name="rwkv_mid",
    )(rkv, hid, w2, a2, g2, w0, a0, k_k)


def _rwkv_scan_kernel(r_ref, k_ref, v_ref, kk_ref, a_ref, lw_ref, ka_ref, y_ref, s_scr, *, rev, npairs):
    @pl.when(pl.program_id(2) == 0)
    def _():
        s_scr[...] = jnp.zeros_like(s_scr)

    ln = RWKV_CHUNK
    row = lax.broadcasted_iota(jnp.int32, (ln, LANE), 0)
    lane = lax.broadcasted_iota(jnp.int32, (ln, LANE), 1)
    col = lane % RWKV_N
    head0 = lane < RWKV_N
    strict = (col > row) if rev else (col < row)
    incl = (col >= row) if rev else (col <= row)
    eye = (col == row).astype(F32)

    def same_block(size):
        return (row // size) == (col // size)

    leaf = same_block(INV_LEAF)
    offs = [same_block(2 * sz) & jnp.logical_not(same_block(sz))
            for sz in (INV_LEAF << i for i in range(int(math.log2(ln // INV_LEAF))))]
    ti = lax.broadcasted_iota(jnp.int32, (ln, ln), 0)
    si = lax.broadcasted_iota(jnp.int32, (ln, ln), 1)
    tri = ((si >= ti) if rev else (si <= ti)).astype(BF16)
    ra = lax.broadcasted_iota(jnp.int32, (LANE, LANE), 0)
    rb = lax.broadcasted_iota(jnp.int32, (LANE, LANE), 1)
    same_head = (ra // RWKV_N) == (rb // RWKV_N)
    eye_full = ra == rb
    last = 0 if rev else ln - 1

    def bd(x):
        zero = jnp.zeros_like(x)
        return jnp.concatenate([jnp.where(head0, x, zero), jnp.where(head0, zero, x)], axis=0)

    def pm(a, b):
        return _dot(a.astype(BF16), bd(b.astype(BF16)))

    sls = [slice(p * LANE, (p + 1) * LANE) for p in range(npairs)]

    def prepare(sl):
        lw = lw_ref[0, :, sl]
        hi = lw.astype(BF16)
        lo = (lw - hi.astype(F32)).astype(BF16)
        cc = _dot(tri, jnp.concatenate([hi, lo], axis=1))
        c = cc[:, :LANE] + cc[:, LANE:]
        a = a_ref[0, :, sl].astype(F32)
        kk = kk_ref[:, sl].astype(F32)
        winv = jnp.exp(-c)
        kd = k_ref[0, :, sl].astype(F32) * (1.0 + (a - 1.0) * ka_ref[:, sl])
        at = (-(kk * jnp.exp(c - lw))).astype(BF16)
        bt = (kk * a * winv).astype(BF16)
        kt = (kd * winv).astype(BF16)
        rt = (r_ref[0, :, sl].astype(F32) * jnp.exp(c)).astype(BF16)
        ctot = jnp.sum(jnp.where(eye_full, c[last:last + 1, :], 0.0), axis=1, keepdims=True)
        return at, bt, kt, rt, jnp.exp(ctot)

    pairs = range(npairs)
    prep = [prepare(sl) for sl in sls]
    ar = [jnp.concatenate([at, rt], axis=0) for at, _, _, rt, _ in prep]
    res = [_dot_nt(lhs, jnp.concatenate([bd(bt), bd(kt)], axis=0)) for lhs, (_, bt, kt, _, _) in zip(ar, prep)]
    mab = [jnp.where(strict, x[:ln, :LANE], 0.0) for x in res]
    mrk = [jnp.concatenate([jnp.where(strict, x[:ln, LANE:], 0.0), jnp.where(incl, x[ln:, LANE:], 0.0)],
                           axis=0).astype(BF16) for x in res]
    rbm = [jnp.where(incl, x[ln:, :LANE], 0.0).astype(BF16) for x in res]

    md = [jnp.where(leaf, m, 0.0) for m in mab]
    sq = [pm(m, m) for m in md]
    tinv = [eye + m + pm(q, eye + m) for m, q in zip(md, sq)]
    for off in offs:
        half = [pm(t, jnp.where(off, m, 0.0)) for t, m in zip(tinv, mab)]
        tinv = [t + pm(hf, t) for t, hf in zip(tinv, half)]

    s0 = [s_scr[p] for p in pairs]
    s0b = [s.astype(BF16) for s in s0]
    xy = [_dot(lhs, sb) + _dot(mk, bd(v_ref[0, :, sl])) for lhs, sb, mk, sl in zip(ar, s0b, mrk, sls)]
    ub = [pm(t, q[:ln]).astype(BF16) for t, q in zip(tinv, xy)]
    for p in pairs:
        at, bt, kt, rt, wtot = prep[p]
        y = xy[p][ln:] + _dot(rbm[p], bd(ub[p]))
        y_ref[:, sls[p]] = y.astype(y_ref.dtype)
        upd = _dot_tn(jnp.concatenate([bt, kt], axis=0), jnp.concatenate([ub[p], v_ref[0, :, sls[p]]], axis=0))
        s_scr[p] = (s0[p] + jnp.where(same_head, upd, 0.0)) * wtot


def _rwkv_scan(rkv, kk, a, lw, k_a, dims, rev):
    _, r, d = rkv.shape
    npairs = min(32, d // LANE)
    pw = npairs * LANE
    ln = RWKV_CHUNK
    nc, _, rowblk = _chunk_maps(dims, ln, rev)
    dsel = 1 if rev else 0

    def spec3(lead):
        return pl.BlockSpec((1, ln, pw), lambda b, g, n: (lead, rowblk(b, n), g))

    return pl.pallas_call(
        functools.partial(_rwkv_scan_kernel, rev=rev, npairs=npairs),
        grid=(dims["B"], d // pw, nc),
        in_specs=[spec3(0), spec3(1), spec3(2),
                  pl.BlockSpec((ln, pw), lambda b, g, n: (rowblk(b, n), g)),
                  spec3(dsel), spec3(dsel),
                  pl.BlockSpec((1, pw), lambda b, g, n: (0, g))],
        out_specs=pl.BlockSpec((ln, pw), lambda b, g, n: (rowblk(b, n), g)),
        out_shape=jax.ShapeDtypeStruct((r, d), BF16),
        scratch_shapes=[pltpu.VMEM((npairs, LANE, LANE), F32)],
        compiler_params=_cparams(("parallel", "parallel", "arbitrary"),
                                 2 * ln * pw * (5 * 2 + 4 + 2) + npairs * LANE * LANE * 4 + (8 << 20)),
        name="rwkv_scan_bwd" if rev else "rwkv_scan_fwd",
    )(rkv, rkv, rkv, kk, a, lw, k_a)


def _rwkv_finish_kernel(yf_ref, yb_ref, r_ref, k_ref, v_ref, a_ref, gate_ref, ka_ref, rk_ref, lw_ref, lb_ref, z_ref):
    ones = _pair_ones()
    inv_n = 1.0 / RWKV_N
    y = yf_ref[...].astype(F32) + yb_ref[...].astype(F32)
    mean = _head_sums(y, ones) * inv_n
    yc = y - mean
    var = _head_sums(yc * yc, ones) * inv_n
    yn = yc * lax.rsqrt(var + LNX_EPS) * lw_ref[...] + lb_ref[...]
    k = k_ref[0].astype(F32)
    ka = ka_ref[...]
    kd0 = k * (1.0 + (a_ref[0].astype(F32) - 1.0) * ka)
    kd1 = k * (1.0 + (a_ref[1].astype(F32) - 1.0) * ka)
    kb = 0.5 * (kd0 + kd1)
    bonus = _head_sums(r_ref[0].astype(F32) * kb * rk_ref[...], ones) * v_ref[0].astype(F32)
    z_ref[...] = ((yn + bonus) * gate_ref[...].astype(F32)).astype(z_ref.dtype)


def _rwkv_finish(yf, yb, rkv, a, gate, k_a, r_k, lnx_w, lnx_b, dims, rows):
    d = yf.shape[1]
    tm = _pow2_tile(256, dims["S"], dims["RC"])
    tc = _pow2_tile(512, d)
    blk = pl.BlockSpec((tm, tc), lambda i, j: (i, j))
    vec = pl.BlockSpec((1, tc), lambda i, j: (0, j))

    def lead(g):
        return pl.BlockSpec((1, tm, tc), lambda i, j: (g, i, j))

    return pl.pallas_call(
        _rwkv_finish_kernel,
        grid=(rows // tm, d // tc),
        in_specs=[blk, blk, lead(0), lead(1), lead(2),
                  pl.BlockSpec((2, tm, tc), lambda i, j: (0, i, j)), blk, vec, vec, vec, vec],
        out_specs=blk,
        out_shape=jax.ShapeDtypeStruct((rows, d), BF16),
        compiler_params=_cparams(("parallel", "parallel"), 2 * tm * tc * 2 * 9 + 16 * tm * tc * 4),
        name="rwkv_finish",
    )(yf, yb, rkv, rkv, rkv, a, gate, k_a, r_k, lnx_w, lnx_b)


def _even_layer(h, mod, norm_g, w_in, b_gate, conv_qk, gn_ret, gn_ml, w_out, dims, rows_out):
    d = h.shape[1]
    heads, nqk, nv = dims["H"], dims["NQK"], dims["NV"]
    n_main = 4 * nqk + 4 * nv
    n_gate = 4 * heads
    w_gate = jnp.pad(w_in[:, n_main:], ((0, 0), (0, LANE - n_gate)))
    bg = jnp.pad(b_gate, (0, LANE - n_gate)).reshape(1, LANE)
    shift, scale, gate = mod[0], mod[1], mod[2]
    p, gates = _inproj(h, norm_g[0].reshape(1, d), shift, scale, w_in, n_main, w_gate, bg, dims)
    post = jnp.concatenate([jnp.ones((nqk,), F32), jnp.full((nqk,), HEAD_DK ** -0.5, F32)]).reshape(1, 2 * nqk)
    qk = _conv_qk(p, conv_qk, post, dims)
    cos, sin = _rope_tables(dims)
    yrf = _retention(p, cos, sin, dims, False)
    yrb = _retention(p, cos, sin, dims, True)
    ymf = _mlstm(qk, p, gates, dims, False)
    ymb = _mlstm(qk, p, gates, dims, True)
    z = _merge(yrf, yrb, ymf, ymb, p, gn_ret.reshape(1, nv), gn_ml.reshape(1, nv), dims, rows_out)
    return _outproj(z, w_out, h, gate, norm_g[1].reshape(1, d), dims, rows_out)


def _rwkv_layer(h, mod, norm_g, mu, w_rkv, w_rkv0, w0, w1, w2, a0, a1, a2, g1, g2, k_k, k_a, r_k, lnx_w, lnx_b,
                w_out, dims, rows_out):
    d = h.shape[1]
    shift, scale, gate = mod[0], mod[1], mod[2]
    order = jnp.array([0, 2, 3, 1, 4, 5])
    mu6 = mu[order].reshape(6, 1, d)
    x6 = _rwkv_prep(h, norm_g[0].reshape(1, d), shift, scale, mu6, dims)
    rkv = _bmm(x6, w_rkv, 0, w_rkv0, 3, BF16, dims)
    lora = w1.shape[2]
    glora = g1.shape[1]
    lp = _round_up(lora, LANE)
    gp = _round_up(glora, LANE)
    lw_tot = max(2 * lp, gp)

    def two(w):
        w = jnp.pad(w, ((0, 0), (0, 0), (0, lp - lora)))
        w = jnp.concatenate([w[0], w[1]], axis=1)
        return jnp.pad(w, ((0, 0), (0, lw_tot - 2 * lp)))

    w_l1 = jnp.stack([two(w1), two(a1), jnp.pad(g1, ((0, 0), (0, lw_tot - glora)))]).astype(BF16)
    hid = _bmm(x6, w_l1, 3, 0, 3, F32, dims)
    pad_rows = lambda w, n: jnp.pad(w, ((0, 0),) * (w.ndim - 2) + ((0, n - w.shape[-2]), (0, 0))).astype(BF16)
    lw, a, kk, gmul = _rwkv_mid(rkv, hid, pad_rows(w2, lp), pad_rows(a2, lp), pad_rows(g2, lw_tot),
                                w0.reshape(2, 1, d), a0.reshape(2, 1, d), k_k.reshape(1, d), dims)
    ka = k_a.reshape(1, d)
    yf = _rwkv_scan(rkv, kk, a, lw, ka, dims, False)
    yb = _rwkv_scan(rkv, kk, a, lw, ka, dims, True)
    z = _rwkv_finish(yf, yb, rkv, a, gmul, ka, r_k.reshape(1, d), lnx_w.reshape(1, d), lnx_b.reshape(1, d),
                     dims, rows_out)
    return _outproj(z, w_out, h, gate, norm_g[1].reshape(1, d), dims, rows_out)


def kernel(x, c, ctx, c_ctx, ada_w, ada_b, norm_g, mlp_w_in, mlp_w_out, ev_w_in, ev_b_gate, ev_conv_qk,
           ev_gn_ret, ev_gn_mlstm, ev_w_out, od_mu, od_w_rkv, od_w0, od_w1, od_w2, od_a0, od_a1, od_a2,
           od_g1, od_g2, od_k_k, od_k_a, od_r_k, od_lnx_w, od_lnx_b, od_w_out):
    b, s, d = x.shape
    cl = ctx.shape[1]
    depth = ada_w.shape[0]
    heads = d // (2 * HEAD_DV)
    assert b + 1 <= MOD_ROWS and s % CHUNK == 0 and cl % CHUNK == 0 and s % GRID_W == 0
    dims = dict(B=b, S=s, C=cl, RL=b * s, RC=b * cl, H=heads, NQK=heads * HEAD_DK, NV=heads * HEAD_DV)
    rl, r = dims["RL"], dims["RL"] + dims["RC"]

    cond = jnp.concatenate([c, c_ctx[None, :], jnp.zeros((MOD_ROWS - b - 1, d), F32)], axis=0)
    tables = _ada_tables(cond, ada_w, ada_b)
    h = jnp.concatenate([x.reshape(rl, d), ctx.reshape(b * cl, d)], axis=0)

    for layer in range(depth):
        with_ctx = layer < depth - 1
        rows_out = r if with_ctx else rl
        mod = tables[layer].reshape(MOD_ROWS, 6, 1, d).transpose(1, 0, 2, 3)
        g = norm_g[layer]
        if layer % 2 == 0:
            e = layer // 2
            h_mix = _even_layer(h, mod[0:3], g[0:2], ev_w_in[e].astype(BF16), ev_b_gate[e], ev_conv_qk[e],
                                ev_gn_ret[e], ev_gn_mlstm[e], _to_bf16(ev_w_out, e), dims, rows_out)
        else:
            o = layer // 2
            w_rkv_all = od_w_rkv.reshape((-1,) + od_w_rkv.shape[2:])
            h_mix = _rwkv_layer(h, mod[0:3], g[0:2], od_mu[o], w_rkv_all, 3 * o, od_w0[o], od_w1[o], od_w2[o],
                                od_a0[o], od_a1[o], od_a2[o], od_g1[o], od_g2[o], od_k_k[o], od_k_a[o], od_r_k[o],
                                od_lnx_w[o], od_lnx_b[o], _to_bf16(od_w_out, o), dims, rows_out)
        h = _mlp(h_mix, g[2].reshape(1, d), mod[3], mod[4], _to_bf16(mlp_w_in, layer),
                 _to_bf16(mlp_w_out, layer), mod[5], g[3].reshape(1, d), dims, rows_out)
    return h[:rl].reshape(b, s, d)
```

```python
import functools
import math

import jax
import jax.numpy as jnp
from jax import lax
from jax.experimental import pallas as pl
from jax.experimental.pallas import tpu as pltpu

F32 = jnp.float32
BF16 = jnp.bfloat16

EPS = 1e-6
LNX_EPS = 64e-5
ROPE_BASE = 10000.0
GRID_W = 64
CHUNK = 128
RWKV_CHUNK = 64
HEAD_DK = 128
HEAD_DV = 256
RWKV_N = 64
INV_LEAF = 4
CONV_K = 3
LANE = 128
MOD_ROWS = 8
VMEM_BYTES = 64 * 1024 * 1024

NT_DIMS = (((1,), (1,)), ((), ()))
TN_DIMS = (((0,), (0,)), ((), ()))


def _dot(a, b):
    return jnp.dot(a, b, preferred_element_type=F32)


def _dot_nt(a, b):
    return lax.dot_general(a, b, NT_DIMS, preferred_element_type=F32)


def _dot_tn(a, b):
    return lax.dot_general(a, b, TN_DIMS, preferred_element_type=F32)


def _split_dot(m, x):
    hi = x.astype(BF16)
    lo = (x - hi.astype(F32)).astype(BF16)
    return _dot(m, hi) + _dot(m, lo)


def _cparams(semantics, vmem_bytes):
    limit = int(min(max(vmem_bytes * 5 // 4 + (4 << 20), 16 << 20), VMEM_BYTES - (6 << 20)))
    return pltpu.CompilerParams(dimension_semantics=semantics, vmem_limit_bytes=limit)


def _pow2_tile(pref, *dims):
    t = pref
    while any(d % t for d in dims):
        t //= 2
    return t


def _round_up(x, m):
    return (x + m - 1) // m * m


CAST_BLOCK_ELEMS = 1 << 21


def _cast_kernel(x_ref, o_ref):
    o_ref[...] = x_ref[...].astype(o_ref.dtype)


def _to_bf16(w, lead, ncols=None):
    shape = w.shape[1:]
    w3 = w.reshape(w.shape[0], -1, shape[-1])
    rows = w3.shape[1]
    ncols = shape[-1] if ncols is None else ncols
    tc = _pow2_tile(4096, ncols)
    tr = _pow2_tile(max(CAST_BLOCK_ELEMS // tc, 16), rows)
    out = pl.pallas_call(
        _cast_kernel,
        grid=(rows // tr, ncols // tc),
        in_specs=[pl.BlockSpec((None, tr, tc), lambda i, j: (lead, i, j))],
        out_specs=pl.BlockSpec((tr, tc), lambda i, j: (i, j)),
        out_shape=jax.ShapeDtypeStruct((rows, ncols), BF16),
        compiler_params=_cparams(("parallel", "parallel"), 2 * tr * tc * 6),
        name="weight_cast",
    )(w3)
    return out.reshape(shape[:-1] + (ncols,))


def _ada_kernel(c_ref, w_ref, b_ref, o_ref):
    c = c_ref[...]
    s = c * jax.nn.sigmoid(c)
    o_ref[0] = _dot(s.astype(BF16), w_ref[0].astype(BF16)) + b_ref[0]


def _ada_tables(cond, ada_w, ada_b):
    depth, d, n = ada_w.shape
    tn = _pow2_tile(512, n)
    return pl.pallas_call(
        _ada_kernel,
        grid=(depth, n // tn),
        in_specs=[pl.BlockSpec((MOD_ROWS, d), lambda l, j: (0, 0)),
                  pl.BlockSpec((1, d, tn), lambda l, j: (l, 0, j)),
                  pl.BlockSpec((1, 1, tn), lambda l, j: (l, 0, j))],
        out_specs=pl.BlockSpec((1, MOD_ROWS, tn), lambda l, j: (l, 0, j)),
        out_shape=jax.ShapeDtypeStruct((depth, MOD_ROWS, n), F32),
        compiler_params=_cparams(("parallel", "parallel"), 2 * d * tn * 4 + d * tn * 2),
        name="ada_tables",
    )(cond, ada_w, ada_b.reshape(depth, 1, n))


ROW_BLOCK = 32
NORM_COLS = 512


def _col_slices(d):
    step = min(NORM_COLS, d)
    return [slice(c, c + step) for c in range(0, d, step)]


def _row_rstd(x_ref, sl, cols, d):
    acc = None
    for c in cols:
        x = x_ref[sl, c]
        acc = x * x if acc is None else acc + x * x
    return lax.rsqrt(jnp.sum(acc, axis=-1, keepdims=True) / d + EPS)


def _modulate_rows(h_ref, g_ref, sh_ref, sc_ref, u_scr, tm):
    d = h_ref.shape[1]
    cols = _col_slices(d)

    def body(r, carry):
        sl = pl.ds(pl.multiple_of(r * ROW_BLOCK, ROW_BLOCK), ROW_BLOCK)
        rs = _row_rstd(h_ref, sl, cols, d)
        for c in cols:
            u = (h_ref[sl, c] * rs * g_ref[:, c]) * (1.0 + sc_ref[0, :, c]) + sh_ref[0, :, c]
            u_scr[sl, c] = u.astype(u_scr.dtype)
        return carry

    lax.fori_loop(0, tm // ROW_BLOCK, body, 0)


def _norm_residual_rows(acc_ref, h_ref, gate_ref, g_ref, o_ref, tm):
    d = acc_ref.shape[1]
    cols = _col_slices(d)

    def body(r, carry):
        sl = pl.ds(pl.multiple_of(r * ROW_BLOCK, ROW_BLOCK), ROW_BLOCK)
        rs = _row_rstd(acc_ref, sl, cols, d)
        for c in cols:
            o_ref[sl, c] = h_ref[sl, c] + gate_ref[0, :, c] * (acc_ref[sl, c] * rs * g_ref[:, c])
        return carry

    lax.fori_loop(0, tm // ROW_BLOCK, body, 0)


def _mod_row_map(tm, n_lat_tiles, s, b):
    def f(i):
        return jnp.where(i < n_lat_tiles, (i * tm) // s, b)
    return f


def _inproj_kernel(h_ref, g_ref, sh_ref, sc_ref, w_ref, wg_ref, bg_ref, p_ref, gt_ref, u_scr, *, tm):
    @pl.when(pl.program_id(1) == 0)
    def _():
        _modulate_rows(h_ref, g_ref, sh_ref, sc_ref, u_scr, tm)
        gt_ref[...] = _dot(u_scr[...], wg_ref[...]) + bg_ref[...]

    p_ref[...] = _dot(u_scr[...], w_ref[...]).astype(p_ref.dtype)


def _inproj(h, g, shift, scale, w, n, wg, bg, dims):
    r, d = h.shape
    tm = _pow2_tile(512, dims["S"], dims["RC"])
    tn = _pow2_tile(1024, n)
    modrow = _mod_row_map(tm, dims["RL"] // tm, dims["S"], dims["B"])
    vmem = 2 * tm * d * 4 + tm * d * 2 + 2 * d * tn * 2 + 2 * tm * tn * 2 + 2 * d * LANE * 2 + 2 * tm * LANE * 4
    return pl.pallas_call(
        functools.partial(_inproj_kernel, tm=tm),
        grid=(r // tm, n // tn),
        in_specs=[pl.BlockSpec((tm, d), lambda i, j: (i, 0)),
                  pl.BlockSpec((1, d), lambda i, j: (0, 0)),
                  pl.BlockSpec((1, 1, d), lambda i, j: (modrow(i), 0, 0)),
                  pl.BlockSpec((1, 1, d), lambda i, j: (modrow(i), 0, 0)),
                  pl.BlockSpec((d, tn), lambda i, j: (0, j)),
                  pl.BlockSpec((d, LANE), lambda i, j: (0, 0)),
                  pl.BlockSpec((1, LANE), lambda i, j: (0, 0))],
        out_specs=[pl.BlockSpec((tm, tn), lambda i, j: (i, j)),
                   pl.BlockSpec((tm, LANE), lambda i, j: (i, 0))],
        out_shape=[jax.ShapeDtypeStruct((r, n), BF16), jax.ShapeDtypeStruct((r, LANE), F32)],
        scratch_shapes=[pltpu.VMEM((tm, d), BF16)],
        compiler_params=_cparams(("parallel", "arbitrary"), vmem),
        name="even_inproj",
    )(h, g, shift, scale, w, wg, bg)


def _conv_kernel(x_ref, w_ref, post_ref, o_ref, *, rows, grid_mode):
    x = x_ref[...].astype(F32)
    t = lax.broadcasted_iota(jnp.int32, x.shape, 0)
    w = w_ref[...]
    if grid_mode:
        col = t % GRID_W
        has_prev, has_next = col > 0, col < GRID_W - 1
    else:
        has_prev, has_next = t > 0, t < rows - 1
    xl = jnp.where(has_prev, pltpu.roll(x, 1, 0), 0.0)
    xr = jnp.where(has_next, pltpu.roll(x, rows - 1, 0), 0.0)

    def tap_row(dr):
        return xl * w[3 * dr:3 * dr + 1] + x * w[3 * dr + 1:3 * dr + 2] + xr * w[3 * dr + 2:3 * dr + 3]

    y = tap_row(1)
    if grid_mode:
        z = jnp.zeros((GRID_W, x.shape[1]), F32)
        y = y + jnp.concatenate([z, tap_row(0)[:rows - GRID_W]], axis=0)
        y = y + jnp.concatenate([tap_row(2)[GRID_W:], z], axis=0)
    y = y * jax.nn.sigmoid(y)
    o_ref[...] = (y * post_ref[...]).astype(o_ref.dtype)


def _conv_qk(p, conv_w, post, dims):
    b, s, c = dims["B"], dims["S"], dims["C"]
    nqk = dims["NQK"]
    ch = 2 * nqk
    tc = LANE
    col0 = 6 * nqk // tc
    w9 = conv_w.reshape(CONV_K * CONV_K, ch)

    def call(rows, blk0, grid_mode):
        return pl.pallas_call(
            functools.partial(_conv_kernel, rows=rows, grid_mode=grid_mode),
            grid=(b, ch // tc),
            in_specs=[pl.BlockSpec((rows, tc), lambda i, j: (blk0 + i, col0 + j)),
                      pl.BlockSpec((CONV_K * CONV_K, tc), lambda i, j: (0, j)),
                      pl.BlockSpec((1, tc), lambda i, j: (0, j))],
            out_specs=pl.BlockSpec((rows, tc), lambda i, j: (i, j)),
            out_shape=jax.ShapeDtypeStruct((b * rows, ch), BF16),
            compiler_params=_cparams(("parallel", "parallel"), 16 * rows * tc * 4),
            name="conv_lat" if grid_mode else "conv_ctx",
        )(p, w9, post)

    return jnp.concatenate([call(s, 0, True), call(c, (b * s) // c, False)], axis=0)


def _chunk_maps(dims, chunk, rev):
    b_, s, c = dims["B"], dims["S"], dims["C"]
    ncc, ncl = c // chunk, s // chunk
    nc = ncc + ncl

    def pos(n):
        if not rev:
            return n
        return jnp.where(n < ncc, ncc - 1 - n, ncc + nc - 1 - n)

    def rowblk(b, n):
        q = pos(n)
        return jnp.where(q < ncc, (b_ * s) // chunk + b * ncc + q, b * ncl + q - ncc)

    return nc, pos, rowblk


def _retention_kernel(q_ref, k_ref, v_ref, cos_ref, sin_ref, intra_ref, qd_ref, kd_ref, cd_ref, y_ref, s_scr,
                      *, heads):
    @pl.when(pl.program_id(1) == 0)
    def _():
        s_scr[...] = jnp.zeros_like(s_scr)

    cos = cos_ref[...]
    sin = sin_ref[...]
    lane = lax.broadcasted_iota(jnp.int32, cos.shape, 1)
    first = (lane % (HEAD_DK // 2)) < (HEAD_DK // 4)

    def rope(x):
        swapped = jnp.where(first, pltpu.roll(x, HEAD_DK - HEAD_DK // 4, 1), pltpu.roll(x, HEAD_DK // 4, 1))
        return x * cos + swapped * sin

    for h in range(heads):
        ks = slice(h * HEAD_DK, (h + 1) * HEAD_DK)
        vs = slice(h * HEAD_DV, (h + 1) * HEAD_DV)
        q = rope(q_ref[:, ks].astype(F32))
        k = rope(k_ref[:, ks].astype(F32)) * HEAD_DK ** -0.5
        v = v_ref[:, vs]
        s = s_scr[h]
        sc = _dot_nt(q.astype(BF16), k.astype(BF16)) * intra_ref[h]
        o = _dot(sc.astype(BF16), v) + _dot((q * qd_ref[h]).astype(BF16), s.astype(BF16))
        s_scr[h] = s * cd_ref[h] + _dot_tn((k * kd_ref[h]).astype(BF16), v)
        y_ref[:, vs] = o.astype(y_ref.dtype)


def _retention_tables(heads, rev):
    h = jnp.arange(heads, dtype=F32) / max(heads - 1, 1)
    lg = jnp.log1p(-jnp.exp2(-(5.0 + 7.0 * h)))
    if rev:
        lg = lg[::-1]
    pos = jnp.arange(CHUNK, dtype=F32)
    diff = pos[:, None] - pos[None, :]
    if rev:
        diff = -diff
    intra = jnp.where(diff >= 0, jnp.exp(jnp.maximum(diff, 0.0)[None] * lg[:, None, None]), 0.0)
    q_pow = (CHUNK - pos) if rev else (pos + 1.0)
    k_pow = pos if rev else (CHUNK - 1.0 - pos)
    qd = jnp.exp(q_pow[None, :] * lg[:, None])[..., None]
    kd = jnp.exp(k_pow[None, :] * lg[:, None])[..., None]
    cd = jnp.exp(CHUNK * lg)[:, None, None]
    return intra, qd, kd, cd


def _retention(p, cos, sin, dims, rev):
    heads, nqk, nv = dims["H"], dims["NQK"], dims["NV"]
    r = p.shape[0]
    nc, pos, rowblk = _chunk_maps(dims, CHUNK, rev)
    intra, qd, kd, cd = _retention_tables(heads, rev)
    full = lambda shape: pl.BlockSpec(shape, lambda b, n: (0,) * len(shape))
    vmem = (2 * CHUNK * (2 * nqk + 2 * nv) * 2 + 2 * heads * CHUNK * (CHUNK + 2 * LANE) * 4
            + heads * HEAD_DK * HEAD_DV * 4)
    return pl.pallas_call(
        functools.partial(_retention_kernel, heads=heads),
        grid=(dims["B"], nc),
        in_specs=[pl.BlockSpec((CHUNK, nqk), lambda b, n: (rowblk(b, n), 0)),
                  pl.BlockSpec((CHUNK, nqk), lambda b, n: (rowblk(b, n), 1)),
                  pl.BlockSpec((CHUNK, nv), lambda b, n: (rowblk(b, n), 1)),
                  pl.BlockSpec((CHUNK, HEAD_DK), lambda b, n: (pos(n), 0)),
                  pl.BlockSpec((CHUNK, HEAD_DK), lambda b, n: (pos(n), 0)),
                  full((heads, CHUNK, CHUNK)), full((heads, CHUNK, 1)), full((heads, CHUNK, 1)),
                  full((heads, 1, 1))],
        out_specs=pl.BlockSpec((CHUNK, nv), lambda b, n: (rowblk(b, n), 0)),
        out_shape=jax.ShapeDtypeStruct((r, nv), BF16),
        scratch_shapes=[pltpu.VMEM((heads, HEAD_DK, HEAD_DV), F32)],
        compiler_params=_cparams(("parallel", "arbitrary"), vmem),
        name="retention_bwd" if rev else "retention_fwd",
    )(p, p, p, cos, sin, intra, qd, kd, cd)


def _rope_tables(dims):
    s, c = dims["S"], dims["C"]
    quarter = HEAD_DK // 4
    inv = jnp.power(ROPE_BASE, -jnp.arange(quarter, dtype=F32) / quarter)
    t = jnp.arange(s)
    rows = (t // GRID_W).astype(F32)
    cols = (t % GRID_W).astype(F32)
    ar = rows[:, None] * inv[None, :]
    ac = cols[:, None] * inv[None, :]
    cos = jnp.concatenate([jnp.cos(ar), jnp.cos(ar), jnp.cos(ac), jnp.cos(ac)], axis=-1)
    sin = jnp.concatenate([-jnp.sin(ar), jnp.sin(ar), -jnp.sin(ac), jnp.sin(ac)], axis=-1)
    cos = jnp.concatenate([jnp.ones((c, HEAD_DK), F32), cos], axis=0)
    sin = jnp.concatenate([jnp.zeros((c, HEAD_DK), F32), sin], axis=0)
    return cos, sin


def _mlstm_kernel(q_ref, k_ref, v_ref, g_ref, y_ref, c_scr, n_scr, m_scr, *, heads, rev):
    @pl.when(pl.program_id(1) == 0)
    def _():
        c_scr[...] = jnp.zeros_like(c_scr)
        n_scr[...] = jnp.zeros_like(n_scr)
        m_scr[...] = jnp.zeros_like(m_scr)

    ti = lax.broadcasted_iota(jnp.int32, (CHUNK, CHUNK), 0)
    si = lax.broadcasted_iota(jnp.int32, (CHUNK, CHUNK), 1)
    causal = (si >= ti) if rev else (si <= ti)
    tri = causal.astype(BF16)
    g = g_ref[...]
    logf = jnp.minimum(g, 0.0) - jnp.log1p(jnp.exp(-jnp.abs(g)))
    cs = _split_dot(tri, logf)
    g_t = g.T
    cs_t = cs.T
    icol0 = 2 * heads if rev else 0
    fcol0 = icol0 + heads
    last = 0 if rev else CHUNK - 1

    for h in range(heads):
        ks = slice(h * HEAD_DK, (h + 1) * HEAD_DK)
        vs = slice(h * HEAD_DV, (h + 1) * HEAD_DV)
        ic, fc = icol0 + h, fcol0 + h
        b_col, b_row = cs[:, fc:fc + 1], cs_t[fc:fc + 1, :]
        i_col, i_row = g[:, ic:ic + 1], g_t[ic:ic + 1, :]
        b_last = cs[last:last + 1, fc:fc + 1]
        m = m_scr[h]
        q = q_ref[:, ks]
        k = k_ref[:, ks]
        v = v_ref[:, vs]
        c_mem = c_scr[h]
        n_mem = n_scr[h]

        d_log = jnp.where(causal, b_col - b_row + i_row, -jnp.inf)
        m_inter = b_col + m
        m_row = jnp.maximum(jnp.max(d_log, axis=-1, keepdims=True), m_inter)
        w_intra = jnp.exp(d_log - m_row)
        w_inter = jnp.exp(m_inter - m_row)
        sc = _dot_nt(q, k) * w_intra
        num = _dot(sc.astype(BF16), v) + w_inter * _dot(q, c_mem.astype(BF16))
        den = jnp.sum(sc, axis=-1, keepdims=True) + w_inter * jnp.sum(q.astype(F32) * n_mem, axis=-1, keepdims=True)
        y_ref[:, vs] = (num / jnp.maximum(jnp.abs(den), jnp.exp(-m_row))).astype(y_ref.dtype)

        gk = b_last - b_col + i_col
        m_new = jnp.maximum(b_last + m, jnp.max(gk, axis=0, keepdims=True))
        w_k = jnp.exp(gk - m_new)
        w_c = jnp.exp(b_last + m - m_new)
        kw = k.astype(F32) * w_k
        c_scr[h] = w_c * c_mem + _dot_tn(kw.astype(BF16), v)
        n_scr[h] = w_c * n_mem + jnp.sum(kw, axis=0, keepdims=True)
        m_scr[h] = m_new


def _mlstm(qk, p, gates, dims, rev):
    heads, nqk, nv = dims["H"], dims["NQK"], dims["NV"]
    r = p.shape[0]
    nc, _, rowblk = _chunk_maps(dims, CHUNK, rev)
    vmem = 2 * CHUNK * (2 * nqk + 2 * nv) * 2 + 2 * CHUNK * LANE * 4 + heads * HEAD_DK * (HEAD_DV + 8) * 4
    return pl.pallas_call(
        functools.partial(_mlstm_kernel, heads=heads, rev=rev),
        grid=(dims["B"], nc),
        in_specs=[pl.BlockSpec((CHUNK, nqk), lambda b, n: (rowblk(b, n), 0)),
                  pl.BlockSpec((CHUNK, nqk), lambda b, n: (rowblk(b, n), 1)),
                  pl.BlockSpec((CHUNK, nv), lambda b, n: (rowblk(b, n), 4)),
                  pl.BlockSpec((CHUNK, LANE), lambda b, n: (rowblk(b, n), 0))],
        out_specs=pl.BlockSpec((CHUNK, nv), lambda b, n: (rowblk(b, n), 0)),
        out_shape=jax.ShapeDtypeStruct((r, nv), BF16),
        scratch_shapes=[pltpu.VMEM((heads, HEAD_DK, HEAD_DV), F32),
                        pltpu.VMEM((heads, 1, HEAD_DK), F32),
                        pltpu.VMEM((heads, 1, 1), F32)],
        compiler_params=_cparams(("parallel", "arbitrary"), vmem),
        name="mlstm_bwd" if rev else "mlstm_fwd",
    )(qk, qk, p, gates)


def _merge_kernel(yrf_ref, yrb_ref, ymf_ref, ymb_ref, ag_ref, bo_ref, gr_ref, gm_ref, z_ref, *, heads):
    nv = heads * HEAD_DV
    for h in range(heads):
        vs = slice(h * HEAD_DV, (h + 1) * HEAD_DV)
        yr = yrf_ref[:, vs].astype(F32) + yrb_ref[:, vs].astype(F32)
        yr = yr * lax.rsqrt(jnp.mean(yr * yr, axis=-1, keepdims=True) + EPS) * gr_ref[:, vs]
        ag = ag_ref[:, vs].astype(F32)
        z_ref[:, vs] = (yr * (ag * jax.nn.sigmoid(ag))).astype(z_ref.dtype)
        ym = ymf_ref[:, vs].astype(F32) + ymb_ref[:, vs].astype(F32)
        ym = ym * lax.rsqrt(jnp.mean(ym * ym, axis=-1, keepdims=True) + EPS) * gm_ref[:, vs]
        bo = bo_ref[:, vs].astype(F32)
        z_ref[:, nv + h * HEAD_DV:nv + (h + 1) * HEAD_DV] = (ym * jax.nn.sigmoid(bo)).astype(z_ref.dtype)


def _merge(yrf, yrb, ymf, ymb, p, gn_ret, gn_ml, dims, rows):
    heads, nv = dims["H"], dims["NV"]
    tm = _pow2_tile(256, dims["S"], dims["RC"])
    ys = pl.BlockSpec((tm, nv), lambda i: (i, 0))
    return pl.pallas_call(
        functools.partial(_merge_kernel, heads=heads),
        grid=(rows // tm,),
        in_specs=[ys, ys, ys, ys,
                  pl.BlockSpec((tm, nv), lambda i: (i, 2)),
                  pl.BlockSpec((tm, nv), lambda i: (i, 5)),
                  pl.BlockSpec((1, nv), lambda i: (0, 0)),
                  pl.BlockSpec((1, nv), lambda i: (0, 0))],
        out_specs=pl.BlockSpec((tm, 2 * nv), lambda i: (i, 0)),
        out_shape=jax.ShapeDtypeStruct((rows, 2 * nv), BF16),
        compiler_params=_cparams(("parallel",), 2 * tm * nv * 2 * 8 + 8 * tm * nv * 4),
        name="even_merge",
    )(yrf, yrb, ymf, ymb, p, p, gn_ret, gn_ml)


def _outproj_kernel(z_ref, w_ref, h_ref, gate_ref, g_ref, o_ref, *, tm, tn):
    j = pl.program_id(1)
    o_ref[:, pl.ds(pl.multiple_of(j * tn, tn), tn)] = _dot(z_ref[...], w_ref[...])

    @pl.when(j == pl.num_programs(1) - 1)
    def _():
        _norm_residual_rows(o_ref, h_ref, gate_ref, g_ref, o_ref, tm)


def _outproj(z, w, h, gate, g, dims, rows):
    k, n = w.shape
    tm = _pow2_tile(512, dims["S"], dims["RC"])
    tn = _pow2_tile(512, n)
    modrow = _mod_row_map(tm, dims["RL"] // tm, dims["S"], dims["B"])
    vmem = 2 * tm * k * 2 + 2 * k * tn * 2 + tm * n * 4 * 4 + 2 * tm * tn * 4
    return pl.pallas_call(
        functools.partial(_outproj_kernel, tm=tm, tn=tn),
        grid=(rows // tm, n // tn),
        in_specs=[pl.BlockSpec((tm, k), lambda i, j: (i, 0)),
                  pl.BlockSpec((k, tn), lambda i, j: (0, j)),
                  pl.BlockSpec((tm, n), lambda i, j: (i, 0)),
                  pl.BlockSpec((1, 1, n), lambda i, j: (modrow(i), 0, 0)),
                  pl.BlockSpec((1, n), lambda i, j: (0, 0))],
        out_specs=pl.BlockSpec((tm, n), lambda i, j: (i, 0)),
        out_shape=jax.ShapeDtypeStruct((rows, n), F32),
        compiler_params=_cparams(("parallel", "arbitrary"), vmem),
        name="outproj_norm_residual",
    )(z, w, h, gate, g)


def _mlp_kernel(h_ref, gpre_ref, sh_ref, sc_ref, w1_ref, w2_ref, gate_ref, gpost_ref, o_ref, u_scr, *, tm):
    f = pl.program_id(1)

    @pl.when(f == 0)
    def _():
        _modulate_rows(h_ref, gpre_ref, sh_ref, sc_ref, u_scr, tm)
        o_ref[...] = jnp.zeros_like(o_ref)

    a = jnp.maximum(_dot(u_scr[...], w1_ref[...]), 0.0)
    o_ref[...] += _dot((a * a).astype(BF16), w2_ref[...])

    @pl.when(f == pl.num_programs(1) - 1)
    def _():
        _norm_residual_rows(o_ref, h_ref, gate_ref, gpost_ref, o_ref, tm)


def _mlp(h, g_pre, shift, scale, w1, w2, gate, g_post, dims, rows):
    d, ff = w1.shape
    tm = _pow2_tile(512, dims["S"], dims["RC"])
    tf = _pow2_tile(512, ff)
    modrow = _mod_row_map(tm, dims["RL"] // tm, dims["S"], dims["B"])
    mod_spec = pl.BlockSpec((1, 1, d), lambda i, f: (modrow(i), 0, 0))
    row_spec = pl.BlockSpec((1, d), lambda i, f: (0, 0))
    vmem = 4 * tm * d * 4 + tm * d * 2 + 4 * d * tf * 2 + 3 * tm * tf * 4
    return pl.pallas_call(
        functools.partial(_mlp_kernel, tm=tm),
        grid=(rows // tm, ff // tf),
        in_specs=[pl.BlockSpec((tm, d), lambda i, f: (i, 0)),
                  row_spec, mod_spec, mod_spec,
                  pl.BlockSpec((d, tf), lambda i, f: (0, f)),
                  pl.BlockSpec((tf, d), lambda i, f: (f, 0)),
                  mod_spec, row_spec],
        out_specs=pl.BlockSpec((tm, d), lambda i, f: (i, 0)),
        out_shape=jax.ShapeDtypeStruct((rows, d), F32),
        scratch_shapes=[pltpu.VMEM((tm, d), BF16)],
        compiler_params=_cparams(("parallel", "arbitrary"), vmem),
        name="mlp_block",
    )(h, g_pre, shift, scale, w1, w2, gate, g_post)


HALO = GRID_W


def _rwkv_prep_kernel(h_ref, ha_ref, hb_ref, g_ref, sh_ref, sc_ref, mu_ref, o_ref, rs_scr, ra_scr, rb_scr,
                      *, tm, tc, n_lat_tiles, s, c):
    i = pl.program_id(0)
    jc = pl.program_id(1)
    is_lat = i < n_lat_tiles
    cs = pl.ds(pl.multiple_of(jc * tc, tc), tc)

    @pl.when(jc == 0)
    def _():
        def rstd(x):
            return lax.rsqrt(jnp.mean(x * x, axis=-1, keepdims=True) + EPS)

        def body(r, carry):
            sl = pl.ds(pl.multiple_of(r * ROW_BLOCK, ROW_BLOCK), ROW_BLOCK)
            rs_scr[sl, :] = rstd(h_ref[sl, :])
            return carry

        lax.fori_loop(0, tm // ROW_BLOCK, body, 0)
        ra_scr[...] = rstd(ha_ref[...])
        rb_scr[...] = rstd(hb_ref[...])

    g = g_ref[...]
    sh = sh_ref[0]
    sc = sc_ref[0]

    def mod(x, rs):
        return (x * rs * g) * (1.0 + sc) + sh

    u = mod(h_ref[:, cs], rs_scr[...])
    t = lax.broadcasted_iota(jnp.int32, u.shape, 0)
    period = jnp.where(is_lat, GRID_W, c)
    tpos = t & (period - 1)

    def emit(shifted):
        xx = shifted - u
        for m in range(6):
            o_ref[m] = (u + xx * mu_ref[m]).astype(o_ref.dtype)

    use_prev = (jc == 0) | (jnp.logical_not(is_lat) & (jc == 1))
    use_next = (is_lat & (jc == 1)) | (jnp.logical_not(is_lat) & (jc >= 2))

    @pl.when(use_prev)
    def _():
        emit(jnp.where(tpos > 0, pltpu.roll(u, 1, 0), 0.0))

    @pl.when(use_next)
    def _():
        emit(jnp.where(tpos < period - 1, pltpu.roll(u, tm - 1, 0), 0.0))

    @pl.when(is_lat & (jc == 2))
    def _():
        first = (i * tm) % s == 0
        above = jnp.where(first, 0.0, mod(ha_ref[:, cs], ra_scr[...]))
        emit(jnp.concatenate([above, u[:tm - HALO]], axis=0))

    @pl.when(is_lat & (jc == 3))
    def _():
        lastt = ((i + 1) * tm) % s == 0
        below = jnp.where(lastt, 0.0, mod(hb_ref[:, cs], rb_scr[...]))
        emit(jnp.concatenate([u[HALO:], below], axis=0))


def _rwkv_prep(h, g, shift, scale, mu6, dims):
    r, d = h.shape
    s, c = dims["S"], dims["C"]
    tm = c
    assert tm % HALO == 0 and s % tm == 0 and tm & (tm - 1) == 0
    tc = d // 4
    hb = tm // HALO
    nblk = r // HALO
    n_lat_tiles = dims["RL"] // tm
    modrow = _mod_row_map(tm, n_lat_tiles, s, dims["B"])
    mod_spec = pl.BlockSpec((1, 1, tc), lambda i, j: (modrow(i), 0, j))
    vmem = 2 * tm * d * 4 + 4 * HALO * d * 4 + 2 * 6 * tm * tc * 2 + 12 * tm * tc * 4
    return pl.pallas_call(
        functools.partial(_rwkv_prep_kernel, tm=tm, tc=tc, n_lat_tiles=n_lat_tiles, s=s, c=c),
        grid=(r // tm, 4),
        in_specs=[pl.BlockSpec((tm, d), lambda i, j: (i, 0)),
                  pl.BlockSpec((HALO, d), lambda i, j: (jnp.maximum(i * hb - 1, 0), 0)),
                  pl.BlockSpec((HALO, d), lambda i, j: (jnp.minimum((i + 1) * hb, nblk - 1), 0)),
                  pl.BlockSpec((1, tc), lambda i, j: (0, j)),
                  mod_spec, mod_spec,
                  pl.BlockSpec((6, 1, tc), lambda i, j: (0, 0, j))],
        out_specs=pl.BlockSpec((6, tm, tc), lambda i, j: (0, i, j)),
        out_shape=jax.ShapeDtypeStruct((6, r, d), BF16),
        scratch_shapes=[pltpu.VMEM((tm, 1), F32), pltpu.VMEM((HALO, 1), F32), pltpu.VMEM((HALO, 1), F32)],
        compiler_params=_cparams(("parallel", "arbitrary"), vmem),
        name="rwkv_prep",
    )(h, h, h, g, shift, scale, mu6)


BMM_CAST_ROWS = 256


def _bmm_kernel(x_ref, w_ref, o_ref, w_scr):
    @pl.when(pl.program_id(2) == 0)
    def _():
        def body(c, carry):
            sl = pl.ds(pl.multiple_of(c * BMM_CAST_ROWS, BMM_CAST_ROWS), BMM_CAST_ROWS)
            w_scr[sl, :] = w_ref[0, sl, :].astype(w_scr.dtype)
            return carry

        lax.fori_loop(0, w_scr.shape[0] // BMM_CAST_ROWS, body, 0)

    o_ref[0] = _dot(x_ref[0], w_scr[...]).astype(o_ref.dtype)


def _bmm(x, w, lhs0, w0, ng, out_dtype, dims):
    _, k, n = w.shape
    r = x.shape[1]
    tm = _pow2_tile(512, dims["S"], dims["RC"])
    tn = _pow2_tile(1024, n)
    osz = jnp.dtype(out_dtype).itemsize
    wsz = jnp.dtype(w.dtype).itemsize
    assert k % BMM_CAST_ROWS == 0
    return pl.pallas_call(
        _bmm_kernel,
        grid=(ng, n // tn, r // tm),
        in_specs=[pl.BlockSpec((1, tm, k), lambda g, j, i: (lhs0 + g, i, 0)),
                  pl.BlockSpec((1, k, tn), lambda g, j, i: (w0 + g, 0, j))],
        out_specs=pl.BlockSpec((1, tm, tn), lambda g, j, i: (g, i, j)),
        out_shape=jax.ShapeDtypeStruct((ng, r, n), out_dtype),
        scratch_shapes=[pltpu.VMEM((k, tn), BF16)],
        compiler_params=_cparams(("parallel", "parallel", "arbitrary"),
                                 2 * tm * k * 2 + 2 * k * tn * wsz + k * tn * 2 + 2 * tm * tn * osz + tm * tn * 4),
        name="rwkv_bmm",
    )(x, w)


def _pair_ones():
    a = lax.broadcasted_iota(jnp.int32, (LANE, LANE), 0) // RWKV_N
    b = lax.broadcasted_iota(jnp.int32, (LANE, LANE), 1) // RWKV_N
    return (a == b).astype(BF16)


def _head_sums(x, ones):
    parts = [_dot(x[:, l:l + LANE].astype(BF16), ones) for l in range(0, x.shape[1], LANE)]
    return parts[0] if len(parts) == 1 else jnp.concatenate(parts, axis=1)


def _rwkv_mid_kernel(k_ref, hid_ref, w2_ref, a2_ref, g2_ref, w0_ref, a0_ref, kk_w_ref,
                     lw_ref, a_ref, kk_ref, gate_ref, *, lp):
    ones = _pair_ones()
    hw = hid_ref[0]
    ha = hid_ref[1]
    hg = hid_ref[2]
    for d in range(2):
        tw = jnp.tanh(hw[:, d * lp:(d + 1) * lp]).astype(BF16)
        z = w0_ref[d] + _dot(tw, w2_ref[d])
        lw_ref[d] = -math.exp(-0.5) * jax.nn.sigmoid(z)
        za = a0_ref[d] + _dot(ha[:, d * lp:(d + 1) * lp].astype(BF16), a2_ref[d])
        a_ref[d] = jax.nn.sigmoid(za).astype(a_ref.dtype)
    gate_ref[...] = _dot(jax.nn.sigmoid(hg).astype(BF16), g2_ref[...]).astype(gate_ref.dtype)
    kk = k_ref[0].astype(F32) * kk_w_ref[...]
    ss = _head_sums(kk * kk, ones)
    kk_ref[...] = (kk * lax.rsqrt(jnp.maximum(ss, 1e-24))).astype(kk_ref.dtype)


def _rwkv_mid(rkv, hid, w2, a2, g2, w0, a0, k_k, dims):
    _, r, d = rkv.shape
    lp = w2.shape[1]
    lw_tot = hid.shape[2]
    tm = _pow2_tile(256, dims["S"], dims["RC"])
    tc = _pow2_tile(1024, d)
    vmem = 2 * (tm * tc * 2 + 3 * tm * lw_tot * 4 + (4 * lp + g2.shape[0]) * tc * 2
                + tm * tc * (8 + 4 + 2 + 2)) + 10 * tm * tc * 4
    return pl.pallas_call(
        functools.partial(_rwkv_mid_kernel, lp=lp),
        grid=(r // tm, d // tc),
        in_specs=[pl.BlockSpec((1, tm, tc), lambda i, j: (1, i, j)),
                  pl.BlockSpec((3, tm, lw_tot), lambda i, j: (0, i, 0)),
                  pl.BlockSpec((2, lp, tc), lambda i, j: (0, 0, j)),
                  pl.BlockSpec((2, lp, tc), lambda i, j: (0, 0, j)),
                  pl.BlockSpec((g2.shape[0], tc), lambda i, j: (0, j)),
                  pl.BlockSpec((2, 1, tc), lambda i, j: (0, 0, j)),
                  pl.BlockSpec((2, 1, tc), lambda i, j: (0, 0, j)),
                  pl.BlockSpec((1, tc), lambda i, j: (0, j))],
        out_specs=[pl.BlockSpec((2, tm, tc), lambda i, j: (0, i, j)),
                   pl.BlockSpec((2, tm, tc), lambda i, j: (0, i, j)),
                   pl.BlockSpec((tm, tc), lambda i, j: (i, j)),
                   pl.BlockSpec((tm, tc), lambda i, j: (i, j))],
        out_shape=[jax.ShapeDtypeStruct((2, r, d), F32), jax.ShapeDtypeStruct((2, r, d), BF16),
                   jax.ShapeDtypeStruct((r, d), BF16), jax.ShapeDtypeStruct((r, d), BF16)],
        compiler_params=_cparams(("parallel", "arbitrary"), vmem),
        name="rwkv_mid",
    )(rkv, hid, w2, a2, g2, w0, a0, k_k)


def _rwkv_scan_kernel(r_ref, k_ref, v_ref, kk_ref, a_ref, lw_ref, ka_ref, y_ref, s_scr, *, rev, npairs):
    @pl.when(pl.program_id(2) == 0)
    def _():
        s_scr[...] = jnp.zeros_like(s_scr)

    ln = RWKV_CHUNK
    row = lax.broadcasted_iota(jnp.int32, (ln, LANE), 0)
    lane = lax.broadcasted_iota(jnp.int32, (ln, LANE), 1)
    col = lane % RWKV_N
    head0 = lane < RWKV_N
    strict = (col > row) if rev else (col < row)
    incl = (col >= row) if rev else (col <= row)
    eye = (col == row).astype(F32)

    def same_block(size):
        return (row // size) == (col // size)

    leaf = same_block(INV_LEAF)
    offs = [same_block(2 * sz) & jnp.logical_not(same_block(sz))
            for sz in (INV_LEAF << i for i in range(int(math.log2(ln // INV_LEAF))))]
    ti = lax.broadcasted_iota(jnp.int32, (ln, ln), 0)
    si = lax.broadcasted_iota(jnp.int32, (ln, ln), 1)
    tri = ((si >= ti) if rev else (si <= ti)).astype(BF16)
    ra = lax.broadcasted_iota(jnp.int32, (LANE, LANE), 0)
    rb = lax.broadcasted_iota(jnp.int32, (LANE, LANE), 1)
    same_head = (ra // RWKV_N) == (rb // RWKV_N)
    eye_full = ra == rb
    last = 0 if rev else ln - 1

    def bd(x):
        zero = jnp.zeros_like(x)
        return jnp.concatenate([jnp.where(head0, x, zero), jnp.where(head0, zero, x)], axis=0)

    def pm(a, b):
        return _dot(a.astype(BF16), bd(b.astype(BF16)))

    sls = [slice(p * LANE, (p + 1) * LANE) for p in range(npairs)]

    def prepare(sl):
        lw = lw_ref[0, :, sl]
        hi = lw.astype(BF16)
        lo = (lw - hi.astype(F32)).astype(BF16)
        cc = _dot(tri, jnp.concatenate([hi, lo], axis=1))
        c = cc[:, :LANE] + cc[:, LANE:]
        a = a_ref[0, :, sl].astype(F32)
        kk = kk_ref[:, sl].astype(F32)
        winv = jnp.exp(-c)
        kd = k_ref[0, :, sl].astype(F32) * (1.0 + (a - 1.0) * ka_ref[:, sl])
        at = (-(kk * jnp.exp(c - lw))).astype(BF16)
        bt = (kk * a * winv).astype(BF16)
        kt = (kd * winv).astype(BF16)
        rt = (r_ref[0, :, sl].astype(F32) * jnp.exp(c)).astype(BF16)
        ctot = jnp.sum(jnp.where(eye_full, c[last:last + 1, :], 0.0), axis=1, keepdims=True)
        return at, bt, kt, rt, jnp.exp(ctot)

    pairs = range(npairs)
    prep = [prepare(sl) for sl in sls]
    ar = [jnp.concatenate([at, rt], axis=0) for at, _, _, rt, _ in prep]
    res = [_dot_nt(lhs, jnp.concatenate([bd(bt), bd(kt)], axis=0)) for lhs, (_, bt, kt, _, _) in zip(ar, prep)]
    mab = [jnp.where(strict, x[:ln, :LANE], 0.0) for x in res]
    mrk = [jnp.concatenate([jnp.where(strict, x[:ln, LANE:], 0.0), jnp.where(incl, x[ln:, LANE:], 0.0)],
                           axis=0).astype(BF16) for x in res]
    rbm = [jnp.where(incl, x[ln:, :LANE], 0.0).astype(BF16) for x in res]

    md = [jnp.where(leaf, m, 0.0) for m in mab]
    sq = [pm(m, m) for m in md]
    tinv = [eye + m + pm(q, eye + m) for m, q in zip(md, sq)]
    for off in offs:
        half = [pm(t, jnp.where(off, m, 0.0)) for t, m in zip(tinv, mab)]
        tinv = [t + pm(hf, t) for t, hf in zip(tinv, half)]

    s0 = [s_scr[p] for p in pairs]
    s0b = [s.astype(BF16) for s in s0]
    xy = [_dot(lhs, sb) + _dot(mk, bd(v_ref[0, :, sl])) for lhs, sb, mk, sl in zip(ar, s0b, mrk, sls)]
    ub = [pm(t, q[:ln]).astype(BF16) for t, q in zip(tinv, xy)]
    for p in pairs:
        at, bt, kt, rt, wtot = prep[p]
        y = xy[p][ln:] + _dot(rbm[p], bd(ub[p]))
        y_ref[:, sls[p]] = y.astype(y_ref.dtype)
        upd = _dot_tn(jnp.concatenate([bt, kt], axis=0), jnp.concatenate([ub[p], v_ref[0, :, sls[p]]], axis=0))
        s_scr[p] = (s0[p] + jnp.where(same_head, upd, 0.0)) * wtot


def _rwkv_scan(rkv, kk, a, lw, k_a, dims, rev):
    _, r, d = rkv.shape
    npairs = min(32, d // LANE)
    pw = npairs * LANE
    ln = RWKV_CHUNK
    nc, _, rowblk = _chunk_maps(dims, ln, rev)
    dsel = 1 if rev else 0

    def spec3(lead):
        return pl.BlockSpec((1, ln, pw), lambda b, g, n: (lead, rowblk(b, n), g))

    return pl.pallas_call(
        functools.partial(_rwkv_scan_kernel, rev=rev, npairs=npairs),
        grid=(dims["B"], d // pw, nc),
        in_specs=[spec3(0), spec3(1), spec3(2),
                  pl.BlockSpec((ln, pw), lambda b, g, n: (rowblk(b, n), g)),
                  spec3(dsel), spec3(dsel),
                  pl.BlockSpec((1, pw), lambda b, g, n: (0, g))],
        out_specs=pl.BlockSpec((ln, pw), lambda b, g, n: (rowblk(b, n), g)),
        out_shape=jax.ShapeDtypeStruct((r, d), BF16),
        scratch_shapes=[pltpu.VMEM((npairs, LANE, LANE), F32)],
        compiler_params=_cparams(("parallel", "parallel", "arbitrary"),
                                 2 * ln * pw * (5 * 2 + 4 + 2) + npairs * LANE * LANE * 4 + (8 << 20)),
        name="rwkv_scan_bwd" if rev else "rwkv_scan_fwd",
    )(rkv, rkv, rkv, kk, a, lw, k_a)


def _rwkv_finish_kernel(yf_ref, yb_ref, r_ref, k_ref, v_ref, a_ref, gate_ref, ka_ref, rk_ref, lw_ref, lb_ref, z_ref):
    ones = _pair_ones()
    inv_n = 1.0 / RWKV_N
    y = yf_ref[...].astype(F32) + yb_ref[...].astype(F32)
    mean = _head_sums(y, ones) * inv_n
    yc = y - mean
    var = _head_sums(yc * yc, ones) * inv_n
    yn = yc * lax.rsqrt(var + LNX_EPS) * lw_ref[...] + lb_ref[...]
    k = k_ref[0].astype(F32)
    ka = ka_ref[...]
    kd0 = k * (1.0 + (a_ref[0].astype(F32) - 1.0) * ka)
    kd1 = k * (1.0 + (a_ref[1].astype(F32) - 1.0) * ka)
    kb = 0.5 * (kd0 + kd1)
    bonus = _head_sums(r_ref[0].astype(F32) * kb * rk_ref[...], ones) * v_ref[0].astype(F32)
    z_ref[...] = ((yn + bonus) * gate_ref[...].astype(F32)).astype(z_ref.dtype)


def _rwkv_finish(yf, yb, rkv, a, gate, k_a, r_k, lnx_w, lnx_b, dims, rows):
    d = yf.shape[1]
    tm = _pow2_tile(256, dims["S"], dims["RC"])
    tc = _pow2_tile(1024, d)
    blk = pl.BlockSpec((tm, tc), lambda i, j: (i, j))
    vec = pl.BlockSpec((1, tc), lambda i, j: (0, j))

    def lead(g):
        return pl.BlockSpec((1, tm, tc), lambda i, j: (g, i, j))

    return pl.pallas_call(
        _rwkv_finish_kernel,
        grid=(rows // tm, d // tc),
        in_specs=[blk, blk, lead(0), lead(1), lead(2),
                  pl.BlockSpec((2, tm, tc), lambda i, j: (0, i, j)), blk, vec, vec, vec, vec],
        out_specs=blk,
        out_shape=jax.ShapeDtypeStruct((rows, d), BF16),
        compiler_params=_cparams(("parallel", "parallel"), 2 * tm * tc * 2 * 9 + 16 * tm * tc * 4),
        name="rwkv_finish",
    )(yf, yb, rkv, rkv, rkv, a, gate, k_a, r_k, lnx_w, lnx_b)


def _even_layer(h, mod, norm_g, w_in, b_gate, conv_qk, gn_ret, gn_ml, w_out, dims, rows_out):
    d = h.shape[1]
    heads, nqk, nv = dims["H"], dims["NQK"], dims["NV"]
    n_main = 4 * nqk + 4 * nv
    n_gate = 4 * heads
    w_gate = jnp.pad(w_in[:, n_main:], ((0, 0), (0, LANE - n_gate)))
    bg = jnp.pad(b_gate, (0, LANE - n_gate)).reshape(1, LANE)
    shift, scale, gate = mod[0], mod[1], mod[2]
    p, gates = _inproj(h, norm_g[0].reshape(1, d), shift, scale, w_in, n_main, w_gate, bg, dims)
    post = jnp.concatenate([jnp.ones((nqk,), F32), jnp.full((nqk,), HEAD_DK ** -0.5, F32)]).reshape(1, 2 * nqk)
    qk = _conv_qk(p, conv_qk, post, dims)
    cos, sin = _rope_tables(dims)
    yrf = _retention(p, cos, sin, dims, False)
    yrb = _retention(p, cos, sin, dims, True)
    ymf = _mlstm(qk, p, gates, dims, False)
    ymb = _mlstm(qk, p, gates, dims, True)
    z = _merge(yrf, yrb, ymf, ymb, p, gn_ret.reshape(1, nv), gn_ml.reshape(1, nv), dims, rows_out)
    return _outproj(z, w_out, h, gate, norm_g[1].reshape(1, d), dims, rows_out)


def _rwkv_layer(h, mod, norm_g, mu, w_rkv, w_rkv0, w0, w1, w2, a0, a1, a2, g1, g2, k_k, k_a, r_k, lnx_w, lnx_b,
                w_out, dims, rows_out):
    d = h.shape[1]
    shift, scale, gate = mod[0], mod[1], mod[2]
    order = jnp.array([0, 2, 3, 1, 4, 5])
    mu6 = mu[order].reshape(6, 1, d)
    x6 = _rwkv_prep(h, norm_g[0].reshape(1, d), shift, scale, mu6, dims)
    rkv = _bmm(x6, w_rkv, 0, w_rkv0, 3, BF16, dims)
    lora = w1.shape[2]
    glora = g1.shape[1]
    lp = _round_up(lora, LANE)
    gp = _round_up(glora, LANE)
    lw_tot = max(2 * lp, gp)

    def two(w):
        w = jnp.pad(w, ((0, 0), (0, 0), (0, lp - lora)))
        w = jnp.concatenate([w[0], w[1]], axis=1)
        return jnp.pad(w, ((0, 0), (0, lw_tot - 2 * lp)))

    w_l1 = jnp.stack([two(w1), two(a1), jnp.pad(g1, ((0, 0), (0, lw_tot - glora)))]).astype(BF16)
    hid = _bmm(x6, w_l1, 3, 0, 3, F32, dims)
    pad_rows = lambda w, n: jnp.pad(w, ((0, 0),) * (w.ndim - 2) + ((0, n - w.shape[-2]), (0, 0))).astype(BF16)
    lw, a, kk, gmul = _rwkv_mid(rkv, hid, pad_rows(w2, lp), pad_rows(a2, lp), pad_rows(g2, lw_tot),
                                w0.reshape(2, 1, d), a0.reshape(2, 1, d), k_k.reshape(1, d), dims)
    ka = k_a.reshape(1, d)
    yf = _rwkv_scan(rkv, kk, a, lw, ka, dims, False)
    yb = _rwkv_scan(rkv, kk, a, lw, ka, dims, True)
    z = _rwkv_finish(yf, yb, rkv, a, gmul, ka, r_k.reshape(1, d), lnx_w.reshape(1, d), lnx_b.reshape(1, d),
                     dims, rows_out)
    return _outproj(z, w_out, h, gate, norm_g[1].reshape(1, d), dims, rows_out)


def kernel(x, c, ctx, c_ctx, ada_w, ada_b, norm_g, mlp_w_in, mlp_w_out, ev_w_in, ev_b_gate, ev_conv_qk,
           ev_gn_ret, ev_gn_mlstm, ev_w_out, od_mu, od_w_rkv, od_w0, od_w1, od_w2, od_a0, od_a1, od_a2,
           od_g1, od_g2, od_k_k, od_k_a, od_r_k, od_lnx_w, od_lnx_b, od_w_out):
    b, s, d = x.shape
    cl = ctx.shape[1]
    depth = ada_w.shape[0]
    heads = d // (2 * HEAD_DV)
    assert b + 1 <= MOD_ROWS and s % CHUNK == 0 and cl % CHUNK == 0 and s % GRID_W == 0
    dims = dict(B=b, S=s, C=cl, RL=b * s, RC=b * cl, H=heads, NQK=heads * HEAD_DK, NV=heads * HEAD_DV)
    rl, r = dims["RL"], dims["RL"] + dims["RC"]

    cond = jnp.concatenate([c, c_ctx[None, :], jnp.zeros((MOD_ROWS - b - 1, d), F32)], axis=0)
    tables = _ada_tables(cond, ada_w, ada_b)
    h = jnp.concatenate([x.reshape(rl, d), ctx.reshape(b * cl, d)], axis=0)

    for layer in range(depth):
        with_ctx = layer < depth - 1
        rows_out = r if with_ctx else rl
        mod = tables[layer].reshape(MOD_ROWS, 6, 1, d).transpose(1, 0, 2, 3)
        g = norm_g[layer]
        if layer % 2 == 0:
            e = layer // 2
            h_mix = _even_layer(h, mod[0:3], g[0:2], ev_w_in[e].astype(BF16), ev_b_gate[e], ev_conv_qk[e],
                                ev_gn_ret[e], ev_gn_mlstm[e], _to_bf16(ev_w_out, e), dims, rows_out)
        else:
            o = layer // 2
            w_rkv_all = od_w_rkv.reshape((-1,) + od_w_rkv.shape[2:])
            h_mix = _rwkv_layer(h, mod[0:3], g[0:2], od_mu[o], w_rkv_all, 3 * o, od_w0[o], od_w1[o], od_w2[o],
                                od_a0[o], od_a1[o], od_a2[o], od_g1[o], od_g2[o], od_k_k[o], od_k_a[o], od_r_k[o],
                                od_lnx_w[o], od_lnx_b[o], _to_bf16(od_w_out, o), dims, rows_out)
        h = _mlp(h_mix, g[2].reshape(1, d), mod[3], mod[4], _to_bf16(mlp_w_in, layer),
                 _to_bf16(mlp_w_out, layer), mod[5], g[3].reshape(1, d), dims, rows_out)
    return h[:rl].reshape(b, s, d)
```

```python
import functools
import math

import jax
import jax.numpy as jnp
from jax import lax
from jax.experimental import pallas as pl
from jax.experimental.pallas import tpu as pltpu

F32 = jnp.float32
BF16 = jnp.bfloat16

EPS = 1e-6
LNX_EPS = 64e-5
ROPE_BASE = 10000.0
GRID_W = 64
CHUNK = 128
RWKV_CHUNK = 64
HEAD_DK = 128
HEAD_DV = 256
RWKV_N = 64
INV_LEAF = 4
CONV_K = 3
LANE = 128
MOD_ROWS = 8
VMEM_BYTES = 64 * 1024 * 1024

NT_DIMS = (((1,), (1,)), ((), ()))
TN_DIMS = (((0,), (0,)), ((), ()))


def _dot(a, b):
    return jnp.dot(a, b, preferred_element_type=F32)


def _dot_nt(a, b):
    return lax.dot_general(a, b, NT_DIMS, preferred_element_type=F32)


def _dot_tn(a, b):
    return lax.dot_general(a, b, TN_DIMS, preferred_element_type=F32)


def _split_dot(m, x):
    hi = x.astype(BF16)
    lo = (x - hi.astype(F32)).astype(BF16)
    return _dot(m, hi) + _dot(m, lo)


def _cparams(semantics, vmem_bytes):
    limit = int(min(max(vmem_bytes * 5 // 4 + (4 << 20), 16 << 20), VMEM_BYTES - (6 << 20)))
    return pltpu.CompilerParams(dimension_semantics=semantics, vmem_limit_bytes=limit)


def _pow2_tile(pref, *dims):
    t = pref
    while any(d % t for d in dims):
        t //= 2
    return t


def _round_up(x, m):
    return (x + m - 1) // m * m


CAST_BLOCK_ELEMS = 1 << 21


def _cast_kernel(x_ref, o_ref):
    o_ref[...] = x_ref[...].astype(o_ref.dtype)


def _to_bf16(w, lead, ncols=None):
    shape = w.shape[1:]
    w3 = w.reshape(w.shape[0], -1, shape[-1])
    rows = w3.shape[1]
    ncols = shape[-1] if ncols is None else ncols
    tc = _pow2_tile(4096, ncols)
    tr = _pow2_tile(max(CAST_BLOCK_ELEMS // tc, 16), rows)
    out = pl.pallas_call(
        _cast_kernel,
        grid=(rows // tr, ncols // tc),
        in_specs=[pl.BlockSpec((None, tr, tc), lambda i, j: (lead, i, j))],
        out_specs=pl.BlockSpec((tr, tc), lambda i, j: (i, j)),
        out_shape=jax.ShapeDtypeStruct((rows, ncols), BF16),
        compiler_params=_cparams(("parallel", "parallel"), 2 * tr * tc * 6),
        name="weight_cast",
    )(w3)
    return out.reshape(shape[:-1] + (ncols,))


def _ada_kernel(c_ref, w_ref, b_ref, o_ref):
    c = c_ref[...]
    s = c * jax.nn.sigmoid(c)
    o_ref[0] = _dot(s.astype(BF16), w_ref[0].astype(BF16)) + b_ref[0]


def _ada_tables(cond, ada_w, ada_b):
    depth, d, n = ada_w.shape
    tn = _pow2_tile(512, n)
    return pl.pallas_call(
        _ada_kernel,
        grid=(depth, n // tn),
        in_specs=[pl.BlockSpec((MOD_ROWS, d), lambda l, j: (0, 0)),
                  pl.BlockSpec((1, d, tn), lambda l, j: (l, 0, j)),
                  pl.BlockSpec((1, 1, tn), lambda l, j: (l, 0, j))],
        out_specs=pl.BlockSpec((1, MOD_ROWS, tn), lambda l, j: (l, 0, j)),
        out_shape=jax.ShapeDtypeStruct((depth, MOD_ROWS, n), F32),
        compiler_params=_cparams(("parallel", "parallel"), 2 * d * tn * 4 + d * tn * 2),
        name="ada_tables",
    )(cond, ada_w, ada_b.reshape(depth, 1, n))


ROW_BLOCK = 32
NORM_COLS = 512


def _col_slices(d):
    step = min(NORM_COLS, d)
    return [slice(c, c + step) for c in range(0, d, step)]


def _row_rstd(x_ref, sl, cols, d):
    acc = None
    for c in cols:
        x = x_ref[sl, c]
        acc = x * x if acc is None else acc + x * x
    return lax.rsqrt(jnp.sum(acc, axis=-1, keepdims=True) / d + EPS)


def _modulate_rows(h_ref, g_ref, sh_ref, sc_ref, u_scr, tm):
    d = h_ref.shape[1]
    cols = _col_slices(d)

    def body(r, carry):
        sl = pl.ds(pl.multiple_of(r * ROW_BLOCK, ROW_BLOCK), ROW_BLOCK)
        rs = _row_rstd(h_ref, sl, cols, d)
        for c in cols:
            u = (h_ref[sl, c] * rs * g_ref[:, c]) * (1.0 + sc_ref[0, :, c]) + sh_ref[0, :, c]
            u_scr[sl, c] = u.astype(u_scr.dtype)
        return carry

    lax.fori_loop(0, tm // ROW_BLOCK, body, 0)


def _norm_residual_rows(acc_ref, h_ref, gate_ref, g_ref, o_ref, tm):
    d = acc_ref.shape[1]
    cols = _col_slices(d)

    def body(r, carry):
        sl = pl.ds(pl.multiple_of(r * ROW_BLOCK, ROW_BLOCK), ROW_BLOCK)
        rs = _row_rstd(acc_ref, sl, cols, d)
        for c in cols:
            o_ref[sl, c] = h_ref[sl, c] + gate_ref[0, :, c] * (acc_ref[sl, c] * rs * g_ref[:, c])
        return carry

    lax.fori_loop(0, tm // ROW_BLOCK, body, 0)


def _mod_row_map(tm, n_lat_tiles, s, b):
    def f(i):
        return jnp.where(i < n_lat_tiles, (i * tm) // s, b)
    return f


def _inproj_kernel(h_ref, g_ref, sh_ref, sc_ref, w_ref, wg_ref, bg_ref, p_ref, gt_ref, u_scr, *, tm):
    @pl.when(pl.program_id(1) == 0)
    def _():
        _modulate_rows(h_ref, g_ref, sh_ref, sc_ref, u_scr, tm)
        gt_ref[...] = _dot(u_scr[...], wg_ref[...]) + bg_ref[...]

    p_ref[...] = _dot(u_scr[...], w_ref[...]).astype(p_ref.dtype)


def _inproj(h, g, shift, scale, w, n, wg, bg, dims):
    r, d = h.shape
    tm = _pow2_tile(512, dims["S"], dims["RC"])
    tn = _pow2_tile(1024, n)
    modrow = _mod_row_map(tm, dims["RL"] // tm, dims["S"], dims["B"])
    vmem = 2 * tm * d * 4 + tm * d * 2 + 2 * d * tn * 2 + 2 * tm * tn * 2 + 2 * d * LANE * 2 + 2 * tm * LANE * 4
    return pl.pallas_call(
        functools.partial(_inproj_kernel, tm=tm),
        grid=(r // tm, n // tn),
        in_specs=[pl.BlockSpec((tm, d), lambda i, j: (i, 0)),
                  pl.BlockSpec((1, d), lambda i, j: (0, 0)),
                  pl.BlockSpec((1, 1, d), lambda i, j: (modrow(i), 0, 0)),
                  pl.BlockSpec((1, 1, d), lambda i, j: (modrow(i), 0, 0)),
                  pl.BlockSpec((d, tn), lambda i, j: (0, j)),
                  pl.BlockSpec((d, LANE), lambda i, j: (0, 0)),
                  pl.BlockSpec((1, LANE), lambda i, j: (0, 0))],
        out_specs=[pl.BlockSpec((tm, tn), lambda i, j: (i, j)),
                   pl.BlockSpec((tm, LANE), lambda i, j: (i, 0))],
        out_shape=[jax.ShapeDtypeStruct((r, n), BF16), jax.ShapeDtypeStruct((r, LANE), F32)],
        scratch_shapes=[pltpu.VMEM((tm, d), BF16)],
        compiler_params=_cparams(("parallel", "arbitrary"), vmem),
        name="even_inproj",
    )(h, g, shift, scale, w, wg, bg)


def _conv_kernel(x_ref, w_ref, post_ref, o_ref, *, rows, grid_mode):
    x = x_ref[...].astype(F32)
    t = lax.broadcasted_iota(jnp.int32, x.shape, 0)
    w = w_ref[...]
    if grid_mode:
        col = t % GRID_W
        has_prev, has_next = col > 0, col < GRID_W - 1
    else:
        has_prev, has_next = t > 0, t < rows - 1
    xl = jnp.where(has_prev, pltpu.roll(x, 1, 0), 0.0)
    xr = jnp.where(has_next, pltpu.roll(x, rows - 1, 0), 0.0)

    def tap_row(dr):
        return xl * w[3 * dr:3 * dr + 1] + x * w[3 * dr + 1:3 * dr + 2] + xr * w[3 * dr + 2:3 * dr + 3]

    y = tap_row(1)
    if grid_mode:
        z = jnp.zeros((GRID_W, x.shape[1]), F32)
        y = y + jnp.concatenate([z, tap_row(0)[:rows - GRID_W]], axis=0)
        y = y + jnp.concatenate([tap_row(2)[GRID_W:], z], axis=0)
    y = y * jax.nn.sigmoid(y)
    o_ref[...] = (y * post_ref[...]).astype(o_ref.dtype)


def _conv_qk(p, conv_w, post, dims):
    b, s, c = dims["B"], dims["S"], dims["C"]
    nqk = dims["NQK"]
    ch = 2 * nqk
    tc = LANE
    col0 = 6 * nqk // tc
    w9 = conv_w.reshape(CONV_K * CONV_K, ch)

    def call(rows, blk0, grid_mode):
        return pl.pallas_call(
            functools.partial(_conv_kernel, rows=rows, grid_mode=grid_mode),
            grid=(b, ch // tc),
            in_specs=[pl.BlockSpec((rows, tc), lambda i, j: (blk0 + i, col0 + j)),
                      pl.BlockSpec((CONV_K * CONV_K, tc), lambda i, j: (0, j)),
                      pl.BlockSpec((1, tc), lambda i, j: (0, j))],
            out_specs=pl.BlockSpec((rows, tc), lambda i, j: (i, j)),
            out_shape=jax.ShapeDtypeStruct((b * rows, ch), BF16),
            compiler_params=_cparams(("parallel", "parallel"), 16 * rows * tc * 4),
            name="conv_lat" if grid_mode else "conv_ctx",
        )(p, w9, post)

    return jnp.concatenate([call(s, 0, True), call(c, (b * s) // c, False)], axis=0)


def _chunk_maps(dims, chunk, rev):
    b_, s, c = dims["B"], dims["S"], dims["C"]
    ncc, ncl = c // chunk, s // chunk
    nc = ncc + ncl

    def pos(n):
        if not rev:
            return n
        return jnp.where(n < ncc, ncc - 1 - n, ncc + nc - 1 - n)

    def rowblk(b, n):
        q = pos(n)
        return jnp.where(q < ncc, (b_ * s) // chunk + b * ncc + q, b * ncl + q - ncc)

    return nc, pos, rowblk


def _retention_kernel(q_ref, k_ref, v_ref, cos_ref, sin_ref, intra_ref, qd_ref, kd_ref, cd_ref, y_ref, s_scr,
                      *, heads):
    @pl.when(pl.program_id(1) == 0)
    def _():
        s_scr[...] = jnp.zeros_like(s_scr)

    cos = cos_ref[...]
    sin = sin_ref[...]
    lane = lax.broadcasted_iota(jnp.int32, cos.shape, 1)
    first = (lane % (HEAD_DK // 2)) < (HEAD_DK // 4)

    def rope(x):
        swapped = jnp.where(first, pltpu.roll(x, HEAD_DK - HEAD_DK // 4, 1), pltpu.roll(x, HEAD_DK // 4, 1))
        return x * cos + swapped * sin

    for h in range(heads):
        ks = slice(h * HEAD_DK, (h + 1) * HEAD_DK)
        vs = slice(h * HEAD_DV, (h + 1) * HEAD_DV)
        q = rope(q_ref[:, ks].astype(F32))
        k = rope(k_ref[:, ks].astype(F32)) * HEAD_DK ** -0.5
        v = v_ref[:, vs]
        s = s_scr[h]
        sc = _dot_nt(q.astype(BF16), k.astype(BF16)) * intra_ref[h]
        o = _dot(sc.astype(BF16), v) + _dot((q * qd_ref[h]).astype(BF16), s.astype(BF16))
        s_scr[h] = s * cd_ref[h] + _dot_tn((k * kd_ref[h]).astype(BF16), v)
        y_ref[:, vs] = o.astype(y_ref.dtype)


def _retention_tables(heads, rev):
    h = jnp.arange(heads, dtype=F32) / max(heads - 1, 1)
    lg = jnp.log1p(-jnp.exp2(-(5.0 + 7.0 * h)))
    if rev:
        lg = lg[::-1]
    pos = jnp.arange(CHUNK, dtype=F32)
    diff = pos[:, None] - pos[None, :]
    if rev:
        diff = -diff
    intra = jnp.where(diff >= 0, jnp.exp(jnp.maximum(diff, 0.0)[None] * lg[:, None, None]), 0.0)
    q_pow = (CHUNK - pos) if rev else (pos + 1.0)
    k_pow = pos if rev else (CHUNK - 1.0 - pos)
    qd = jnp.exp(q_pow[None, :] * lg[:, None])[..., None]
    kd = jnp.exp(k_pow[None, :] * lg[:, None])[..., None]
    cd = jnp.exp(CHUNK * lg)[:, None, None]
    return intra, qd, kd, cd


def _retention(p, cos, sin, dims, rev):
    heads, nqk, nv = dims["H"], dims["NQK"], dims["NV"]
    r = p.shape[0]
    nc, pos, rowblk = _chunk_maps(dims, CHUNK, rev)
    intra, qd, kd, cd = _retention_tables(heads, rev)
    full = lambda shape: pl.BlockSpec(shape, lambda b, n: (0,) * len(shape))
    vmem = (2 * CHUNK * (2 * nqk + 2 * nv) * 2 + 2 * heads * CHUNK * (CHUNK + 2 * LANE) * 4
            + heads * HEAD_DK * HEAD_DV * 4)
    return pl.pallas_call(
        functools.partial(_retention_kernel, heads=heads),
        grid=(dims["B"], nc),
        in_specs=[pl.BlockSpec((CHUNK, nqk), lambda b, n: (rowblk(b, n), 0)),
                  pl.BlockSpec((CHUNK, nqk), lambda b, n: (rowblk(b, n), 1)),
                  pl.BlockSpec((CHUNK, nv), lambda b, n: (rowblk(b, n), 1)),
                  pl.BlockSpec((CHUNK, HEAD_DK), lambda b, n: (pos(n), 0)),
                  pl.BlockSpec((CHUNK, HEAD_DK), lambda b, n: (pos(n), 0)),
                  full((heads, CHUNK, CHUNK)), full((heads, CHUNK, 1)), full((heads, CHUNK, 1)),
                  full((heads, 1, 1))],
        out_specs=pl.BlockSpec((CHUNK, nv), lambda b, n: (rowblk(b, n), 0)),
        out_shape=jax.ShapeDtypeStruct((r, nv), BF16),
        scratch_shapes=[pltpu.VMEM((heads, HEAD_DK, HEAD_DV), F32)],
        compiler_params=_cparams(("parallel", "arbitrary"), vmem),
        name="retention_bwd" if rev else "retention_fwd",
    )(p, p, p, cos, sin, intra, qd, kd, cd)


def _rope_tables(dims):
    s, c = dims["S"], dims["C"]
    quarter = HEAD_DK // 4
    inv = jnp.power(ROPE_BASE, -jnp.arange(quarter, dtype=F32) / quarter)
    t = jnp.arange(s)
    rows = (t // GRID_W).astype(F32)
    cols = (t % GRID_W).astype(F32)
    ar = rows[:, None] * inv[None, :]
    ac = cols[:, None] * inv[None, :]
    cos = jnp.concatenate([jnp.cos(ar), jnp.cos(ar), jnp.cos(ac), jnp.cos(ac)], axis=-1)
    sin = jnp.concatenate([-jnp.sin(ar), jnp.sin(ar), -jnp.sin(ac), jnp.sin(ac)], axis=-1)
    cos = jnp.concatenate([jnp.ones((c, HEAD_DK), F32), cos], axis=0)
    sin = jnp.concatenate([jnp.zeros((c, HEAD_DK), F32), sin], axis=0)
    return cos, sin


def _mlstm_kernel(q_ref, k_ref, v_ref, g_ref, y_ref, c_scr, n_scr, m_scr, *, heads, rev):
    @pl.when(pl.program_id(1) == 0)
    def _():
        c_scr[...] = jnp.zeros_like(c_scr)
        n_scr[...] = jnp.zeros_like(n_scr)
        m_scr[...] = jnp.zeros_like(m_scr)

    ti = lax.broadcasted_iota(jnp.int32, (CHUNK, CHUNK), 0)
    si = lax.broadcasted_iota(jnp.int32, (CHUNK, CHUNK), 1)
    causal = (si >= ti) if rev else (si <= ti)
    tri = causal.astype(BF16)
    g = g_ref[...]
    logf = jnp.minimum(g, 0.0) - jnp.log1p(jnp.exp(-jnp.abs(g)))
    cs = _split_dot(tri, logf)
    g_t = g.T
    cs_t = cs.T
    icol0 = 2 * heads if rev else 0
    fcol0 = icol0 + heads
    last = 0 if rev else CHUNK - 1

    for h in range(heads):
        ks = slice(h * HEAD_DK, (h + 1) * HEAD_DK)
        vs = slice(h * HEAD_DV, (h + 1) * HEAD_DV)
        ic, fc = icol0 + h, fcol0 + h
        b_col, b_row = cs[:, fc:fc + 1], cs_t[fc:fc + 1, :]
        i_col, i_row = g[:, ic:ic + 1], g_t[ic:ic + 1, :]
        b_last = cs[last:last + 1, fc:fc + 1]
        m = m_scr[h]
        q = q_ref[:, ks]
        k = k_ref[:, ks]
        v = v_ref[:, vs]
        c_mem = c_scr[h]
        n_mem = n_scr[h]

        d_log = jnp.where(causal, b_col - b_row + i_row, -jnp.inf)
        m_inter = b_col + m
        m_row = jnp.maximum(jnp.max(d_log, axis=-1, keepdims=True), m_inter)
        w_intra = jnp.exp(d_log - m_row)
        w_inter = jnp.exp(m_inter - m_row)
        sc = _dot_nt(q, k) * w_intra
        num = _dot(sc.astype(BF16), v) + w_inter * _dot(q, c_mem.astype(BF16))
        den = jnp.sum(sc, axis=-1, keepdims=True) + w_inter * jnp.sum(q.astype(F32) * n_mem, axis=-1, keepdims=True)
        y_ref[:, vs] = (num / jnp.maximum(jnp.abs(den), jnp.exp(-m_row))).astype(y_ref.dtype)

        gk = b_last - b_col + i_col
        m_new = jnp.maximum(b_last + m, jnp.max(gk, axis=0, keepdims=True))
        w_k = jnp.exp(gk - m_new)
        w_c = jnp.exp(b_last + m - m_new)
        kw = k.astype(F32) * w_k
        c_scr[h] = w_c * c_mem + _dot_tn(kw.astype(BF16), v)
        n_scr[h] = w_c * n_mem + jnp.sum(kw, axis=0, keepdims=True)
        m_scr[h] = m_new


def _mlstm(qk, p, gates, dims, rev):
    heads, nqk, nv = dims["H"], dims["NQK"], dims["NV"]
    r = p.shape[0]
    nc, _, rowblk = _chunk_maps(dims, CHUNK, rev)
    vmem = 2 * CHUNK * (2 * nqk + 2 * nv) * 2 + 2 * CHUNK * LANE * 4 + heads * HEAD_DK * (HEAD_DV + 8) * 4
    return pl.pallas_call(
        functools.partial(_mlstm_kernel, heads=heads, rev=rev),
        grid=(dims["B"], nc),
        in_specs=[pl.BlockSpec((CHUNK, nqk), lambda b, n: (rowblk(b, n), 0)),
                  pl.BlockSpec((CHUNK, nqk), lambda b, n: (rowblk(b, n), 1)),
                  pl.BlockSpec((CHUNK, nv), lambda b, n: (rowblk(b, n), 4)),
                  pl.BlockSpec((CHUNK, LANE), lambda b, n: (rowblk(b, n), 0))],
        out_specs=pl.BlockSpec((CHUNK, nv), lambda b, n: (rowblk(b, n), 0)),
        out_shape=jax.ShapeDtypeStruct((r, nv), BF16),
        scratch_shapes=[pltpu.VMEM((heads, HEAD_DK, HEAD_DV), F32),
                        pltpu.VMEM((heads, 1, HEAD_DK), F32),
                        pltpu.VMEM((heads, 1, 1), F32)],
        compiler_params=_cparams(("parallel", "arbitrary"), vmem),
        name="mlstm_bwd" if rev else "mlstm_fwd",
    )(qk, qk, p, gates)


def _merge_kernel(yrf_ref, yrb_ref, ymf_ref, ymb_ref, ag_ref, bo_ref, gr_ref, gm_ref, z_ref, *, heads):
    nv = heads * HEAD_DV
    for h in range(heads):
        vs = slice(h * HEAD_DV, (h + 1) * HEAD_DV)
        yr = yrf_ref[:, vs].astype(F32) + yrb_ref[:, vs].astype(F32)
        yr = yr * lax.rsqrt(jnp.mean(yr * yr, axis=-1, keepdims=True) + EPS) * gr_ref[:, vs]
        ag = ag_ref[:, vs].astype(F32)
        z_ref[:, vs] = (yr * (ag * jax.nn.sigmoid(ag))).astype(z_ref.dtype)
        ym = ymf_ref[:, vs].astype(F32) + ymb_ref[:, vs].astype(F32)
        ym = ym * lax.rsqrt(jnp.mean(ym * ym, axis=-1, keepdims=True) + EPS) * gm_ref[:, vs]
        bo = bo_ref[:, vs].astype(F32)
        z_ref[:, nv + h * HEAD_DV:nv + (h + 1) * HEAD_DV] = (ym * jax.nn.sigmoid(bo)).astype(z_ref.dtype)


def _merge(yrf, yrb, ymf, ymb, p, gn_ret, gn_ml, dims, rows):
    heads, nv = dims["H"], dims["NV"]
    tm = _pow2_tile(256, dims["S"], dims["RC"])
    ys = pl.BlockSpec((tm, nv), lambda i: (i, 0))
    return pl.pallas_call(
        functools.partial(_merge_kernel, heads=heads),
        grid=(rows // tm,),
        in_specs=[ys, ys, ys, ys,
                  pl.BlockSpec((tm, nv), lambda i: (i, 2)),
                  pl.BlockSpec((tm, nv), lambda i: (i, 5)),
                  pl.BlockSpec((1, nv), lambda i: (0, 0)),
                  pl.BlockSpec((1, nv), lambda i: (0, 0))],
        out_specs=pl.BlockSpec((tm, 2 * nv), lambda i: (i, 0)),
        out_shape=jax.ShapeDtypeStruct((rows, 2 * nv), BF16),
        compiler_params=_cparams(("parallel",), 2 * tm * nv * 2 * 8 + 8 * tm * nv * 4),
        name="even_merge",
    )(yrf, yrb, ymf, ymb, p, p, gn_ret, gn_ml)


def _outproj_kernel(z_ref, w_ref, h_ref, gate_ref, g_ref, o_ref, *, tm, tn):
    j = pl.program_id(1)
    o_ref[:, pl.ds(pl.multiple_of(j * tn, tn), tn)] = _dot(z_ref[...], w_ref[...])

    @pl.when(j == pl.num_programs(1) - 1)
    def _():
        _norm_residual_rows(o_ref, h_ref, gate_ref, g_ref, o_ref, tm)


def _outproj(z, w, h, gate, g, dims, rows):
    k, n = w.shape
    tm = _pow2_tile(512, dims["S"], dims["RC"])
    tn = _pow2_tile(512, n)
    modrow = _mod_row_map(tm, dims["RL"] // tm, dims["S"], dims["B"])
    vmem = 2 * tm * k * 2 + 2 * k * tn * 2 + tm * n * 4 * 4 + 2 * tm * tn * 4
    return pl.pallas_call(
        functools.partial(_outproj_kernel, tm=tm, tn=tn),
        grid=(rows // tm, n // tn),
        in_specs=[pl.BlockSpec((tm, k), lambda i, j: (i, 0)),
                  pl.BlockSpec((k, tn), lambda i, j: (0, j)),
                  pl.BlockSpec((tm, n), lambda i, j: (i, 0)),
                  pl.BlockSpec((1, 1, n), lambda i, j: (modrow(i), 0, 0)),
                  pl.BlockSpec((1, n), lambda i, j: (0, 0))],
        out_specs=pl.BlockSpec((tm, n), lambda i, j: (i, 0)),
        out_shape=jax.ShapeDtypeStruct((rows, n), F32),
        compiler_params=_cparams(("parallel", "arbitrary"), vmem),
        name="outproj_norm_residual",
    )(z, w, h, gate, g)


def _mlp_kernel(h_ref, gpre_ref, sh_ref, sc_ref, w1_hbm, w2_hbm, gate_ref, gpost_ref, o_ref, u_scr, *, tm, tf):
    d, ff = w1_hbm.shape
    _modulate_rows(h_ref, gpre_ref, sh_ref, sc_ref, u_scr, tm)
    o_ref[...] = jnp.zeros_like(o_ref)

    def ff_step(w1_ref, w2_ref):
        a = jnp.maximum(_dot(u_scr[...], w1_ref[...]), 0.0)
        o_ref[...] += _dot((a * a).astype(BF16), w2_ref[...])

    pltpu.emit_pipeline(
        ff_step,
        grid=(ff // tf,),
        in_specs=[pl.BlockSpec((d, tf), lambda f: (0, f)), pl.BlockSpec((tf, d), lambda f: (f, 0))],
    )(w1_hbm, w2_hbm)
    _norm_residual_rows(o_ref, h_ref, gate_ref, gpost_ref, o_ref, tm)


def _mlp(h, g_pre, shift, scale, w1, w2, gate, g_post, dims, rows):
    d, ff = w1.shape
    tm = _pow2_tile(512, dims["S"], dims["RC"])
    tf = _pow2_tile(512, ff)
    modrow = _mod_row_map(tm, dims["RL"] // tm, dims["S"], dims["B"])
    mod_spec = pl.BlockSpec((1, 1, d), lambda i: (modrow(i), 0, 0))
    row_spec = pl.BlockSpec((1, d), lambda i: (0, 0))
    vmem = 4 * tm * d * 4 + tm * d * 2 + 4 * d * tf * 2 + 3 * tm * tf * 4
    return pl.pallas_call(
        functools.partial(_mlp_kernel, tm=tm, tf=tf),
        grid=(rows // tm,),
        in_specs=[pl.BlockSpec((tm, d), lambda i: (i, 0)),
                  row_spec, mod_spec, mod_spec,
                  pl.BlockSpec(memory_space=pl.ANY),
                  pl.BlockSpec(memory_space=pl.ANY),
                  mod_spec, row_spec],
        out_specs=pl.BlockSpec((tm, d), lambda i: (i, 0)),
        out_shape=jax.ShapeDtypeStruct((rows, d), F32),
        scratch_shapes=[pltpu.VMEM((tm, d), BF16)],
        compiler_params=_cparams(("arbitrary",), vmem),
        name="mlp_block",
    )(h, g_pre, shift, scale, w1, w2, gate, g_post)


HALO = GRID_W


def _rwkv_prep_kernel(h_ref, ha_ref, hb_ref, g_ref, sh_ref, sc_ref, mu_ref, o_ref, rs_scr, ra_scr, rb_scr,
                      *, tm, tc, n_lat_tiles, s, c):
    i = pl.program_id(0)
    jc = pl.program_id(1)
    is_lat = i < n_lat_tiles
    cs = pl.ds(pl.multiple_of(jc * tc, tc), tc)

    @pl.when(jc == 0)
    def _():
        def rstd(x):
            return lax.rsqrt(jnp.mean(x * x, axis=-1, keepdims=True) + EPS)

        def body(r, carry):
            sl = pl.ds(pl.multiple_of(r * ROW_BLOCK, ROW_BLOCK), ROW_BLOCK)
            rs_scr[sl, :] = rstd(h_ref[sl, :])
            return carry

        lax.fori_loop(0, tm // ROW_BLOCK, body, 0)
        ra_scr[...] = rstd(ha_ref[...])
        rb_scr[...] = rstd(hb_ref[...])

    g = g_ref[...]
    sh = sh_ref[0]
    sc = sc_ref[0]

    def mod(x, rs):
        return (x * rs * g) * (1.0 + sc) + sh

    u = mod(h_ref[:, cs], rs_scr[...])
    t = lax.broadcasted_iota(jnp.int32, u.shape, 0)
    period = jnp.where(is_lat, GRID_W, c)
    tpos = t & (period - 1)

    def emit(shifted):
        xx = shifted - u
        for m in range(6):
            o_ref[m] = (u + xx * mu_ref[m]).astype(o_ref.dtype)

    use_prev = (jc == 0) | (jnp.logical_not(is_lat) & (jc == 1))
    use_next = (is_lat & (jc == 1)) | (jnp.logical_not(is_lat) & (jc >= 2))

    @pl.when(use_prev)
    def _():
        emit(jnp.where(tpos > 0, pltpu.roll(u, 1, 0), 0.0))

    @pl.when(use_next)
    def _():
        emit(jnp.where(tpos < period - 1, pltpu.roll(u, tm - 1, 0), 0.0))

    @pl.when(is_lat & (jc == 2))
    def _():
        first = (i * tm) % s == 0
        above = jnp.where(first, 0.0, mod(ha_ref[:, cs], ra_scr[...]))
        emit(jnp.concatenate([above, u[:tm - HALO]], axis=0))

    @pl.when(is_lat & (jc == 3))
    def _():
        lastt = ((i + 1) * tm) % s == 0
        below = jnp.where(lastt, 0.0, mod(hb_ref[:, cs], rb_scr[...]))
        emit(jnp.concatenate([u[HALO:], below], axis=0))


def _rwkv_prep(h, g, shift, scale, mu6, dims):
    r, d = h.shape
    s, c = dims["S"], dims["C"]
    tm = c
    assert tm % HALO == 0 and s % tm == 0 and tm & (tm - 1) == 0
    tc = d // 4
    hb = tm // HALO
    nblk = r // HALO
    n_lat_tiles = dims["RL"] // tm
    modrow = _mod_row_map(tm, n_lat_tiles, s, dims["B"])
    mod_spec = pl.BlockSpec((1, 1, tc), lambda i, j: (modrow(i), 0, j))
    vmem = 2 * tm * d * 4 + 4 * HALO * d * 4 + 2 * 6 * tm * tc * 2 + 12 * tm * tc * 4
    return pl.pallas_call(
        functools.partial(_rwkv_prep_kernel, tm=tm, tc=tc, n_lat_tiles=n_lat_tiles, s=s, c=c),
        grid=(r // tm, 4),
        in_specs=[pl.BlockSpec((tm, d), lambda i, j: (i, 0)),
                  pl.BlockSpec((HALO, d), lambda i, j: (jnp.maximum(i * hb - 1, 0), 0)),
                  pl.BlockSpec((HALO, d), lambda i, j: (jnp.minimum((i + 1) * hb, nblk - 1), 0)),
                  pl.BlockSpec((1, tc), lambda i, j: (0, j)),
                  mod_spec, mod_spec,
                  pl.BlockSpec((6, 1, tc), lambda i, j: (0, 0, j))],
        out_specs=pl.BlockSpec((6, tm, tc), lambda i, j: (0, i, j)),
        out_shape=jax.ShapeDtypeStruct((6, r, d), BF16),
        scratch_shapes=[pltpu.VMEM((tm, 1), F32), pltpu.VMEM((HALO, 1), F32), pltpu.VMEM((HALO, 1), F32)],
        compiler_params=_cparams(("parallel", "arbitrary"), vmem),
        name="rwkv_prep",
    )(h, h, h, g, shift, scale, mu6)


BMM_CAST_ROWS = 256


def _bmm_kernel(x_ref, w_ref, o_ref, w_scr):
    @pl.when(pl.program_id(2) == 0)
    def _():
        def body(c, carry):
            sl = pl.ds(pl.multiple_of(c * BMM_CAST_ROWS, BMM_CAST_ROWS), BMM_CAST_ROWS)
            w_scr[sl, :] = w_ref[0, sl, :].astype(w_scr.dtype)
            return carry

        lax.fori_loop(0, w_scr.shape[0] // BMM_CAST_ROWS, body, 0)

    o_ref[0] = _dot(x_ref[0], w_scr[...]).astype(o_ref.dtype)


def _bmm(x, w, lhs0, w0, ng, out_dtype, dims):
    _, k, n = w.shape
    r = x.shape[1]
    tm = _pow2_tile(512, dims["S"], dims["RC"])
    tn = _pow2_tile(1024, n)
    osz = jnp.dtype(out_dtype).itemsize
    wsz = jnp.dtype(w.dtype).itemsize
    assert k % BMM_CAST_ROWS == 0
    return pl.pallas_call(
        _bmm_kernel,
        grid=(ng, n // tn, r // tm),
        in_specs=[pl.BlockSpec((1, tm, k), lambda g, j, i: (lhs0 + g, i, 0)),
                  pl.BlockSpec((1, k, tn), lambda g, j, i: (w0 + g, 0, j))],
        out_specs=pl.BlockSpec((1, tm, tn), lambda g, j, i: (g, i, j)),
        out_shape=jax.ShapeDtypeStruct((ng, r, n), out_dtype),
        scratch_shapes=[pltpu.VMEM((k, tn), BF16)],
        compiler_params=_cparams(("parallel", "parallel", "arbitrary"),
                                 2 * tm * k * 2 + 2 * k * tn * wsz + k * tn * 2 + 2 * tm * tn * osz + tm * tn * 4),
        name="rwkv_bmm",
    )(x, w)


def _pair_ones():
    a = lax.broadcasted_iota(jnp.int32, (LANE, LANE), 0) // RWKV_N
    b = lax.broadcasted_iota(jnp.int32, (LANE, LANE), 1) // RWKV_N
    return (a == b).astype(BF16)


def _head_sums(x, ones):
    parts = [_dot(x[:, l:l + LANE].astype(BF16), ones) for l in range(0, x.shape[1], LANE)]
    return parts[0] if len(parts) == 1 else jnp.concatenate(parts, axis=1)


def _rwkv_mid_kernel(k_ref, hid_ref, w2_ref, a2_ref, g2_ref, w0_ref, a0_ref, kk_w_ref,
                     lw_ref, a_ref, kk_ref, gate_ref, *, lp):
    ones = _pair_ones()
    hw = hid_ref[0]
    ha = hid_ref[1]
    hg = hid_ref[2]
    for d in range(2):
        tw = jnp.tanh(hw[:, d * lp:(d + 1) * lp]).astype(BF16)
        z = w0_ref[d] + _dot(tw, w2_ref[d])
        lw_ref[d] = -math.exp(-0.5) * jax.nn.sigmoid(z)
        za = a0_ref[d] + _dot(ha[:, d * lp:(d + 1) * lp].astype(BF16), a2_ref[d])
        a_ref[d] = jax.nn.sigmoid(za).astype(a_ref.dtype)
    gate_ref[...] = _dot(jax.nn.sigmoid(hg).astype(BF16), g2_ref[...]).astype(gate_ref.dtype)
    kk = k_ref[0].astype(F32) * kk_w_ref[...]
    ss = _head_sums(kk * kk, ones)
    kk_ref[...] = (kk * lax.rsqrt(jnp.maximum(ss, 1e-24))).astype(kk_ref.dtype)


def _rwkv_mid(rkv, hid, w2, a2, g2, w0, a0, k_k, dims):
    _, r, d = rkv.shape
    lp = w2.shape[1]
    lw_tot = hid.shape[2]
    tm = _pow2_tile(256, dims["S"], dims["RC"])
    tc = _pow2_tile(1024, d)
    vmem = 2 * (tm * tc * 2 + 3 * tm * lw_tot * 4 + (4 * lp + g2.shape[0]) * tc * 2
                + tm * tc * (8 + 4 + 2 + 2)) + 10 * tm * tc * 4
    return pl.pallas_call(
        functools.partial(_rwkv_mid_kernel, lp=lp),
        grid=(r // tm, d // tc),
        in_specs=[pl.BlockSpec((1, tm, tc), lambda i, j: (1, i, j)),
                  pl.BlockSpec((3, tm, lw_tot), lambda i, j: (0, i, 0)),
                  pl.BlockSpec((2, lp, tc), lambda i, j: (0, 0, j)),
                  pl.BlockSpec((2, lp, tc), lambda i, j: (0, 0, j)),
                  pl.BlockSpec((g2.shape[0], tc), lambda i, j: (0, j)),
                  pl.BlockSpec((2, 1, tc), lambda i, j: (0, 0, j)),
                  pl.BlockSpec((2, 1, tc), lambda i, j: (0, 0, j)),
                  pl.BlockSpec((1, tc), lambda i, j: (0, j))],
        out_specs=[pl.BlockSpec((2, tm, tc), lambda i, j: (0, i, j)),
                   pl.BlockSpec((2, tm, tc), lambda i, j: (0, i, j)),
                   pl.BlockSpec((tm, tc), lambda i, j: (i, j)),
                   pl.BlockSpec((tm, tc), lambda i, j: (i, j))],
        out_shape=[jax.ShapeDtypeStruct((2, r, d), F32), jax.ShapeDtypeStruct((2, r, d), BF16),
                   jax.ShapeDtypeStruct((r, d), BF16), jax.ShapeDtypeStruct((r, d), BF16)],
        compiler_params=_cparams(("parallel", "arbitrary"), vmem),
        name="rwkv_mid",
    )(rkv, hid, w2, a2, g2, w0, a0, k_k)


def _rwkv_scan_kernel(r_ref, k_ref, v_ref, kk_ref, a_ref, lw_ref, ka_ref, y_ref, s_scr, *, rev, npairs):
    @pl.when(pl.program_id(2) == 0)
    def _():
        s_scr[...] = jnp.zeros_like(s_scr)

    ln = RWKV_CHUNK
    row = lax.broadcasted_iota(jnp.int32, (ln, LANE), 0)
    lane = lax.broadcasted_iota(jnp.int32, (ln, LANE), 1)
    col = lane % RWKV_N
    head0 = lane < RWKV_N
    strict = (col > row) if rev else (col < row)
    incl = (col >= row) if rev else (col <= row)
    eye = (col == row).astype(F32)

    def same_block(size):
        return (row // size) == (col // size)

    leaf = same_block(INV_LEAF)
    offs = [same_block(2 * sz) & jnp.logical_not(same_block(sz))
            for sz in (INV_LEAF << i for i in range(int(math.log2(ln // INV_LEAF))))]
    ti = lax.broadcasted_iota(jnp.int32, (ln, ln), 0)
    si = lax.broadcasted_iota(jnp.int32, (ln, ln), 1)
    tri = ((si >= ti) if rev else (si <= ti)).astype(BF16)
    ra = lax.broadcasted_iota(jnp.int32, (LANE, LANE), 0)
    rb = lax.broadcasted_iota(jnp.int32, (LANE, LANE), 1)
    same_head = (ra // RWKV_N) == (rb // RWKV_N)
    eye_full = ra == rb
    last = 0 if rev else ln - 1

    def bd(x):
        zero = jnp.zeros_like(x)
        return jnp.concatenate([jnp.where(head0, x, zero), jnp.where(head0, zero, x)], axis=0)

    def pm(a, b):
        return _dot(a.astype(BF16), bd(b.astype(BF16)))

    sls = [slice(p * LANE, (p + 1) * LANE) for p in range(npairs)]

    def prepare(sl):
        lw = lw_ref[0, :, sl]
        hi = lw.astype(BF16)
        lo = (lw - hi.astype(F32)).astype(BF16)
        cc = _dot(tri, jnp.concatenate([hi, lo], axis=1))
        c = cc[:, :LANE] + cc[:, LANE:]
        a = a_ref[0, :, sl].astype(F32)
        kk = kk_ref[:, sl].astype(F32)
        winv = jnp.exp(-c)
        kd = k_ref[0, :, sl].astype(F32) * (1.0 + (a - 1.0) * ka_ref[:, sl])
        at = (-(kk * jnp.exp(c - lw))).astype(BF16)
        bt = (kk * a * winv).astype(BF16)
        kt = (kd * winv).astype(BF16)
        rt = (r_ref[0, :, sl].astype(F32) * jnp.exp(c)).astype(BF16)
        ctot = jnp.sum(jnp.where(eye_full, c[last:last + 1, :], 0.0), axis=1, keepdims=True)
        return at, bt, kt, rt, jnp.exp(ctot)

    pairs = range(npairs)
    prep = [prepare(sl) for sl in sls]
    ar = [jnp.concatenate([at, rt], axis=0) for at, _, _, rt, _ in prep]
    res = [_dot_nt(lhs, jnp.concatenate([bd(bt), bd(kt)], axis=0)) for lhs, (_, bt, kt, _, _) in zip(ar, prep)]
    mab = [jnp.where(strict, x[:ln, :LANE], 0.0) for x in res]
    mrk = [jnp.concatenate([jnp.where(strict, x[:ln, LANE:], 0.0), jnp.where(incl, x[ln:, LANE:], 0.0)],
                           axis=0).astype(BF16) for x in res]
    rbm = [jnp.where(incl, x[ln:, :LANE], 0.0).astype(BF16) for x in res]

    md = [jnp.where(leaf, m, 0.0) for m in mab]
    sq = [pm(m, m) for m in md]
    tinv = [eye + m + pm(q, eye + m) for m, q in zip(md, sq)]
    for off in offs:
        half = [pm(t, jnp.where(off, m, 0.0)) for t, m in zip(tinv, mab)]
        tinv = [t + pm(hf, t) for t, hf in zip(tinv, half)]

    s0 = [s_scr[p] for p in pairs]
    s0b = [s.astype(BF16) for s in s0]
    xy = [_dot(lhs, sb) + _dot(mk, bd(v_ref[0, :, sl])) for lhs, sb, mk, sl in zip(ar, s0b, mrk, sls)]
    ub = [pm(t, q[:ln]).astype(BF16) for t, q in zip(tinv, xy)]
    for p in pairs:
        at, bt, kt, rt, wtot = prep[p]
        y = xy[p][ln:] + _dot(rbm[p], bd(ub[p]))
        y_ref[:, sls[p]] = y.astype(y_ref.dtype)
        upd = _dot_tn(jnp.concatenate([bt, kt], axis=0), jnp.concatenate([ub[p], v_ref[0, :, sls[p]]], axis=0))
        s_scr[p] = (s0[p] + jnp.where(same_head, upd, 0.0)) * wtot


def _rwkv_scan(rkv, kk, a, lw, k_a, dims, rev):
    _, r, d = rkv.shape
    npairs = min(32, d // LANE)
    pw = npairs * LANE
    ln = RWKV_CHUNK
    nc, _, rowblk = _chunk_maps(dims, ln, rev)
    dsel = 1 if rev else 0

    def spec3(lead):
        return pl.BlockSpec((1, ln, pw), lambda b, g, n: (lead, rowblk(b, n), g))

    return pl.pallas_call(
        functools.partial(_rwkv_scan_kernel, rev=rev, npairs=npairs),
        grid=(dims["B"], d // pw, nc),
        in_specs=[spec3(0), spec3(1), spec3(2),
                  pl.BlockSpec((ln, pw), lambda b, g, n: (rowblk(b, n), g)),
                  spec3(dsel), spec3(dsel),
                  pl.BlockSpec((1, pw), lambda b, g, n: (0, g))],
        out_specs=pl.BlockSpec((ln, pw), lambda b, g, n: (rowblk(b, n), g)),
        out_shape=jax.ShapeDtypeStruct((r, d), BF16),
        scratch_shapes=[pltpu.VMEM((npairs, LANE, LANE), F32)],
        compiler_params=_cparams(("parallel", "parallel", "arbitrary"),
                                 2 * ln * pw * (5 * 2 + 4 + 2) + npairs * LANE * LANE * 4 + (8 << 20)),
        name="rwkv_scan_bwd" if rev else "rwkv_scan_fwd",
    )(rkv, rkv, rkv, kk, a, lw, k_a)


def _rwkv_finish_kernel(yf_ref, yb_ref, r_ref, k_ref, v_ref, a_ref, gate_ref, ka_ref, rk_ref, lw_ref, lb_ref, z_ref):
    ones = _pair_ones()
    inv_n = 1.0 / RWKV_N
    y = yf_ref[...].astype(F32) + yb_ref[...].astype(F32)
    mean = _head_sums(y, ones) * inv_n
    yc = y - mean
    var = _head_sums(yc * yc, ones) * inv_n
    yn = yc * lax.rsqrt(var + LNX_EPS) * lw_ref[...] + lb_ref[...]
    k = k_ref[0].astype(F32)
    ka = ka_ref[...]
    kd0 = k * (1.0 + (a_ref[0].astype(F32) - 1.0) * ka)
    kd1 = k * (1.0 + (a_ref[1].astype(F32) - 1.0) * ka)
    kb = 0.5 * (kd0 + kd1)
    bonus = _head_sums(r_ref[0].astype(F32) * kb * rk_ref[...], ones) * v_ref[0].astype(F32)
    z_ref[...] = ((yn + bonus) * gate_ref[...].astype(F32)).astype(z_ref.dtype)


def _rwkv_finish(yf, yb, rkv, a, gate, k_a, r_k, lnx_w, lnx_b, dims, rows):
    d = yf.shape[1]
    tm = _pow2_tile(256, dims["S"], dims["RC"])
    tc = _pow2_tile(1024, d)
    blk = pl.BlockSpec((tm, tc), lambda i, j: (i, j))
    vec = pl.BlockSpec((1, tc), lambda i, j: (0, j))

    def lead(g):
        return pl.BlockSpec((1, tm, tc), lambda i, j: (g, i, j))

    return pl.pallas_call(
        _rwkv_finish_kernel,
        grid=(rows // tm, d // tc),
        in_specs=[blk, blk, lead(0), lead(1), lead(2),
                  pl.BlockSpec((2, tm, tc), lambda i, j: (0, i, j)), blk, vec, vec, vec, vec],
        out_specs=blk,
        out_shape=jax.ShapeDtypeStruct((rows, d), BF16),
        compiler_params=_cparams(("parallel", "parallel"), 2 * tm * tc * 2 * 9 + 16 * tm * tc * 4),
        name="rwkv_finish",
    )(yf, yb, rkv, rkv, rkv, a, gate, k_a, r_k, lnx_w, lnx_b)


def _even_layer(h, mod, norm_g, w_in, b_gate, conv_qk, gn_ret, gn_ml, w_out, dims, rows_out):
    d = h.shape[1]
    heads, nqk, nv = dims["H"], dims["NQK"], dims["NV"]
    n_main = 4 * nqk + 4 * nv
    n_gate = 4 * heads
    w_gate = jnp.pad(w_in[:, n_main:], ((0, 0), (0, LANE - n_gate)))
    bg = jnp.pad(b_gate, (0, LANE - n_gate)).reshape(1, LANE)
    shift, scale, gate = mod[0], mod[1], mod[2]
    p, gates = _inproj(h, norm_g[0].reshape(1, d), shift, scale, w_in, n_main, w_gate, bg, dims)
    post = jnp.concatenate([jnp.ones((nqk,), F32), jnp.full((nqk,), HEAD_DK ** -0.5, F32)]).reshape(1, 2 * nqk)
    qk = _conv_qk(p, conv_qk, post, dims)
    cos, sin = _rope_tables(dims)
    yrf = _retention(p, cos, sin, dims, False)
    yrb = _retention(p, cos, sin, dims, True)
    ymf = _mlstm(qk, p, gates, dims, False)
    ymb = _mlstm(qk, p, gates, dims, True)
    z = _merge(yrf, yrb, ymf, ymb, p, gn_ret.reshape(1, nv), gn_ml.reshape(1, nv), dims, rows_out)
    return _outproj(z, w_out, h, gate, norm_g[1].reshape(1, d), dims, rows_out)


def _rwkv_layer(h, mod, norm_g, mu, w_rkv, w_rkv0, w0, w1, w2, a0, a1, a2, g1, g2, k_k, k_a, r_k, lnx_w, lnx_b,
                w_out, dims, rows_out):
    d = h.shape[1]
    shift, scale, gate = mod[0], mod[1], mod[2]
    order = jnp.array([0, 2, 3, 1, 4, 5])
    mu6 = mu[order].reshape(6, 1, d)
    x6 = _rwkv_prep(h, norm_g[0].reshape(1, d), shift, scale, mu6, dims)
    rkv = _bmm(x6, w_rkv, 0, w_rkv0, 3, BF16, dims)
    lora = w1.shape[2]
    glora = g1.shape[1]
    lp = _round_up(lora, LANE)
    gp = _round_up(glora, LANE)
    lw_tot = max(2 * lp, gp)

    def two(w):
        w = jnp.pad(w, ((0, 0), (0, 0), (0, lp - lora)))
        w = jnp.concatenate([w[0], w[1]], axis=1)
        return jnp.pad(w, ((0, 0), (0, lw_tot - 2 * lp)))

    w_l1 = jnp.stack([two(w1), two(a1), jnp.pad(g1, ((0, 0), (0, lw_tot - glora)))]).astype(BF16)
    hid = _bmm(x6, w_l1, 3, 0, 3, F32, dims)
    pad_rows = lambda w, n: jnp.pad(w, ((0, 0),) * (w.ndim - 2) + ((0, n - w.shape[-2]), (0, 0))).astype(BF16)
    lw, a, kk, gmul = _rwkv_mid(rkv, hid, pad_rows(w2, lp), pad_rows(a2, lp), pad_rows(g2, lw_tot),
                                w0.reshape(2, 1, d), a0.reshape(2, 1, d), k_k.reshape(1, d), dims)
    ka = k_a.reshape(1, d)
    yf = _rwkv_scan(rkv, kk, a, lw, ka, dims, False)
    yb = _rwkv_scan(rkv, kk, a, lw, ka, dims, True)
    z = _rwkv_finish(yf, yb, rkv, a, gmul, ka, r_k.reshape(1, d), lnx_w.reshape(1, d), lnx_b.reshape(1, d),
                     dims, rows_out)
    return _outproj(z, w_out, h, gate, norm_g[1].reshape(1, d), dims, rows_out)


def kernel(x, c, ctx, c_ctx, ada_w, ada_b, norm_g, mlp_w_in, mlp_w_out, ev_w_in, ev_b_gate, ev_conv_qk,
           ev_gn_ret, ev_gn_mlstm, ev_w_out, od_mu, od_w_rkv, od_w0, od_w1, od_w2, od_a0, od_a1, od_a2,
           od_g1, od_g2, od_k_k, od_k_a, od_r_k, od_lnx_w, od_lnx_b, od_w_out):
    b, s, d = x.shape
    cl = ctx.shape[1]
    depth = ada_w.shape[0]
    heads = d // (2 * HEAD_DV)
    assert b + 1 <= MOD_ROWS and s % CHUNK == 0 and cl % CHUNK == 0 and s % GRID_W == 0
    dims = dict(B=b, S=s, C=cl, RL=b * s, RC=b * cl, H=heads, NQK=heads * HEAD_DK, NV=heads * HEAD_DV)
    rl, r = dims["RL"], dims["RL"] + dims["RC"]

    cond = jnp.concatenate([c, c_ctx[None, :], jnp.zeros((MOD_ROWS - b - 1, d), F32)], axis=0)
    tables = _ada_tables(cond, ada_w, ada_b)
    h = jnp.concatenate([x.reshape(rl, d), ctx.reshape(b * cl, d)], axis=0)

    for layer in range(depth):
        with_ctx = layer < depth - 1
        rows_out = r if with_ctx else rl
        mod = tables[layer].reshape(MOD_ROWS, 6, 1, d).transpose(1, 0, 2, 3)
        g = norm_g[layer]
        if layer % 2 == 0:
            e = layer // 2
            h_mix = _even_layer(h, mod[0:3], g[0:2], ev_w_in[e].astype(BF16), ev_b_gate[e], ev_conv_qk[e],
                                ev_gn_ret[e], ev_gn_mlstm[e], _to_bf16(ev_w_out, e), dims, rows_out)
        else:
            o = layer // 2
            w_rkv_all = od_w_rkv.reshape((-1,) + od_w_rkv.shape[2:])
            h_mix = _rwkv_layer(h, mod[0:3], g[0:2], od_mu[o], w_rkv_all, 3 * o, od_w0[o], od_w1[o], od_w2[o],
                                od_a0[o], od_a1[o], od_a2[o], od_g1[o], od_g2[o], od_k_k[o], od_k_a[o], od_r_k[o],
                                od_lnx_w[o], od_lnx_b[o], _to_bf16(od_w_out, o), dims, rows_out)
        h = _mlp(h_mix, g[2].reshape(1, d), mod[3], mod[4], _to_bf16(mlp_w_in, layer),
                 _to_bf16(mlp_w_out, layer), mod[5], g[3].reshape(1, d), dims, rows_out)
    return h[:rl].reshape(b, s, d)
```
